```python
import jax, jax.numpy as jnp
from jax import lax
import numpy as np

D_MODEL = 1024
BATCH = 4
SEQ = 4096
DEPTH = 2

POOL_WINDOWS = (2, 4, 8, 16)
POOL_WIDTH = 512
POOL_GROUP = POOL_WIDTH // len(POOL_WINDOWS)
N_HEADS = 8
HEAD_DIM = 64
ATTN_WIDTH = N_HEADS * HEAD_DIM
MOBA_BLOCK = 256
MOBA_TOP_K = 3
Q_CHUNK = 64
ROPE_THETA = 10000.0
N_BRANCH = 2
IN_WIDTH = POOL_WIDTH + 3 * ATTN_WIDTH + N_BRANCH * D_MODEL
D_FF = 2816
EPS = 1e-6
NEG = -1e30

kernel_name = "hybrid_pool_moba_macaron"


def rms_norm(x, g):
    xf = x.astype(jnp.float32)
    y = xf * lax.rsqrt(jnp.mean(xf * xf, axis=-1, keepdims=True) + EPS)
    return (y * g.astype(jnp.float32)).astype(x.dtype)


def swiglu(h, w_gate_up, w_down):
    a, b = jnp.split(h @ w_gate_up, 2, axis=-1)
    return (jax.nn.silu(a) * b) @ w_down


def rope(x):
    S, Dh = x.shape[1], x.shape[-1]
    half = Dh // 2
    inv_freq = 1.0 / (ROPE_THETA ** (jnp.arange(half, dtype=jnp.float32) * (2.0 / Dh)))
    ang = jnp.arange(S, dtype=jnp.float32)[:, None] * inv_freq[None, :]
    cos = jnp.cos(ang)[None, :, None, :]
    sin = jnp.sin(ang)[None, :, None, :]
    xf = x.astype(jnp.float32)
    x1, x2 = xf[..., :half], xf[..., half:]
    out = jnp.concatenate([x1 * cos - x2 * sin, x2 * cos + x1 * sin], axis=-1)
    return out.astype(x.dtype)


def multiscale_pool(u, w_grp, scale):
    B, S, _ = u.shape
    G = len(POOL_WINDOWS)
    uf = u.astype(jnp.float32).reshape(B, S, G, POOL_GROUP)
    csum = jnp.concatenate([jnp.zeros((B, 1, G, POOL_GROUP), jnp.float32),
                            jnp.cumsum(uf, axis=1)], axis=1)
    t = jnp.arange(S)
    means = []
    for g, w in enumerate(POOL_WINDOWS):
        lo = jnp.maximum(t + 1 - w, 0)
        cnt = (t + 1 - lo).astype(jnp.float32)
        means.append((csum[:, 1:, g] - csum[:, lo, g]) / cnt[None, :, None])
    pooled = jnp.stack(means, axis=2)
    d = (pooled - uf).astype(u.dtype)
    y = jnp.einsum('bsgc,gcd->bsgd', d, w_grp).reshape(B, S, POOL_WIDTH)
    return y * scale


def moba_attention(q, k, v):
    B, S, H, Dh = q.shape
    nb = -(-S // MOBA_BLOCK)
    pad = nb * MOBA_BLOCK - S
    kp = jnp.pad(k, ((0, 0), (0, pad), (0, 0), (0, 0)))
    vp = jnp.pad(v, ((0, 0), (0, pad), (0, 0), (0, 0)))
    kb = kp.reshape(B, nb, MOBA_BLOCK, H, Dh).transpose(0, 3, 1, 2, 4)
    vb = vp.reshape(B, nb, MOBA_BLOCK, H, Dh).transpose(0, 3, 1, 2, 4)
    kbar = jnp.mean(kb.astype(jnp.float32), axis=3)
    nc = S // Q_CHUNK
    qc = q.reshape(B, nc, Q_CHUNK, H, Dh).transpose(1, 0, 3, 2, 4)
    k_eff = min(MOBA_TOP_K, nb)
    scale = Dh ** -0.5
    bi = jnp.arange(B)[:, None, None, None]
    hi = jnp.arange(H)[None, :, None, None]
    blk_ids = jnp.arange(nb)

    def one_chunk(args):
        qi, ci = args
        q0 = ci * Q_CHUNK
        qblk = q0 // MOBA_BLOCK
        qpos = q0 + jnp.arange(Q_CHUNK)
        gate = jnp.einsum('bhqd,bhnd->bhqn', qi.astype(jnp.float32), kbar)
        gate = jnp.where(blk_ids < qblk, gate, NEG)
        _, sel = lax.top_k(gate, k_eff)
        valid = sel < qblk
        ksel = kb[bi, hi, sel]
        vsel = vb[bi, hi, sel]
        kown = lax.dynamic_index_in_dim(kb, qblk, axis=2, keepdims=False)
        vown = lax.dynamic_index_in_dim(vb, qblk, axis=2, keepdims=False)
        s_sel = jnp.einsum('bhqd,bhqknd->bhqkn', qi, ksel).astype(jnp.float32) * scale
        s_sel = jnp.where(valid[..., None], s_sel, NEG).reshape(B, H, Q_CHUNK, k_eff * MOBA_BLOCK)
        kpos = qblk * MOBA_BLOCK + jnp.arange(MOBA_BLOCK)
        s_own = jnp.einsum('bhqd,bhnd->bhqn', qi, kown).astype(jnp.float32) * scale
        s_own = jnp.where(kpos[None, :] <= qpos[:, None], s_own, NEG)
        p = jax.nn.softmax(jnp.concatenate([s_sel, s_own], axis=-1), axis=-1).astype(v.dtype)
        p_sel = p[..., :k_eff * MOBA_BLOCK].reshape(B, H, Q_CHUNK, k_eff, MOBA_BLOCK)
        p_own = p[..., k_eff * MOBA_BLOCK:]
        return (jnp.einsum('bhqkn,bhqknd->bhqd', p_sel, vsel)
                + jnp.einsum('bhqn,bhnd->bhqd', p_own, vown))

    out = lax.map(one_chunk, (qc, jnp.arange(nc)))
    return out.transpose(1, 0, 3, 2, 4).reshape(B, S, H * Dh)


def hybrid_layer(x, ffn1_norm, ffn1_w_gate_up, ffn1_w_down, mix_norm, w_in, b_gate,
                 pool_w, pool_scale, q_norm, k_norm, w_branch_pool, w_branch_attn, w_out,
                 ffn2_norm, ffn2_w_gate_up, ffn2_w_down):
    B, S, D = x.shape
    x = x + 0.5 * swiglu(rms_norm(x, ffn1_norm), ffn1_w_gate_up, ffn1_w_down)
    h = rms_norm(x, mix_norm)
    z = h @ w_in
    o1 = POOL_WIDTH
    o2 = o1 + ATTN_WIDTH
    o3 = o2 + ATTN_WIDTH
    o4 = o3 + ATTN_WIDTH
    u, q, k, v, gl = z[..., :o1], z[..., o1:o2], z[..., o2:o3], z[..., o3:o4], z[..., o4:]
    y_pool = multiscale_pool(u, pool_w, pool_scale)
    q = rope(rms_norm(q.reshape(B, S, N_HEADS, HEAD_DIM), q_norm))
    k = rope(rms_norm(k.reshape(B, S, N_HEADS, HEAD_DIM), k_norm))
    v = v.reshape(B, S, N_HEADS, HEAD_DIM)
    y_attn = moba_attention(q, k, v)
    gates = jax.nn.sigmoid(gl + b_gate).reshape(B, S, N_BRANCH, D)
    merged = gates[:, :, 0] * (y_pool @ w_branch_pool) + gates[:, :, 1] * (y_attn @ w_branch_attn)
    x = x + merged @ w_out
    x = x + 0.5 * swiglu(rms_norm(x, ffn2_norm), ffn2_w_gate_up, ffn2_w_down)
    return x


def setup_inputs(seed: int = 0) -> dict:
    key = jax.random.key(seed)
    ks = jax.random.split(key, 18)
    f32 = jnp.float32

    def w(k, shape, fan_in):
        return jax.random.normal(k, shape, f32) * (fan_in ** -0.5)

    def gain(k, shape):
        return 1.0 + 0.1 * jax.random.normal(k, shape, f32)

    L = DEPTH
    return {
        "x": jax.random.normal(ks[0], (BATCH, SEQ, D_MODEL), f32),
        "ffn1_norm": gain(ks[1], (L, D_MODEL)),
        "ffn1_w_gate_up": w(ks[2], (L, D_MODEL, 2 * D_FF), D_MODEL),
        "ffn1_w_down": w(ks[3], (L, D_FF, D_MODEL), D_FF),
        "mix_norm": gain(ks[4], (L, D_MODEL)),
        "w_in": w(ks[5], (L, D_MODEL, IN_WIDTH), D_MODEL),
        "b_gate": 0.1 * jax.random.normal(ks[6], (L, N_BRANCH * D_MODEL), f32),
        "pool_w": w(ks[7], (L, len(POOL_WINDOWS), POOL_GROUP, POOL_GROUP), POOL_GROUP),
        "pool_scale": gain(ks[8], (L, POOL_WIDTH)),
        "q_norm": gain(ks[9], (L, HEAD_DIM)),
        "k_norm": gain(ks[10], (L, HEAD_DIM)),
        "w_branch_pool": w(ks[11], (L, POOL_WIDTH, D_MODEL), POOL_WIDTH),
        "w_branch_attn": w(ks[12], (L, ATTN_WIDTH, D_MODEL), ATTN_WIDTH),
        "w_out": w(ks[13], (L, D_MODEL, D_MODEL), D_MODEL),
        "ffn2_norm": gain(ks[14], (L, D_MODEL)),
        "ffn2_w_gate_up": w(ks[15], (L, D_MODEL, 2 * D_FF), D_MODEL),
        "ffn2_w_down": w(ks[16], (L, D_FF, D_MODEL), D_FF),
    }


def reference(x, ffn1_norm, ffn1_w_gate_up, ffn1_w_down, mix_norm, w_in, b_gate,
              pool_w, pool_scale, q_norm, k_norm, w_branch_pool, w_branch_attn, w_out,
              ffn2_norm, ffn2_w_gate_up, ffn2_w_down):
    for l in range(DEPTH):
        x = hybrid_layer(x, ffn1_norm[l], ffn1_w_gate_up[l], ffn1_w_down[l], mix_norm[l],
                         w_in[l], b_gate[l], pool_w[l], pool_scale[l], q_norm[l], k_norm[l],
                         w_branch_pool[l], w_branch_attn[l], w_out[l],
                         ffn2_norm[l], ffn2_w_gate_up[l], ffn2_w_down[l])
    return x
```

```python
import functools

import jax
import jax.numpy as jnp
from jax import lax
from jax.experimental import pallas as pl
from jax.experimental.pallas import tpu as pltpu

F32 = jnp.float32
BF16 = jnp.bfloat16

D_MODEL = 1024
D_FF = 2816
POOL_WINDOWS = (2, 4, 8, 16)
POOL_WIDTH = 512
POOL_GROUP = 128
N_HEADS = 8
HEAD_DIM = 64
ATTN_WIDTH = 512
MOBA_BLOCK = 256
MOBA_TOP_K = 3
ROPE_THETA = 10000.0
EPS = 1e-6
NEG = -1e30

LANES = 128
HALO = 16
N_SLOTS = 16
FFN_TM = 512
FFN_TF = 256
MERGE_TM = 512
VMEM_LIMIT = 56 * 1024 * 1024


def _rms(x, g):
    ms = jnp.mean(x * x, axis=-1, keepdims=True)
    return x * lax.rsqrt(ms + EPS) * g


def _lane_iota(shape):
    return lax.broadcasted_iota(jnp.int32, shape, len(shape) - 1)


def _ffn_kernel(x_ref, g_ref, wgu_ref, wd_ref, o_ref):
    x = x_ref[...]
    h = _rms(x, g_ref[...]).astype(BF16)
    acc = jnp.zeros(x.shape, F32)
    for c in range(D_FF // FFN_TF):
        a = jnp.dot(h, wgu_ref[:, c * FFN_TF:(c + 1) * FFN_TF], preferred_element_type=F32)
        b = jnp.dot(h, wgu_ref[:, D_FF + c * FFN_TF:D_FF + (c + 1) * FFN_TF],
                    preferred_element_type=F32)
        act = (a * jax.nn.sigmoid(a) * b).astype(BF16)
        acc = acc + jnp.dot(act, wd_ref[c * FFN_TF:(c + 1) * FFN_TF, :],
                            preferred_element_type=F32)
    o_ref[...] = x + 0.5 * acc


def _ffn(x2d, g, wgu, wd):
    t = x2d.shape[0]
    return pl.pallas_call(
        _ffn_kernel,
        grid=(t // FFN_TM,),
        in_specs=[
            pl.BlockSpec((FFN_TM, D_MODEL), lambda i: (i, 0)),
            pl.BlockSpec((1, D_MODEL), lambda i: (0, 0)),
            pl.BlockSpec((D_MODEL, 2 * D_FF), lambda i: (0, 0), pipeline_mode=pl.Buffered(1)),
            pl.BlockSpec((D_FF, D_MODEL), lambda i: (0, 0), pipeline_mode=pl.Buffered(1)),
        ],
        out_specs=pl.BlockSpec((FFN_TM, D_MODEL), lambda i: (i, 0)),
        out_shape=jax.ShapeDtypeStruct((t, D_MODEL), F32),
        compiler_params=pltpu.CompilerParams(
            dimension_semantics=("arbitrary",), vmem_limit_bytes=VMEM_LIMIT),
        name="ffn",
    )(x2d, g, wgu, wd)


def _head_rms_rope(x, gmat, gain, cos, sin_signed):
    sq = x * x
    hi = sq.astype(BF16)
    lo = (sq - hi.astype(F32)).astype(BF16)
    ms = (jnp.dot(hi, gmat, preferred_element_type=F32)
          + jnp.dot(lo, gmat, preferred_element_type=F32))
    y = x * lax.rsqrt(ms + EPS) * gain
    outs = []
    first_half = (_lane_iota((x.shape[0], LANES)) % HEAD_DIM) < (HEAD_DIM // 2)
    for p in range(ATTN_WIDTH // LANES):
        yp = y[:, p * LANES:(p + 1) * LANES]
        rot = jnp.where(first_half,
                        pltpu.roll(yp, LANES - HEAD_DIM // 2, 1),
                        pltpu.roll(yp, HEAD_DIM // 2, 1))
        outs.append(yp * cos + rot * sin_signed)
    return outs


def _inproj_kernel(x_ref, mixg_ref, win_ref, bg_ref, poolw_ref, pools_ref, qg_ref, kg_ref,
                   cos_ref, sin_ref, gmat_ref, wbp_ref,
                   qa_ref, ka_ref, va_ref, p_ref, g1_ref,
                   ucat_ref, kbt_ref):
    i = pl.program_id(1)
    tm = x_ref.shape[1]
    x = x_ref[0]
    h = _rms(x, mixg_ref[...]).astype(BF16)

    @pl.when(i == 0)
    def _():
        ucat_ref[0:HALO, :] = jnp.zeros((HALO, POOL_WIDTH), F32)
        kbt_ref[...] = jnp.zeros(kbt_ref.shape, F32)

    u = jnp.dot(h, win_ref[:, 0:POOL_WIDTH], preferred_element_type=F32)
    ucat_ref[HALO:HALO + tm, :] = u
    pos = i * tm + lax.broadcasted_iota(jnp.int32, (tm, 1), 0)
    ys = []
    for g, w in enumerate(POOL_WINDOWS):
        lanes = slice(g * POOL_GROUP, (g + 1) * POOL_GROUP)
        wsum = u[:, lanes]
        for j in range(1, w):
            wsum = wsum + ucat_ref[HALO - j:HALO - j + tm, lanes]
        cnt = jnp.minimum(pos + 1, w).astype(F32)
        d = (wsum / cnt - u[:, lanes]).astype(BF16)
        ys.append(jnp.dot(d, poolw_ref[g], preferred_element_type=F32))
    y_pool = (jnp.concatenate(ys, axis=-1) * pools_ref[...]).astype(BF16)
    ucat_ref[0:HALO, :] = u[tm - HALO:, :]

    gl = jnp.dot(h, win_ref[:, POOL_WIDTH + 3 * ATTN_WIDTH:], preferred_element_type=F32)
    gates = jax.nn.sigmoid(gl + bg_ref[...])
    p_ref[0] = gates[:, :D_MODEL] * jnp.dot(y_pool, wbp_ref[...], preferred_element_type=F32)
    g1_ref[0] = gates[:, D_MODEL:]

    o1 = POOL_WIDTH
    q = jnp.dot(h, win_ref[:, o1:o1 + ATTN_WIDTH], preferred_element_type=F32)
    k = jnp.dot(h, win_ref[:, o1 + ATTN_WIDTH:o1 + 2 * ATTN_WIDTH], preferred_element_type=F32)
    v = jnp.dot(h, win_ref[:, o1 + 2 * ATTN_WIDTH:o1 + 3 * ATTN_WIDTH],
                preferred_element_type=F32)
    cos = cos_ref[...]
    sin_signed = sin_ref[...]
    q_tiles = _head_rms_rope(q, gmat_ref[...], qg_ref[...], cos, sin_signed)
    k_tiles = _head_rms_rope(k, gmat_ref[...], kg_ref[...], cos, sin_signed)
    qn = jnp.concatenate(q_tiles, axis=-1)
    kn = jnp.concatenate(k_tiles, axis=-1)

    gate = lax.dot_general(qn, kbt_ref[...], (((1,), (1,)), ((), ())),
                           precision=lax.Precision.HIGHEST, preferred_element_type=F32)
    lane = _lane_iota((tm, LANES))
    slot = lane % N_SLOTS
    gm = jnp.where(slot < i, gate, NEG)
    rank = jnp.zeros((tm, LANES), jnp.int32)
    for r in range(1, N_SLOTS):
        same_vreg_group = slot + r < N_SLOTS
        other = jnp.where(same_vreg_group,
                          pltpu.roll(gm, LANES - r, 1),
                          pltpu.roll(gm, N_SLOTS - r, 1))
        ahead = (other > gm) | ((other == gm) & jnp.logical_not(same_vreg_group))
        rank = rank + jnp.where(ahead, 1, 0)
    chosen = (rank < MOBA_TOP_K) & (slot < i)
    bias = jnp.where(chosen | (slot == i), 0.0, NEG)

    kbar = jnp.sum(kn, axis=0, keepdims=True) * (1.0 / MOBA_BLOCK)
    row = lax.broadcasted_iota(jnp.int32, kbt_ref.shape, 0)
    col = lax.broadcasted_iota(jnp.int32, kbt_ref.shape, 1)
    mine = (row % N_SLOTS == i) & (col // HEAD_DIM == row // N_SLOTS)
    kbt_ref[...] = jnp.where(mine, kbar, kbt_ref[...])

    for hd in range(N_HEADS):
        p, odd = hd // 2, hd % 2
        qp, kp = q_tiles[p], k_tiles[p]
        vp = v[:, p * LANES:(p + 1) * LANES]
        if odd:
            qp = pltpu.roll(qp, HEAD_DIM, 1)
            kp = pltpu.roll(kp, HEAD_DIM, 1)
        bias_h = pltpu.roll(bias, (HEAD_DIM - N_SLOTS * hd) % LANES, 1)
        in_bias = (lane >= HEAD_DIM) & (lane < HEAD_DIM + N_SLOTS)
        qa = jnp.where(lane < HEAD_DIM, qp * (HEAD_DIM ** -0.5), jnp.where(in_bias, bias_h, 0.0))
        ka = jnp.where(lane < HEAD_DIM, kp, jnp.where(lane == HEAD_DIM + i, 1.0, 0.0))
        if odd:
            va = jnp.where(lane >= HEAD_DIM, vp, jnp.where(lane == 0, 1.0, 0.0))
        else:
            va = jnp.where(lane < HEAD_DIM, vp, jnp.where(lane == HEAD_DIM, 1.0, 0.0))
        qa_ref[0, hd] = qa.astype(BF16)
        ka_ref[0, hd] = ka.astype(BF16)
        va_ref[0, hd] = va.astype(BF16)


def _inproj(x, mixg, win, bg, poolw, pools, qg, kg, cos, sin_signed, gmat, wbp):
    b, s, _ = x.shape
    tm = MOBA_BLOCK
    nt = s // tm
    const = lambda *shape: pl.BlockSpec(shape, lambda bi, i: (0,) * len(shape))
    head_spec = pl.BlockSpec((1, N_HEADS, tm, LANES), lambda bi, i: (bi, 0, i, 0))
    row_spec = pl.BlockSpec((1, tm, D_MODEL), lambda bi, i: (bi, i, 0))
    head_shape = jax.ShapeDtypeStruct((b, N_HEADS, s, LANES), BF16)
    row_shape = jax.ShapeDtypeStruct((b, s, D_MODEL), F32)
    return pl.pallas_call(
        _inproj_kernel,
        grid=(b, nt),
        in_specs=[
            row_spec,
            const(1, D_MODEL),
            const(*win.shape),
            const(1, 2 * D_MODEL),
            const(*poolw.shape),
            const(1, POOL_WIDTH),
            const(1, ATTN_WIDTH),
            const(1, ATTN_WIDTH),
            pl.BlockSpec((tm, LANES), lambda bi, i: (i, 0)),
            pl.BlockSpec((tm, LANES), lambda bi, i: (i, 0)),
            const(ATTN_WIDTH, ATTN_WIDTH),
            const(POOL_WIDTH, D_MODEL),
        ],
        out_specs=[head_spec, head_spec, head_spec, row_spec, row_spec],
        out_shape=[head_shape, head_shape, head_shape, row_shape, row_shape],
        scratch_shapes=[
            pltpu.VMEM((HALO + tm, POOL_WIDTH), F32),
            pltpu.VMEM((N_HEADS * N_SLOTS, ATTN_WIDTH), F32),
        ],
        compiler_params=pltpu.CompilerParams(
            dimension_semantics=("arbitrary", "arbitrary"), vmem_limit_bytes=VMEM_LIMIT),
        name="inproj",
    )(x, mixg, win, bg, poolw, pools, qg, kg, cos, sin_signed, gmat, wbp)


def _attn_kernel(q_ref, k_ref, v_ref, o_ref):
    i = pl.program_id(2)
    tq = q_ref.shape[2]
    lane = _lane_iota((tq, LANES))
    nt_dims = (((1,), (1,)), ((), ()))
    outs = []
    for hh in range(2):
        q = q_ref[0, hh]
        start = pl.multiple_of(i * tq, tq)
        kd = k_ref[0, hh, pl.ds(start, tq), :]
        vd = v_ref[0, hh, pl.ds(start, tq), :]
        s = lax.dot_general(q, kd, nt_dims, preferred_element_type=F32)
        r_id = lax.broadcasted_iota(jnp.int32, s.shape, 0)
        c_id = lax.broadcasted_iota(jnp.int32, s.shape, 1)
        s = jnp.where(c_id <= r_id, s, NEG)
        m = jnp.max(s, axis=-1, keepdims=True)
        pexp = jnp.exp(s - m)
        acc = jnp.dot(pexp.astype(BF16), vd, preferred_element_type=F32)

        def body(j, carry, q=q, hh=hh):
            m, acc = carry
            st = pl.multiple_of(j * tq, tq)
            kj = k_ref[0, hh, pl.ds(st, tq), :]
            vj = v_ref[0, hh, pl.ds(st, tq), :]
            s = lax.dot_general(q, kj, nt_dims, preferred_element_type=F32)
            m_new = jnp.maximum(m, jnp.max(s, axis=-1, keepdims=True))
            alpha = jnp.exp(m - m_new)
            pexp = jnp.exp(s - m_new)
            acc = alpha * acc + jnp.dot(pexp.astype(BF16), vj, preferred_element_type=F32)
            return m_new, acc

        m, acc = lax.fori_loop(0, i, body, (m, acc))
        ones_lane = 0 if hh else HEAD_DIM
        denom = jnp.sum(jnp.where(lane == ones_lane, acc, 0.0), axis=-1, keepdims=True)
        outs.append(acc / denom)
    o_ref[0] = jnp.where(lane < HEAD_DIM, outs[0], outs[1]).astype(o_ref.dtype)


def _attn(qa, ka, va):
    b, nh, s, _ = qa.shape
    tq = MOBA_BLOCK
    kv_spec = pl.BlockSpec((1, 2, s, LANES), lambda bi, p, i: (bi, p, 0, 0))
    return pl.pallas_call(
        _attn_kernel,
        grid=(b, nh // 2, s // tq),
        in_specs=[pl.BlockSpec((1, 2, tq, LANES), lambda bi, p, i: (bi, p, i, 0)), kv_spec, kv_spec],
        out_specs=pl.BlockSpec((1, tq, LANES), lambda bi, p, i: (bi, i, p)),
        out_shape=jax.ShapeDtypeStruct((b, s, ATTN_WIDTH), BF16),
        compiler_params=pltpu.CompilerParams(
            dimension_semantics=("arbitrary", "arbitrary", "arbitrary"),
            vmem_limit_bytes=VMEM_LIMIT),
        name="attn",
    )(qa, ka, va)


def _merge_kernel(x_ref, p_ref, g1_ref, ya_ref, wba_ref, wo_ref, o_ref):
    merged = p_ref[...] + g1_ref[...] * jnp.dot(ya_ref[...], wba_ref[...],
                                                preferred_element_type=F32)
    o_ref[...] = x_ref[...] + jnp.dot(merged.astype(BF16), wo_ref[...],
                                      preferred_element_type=F32)


def _merge(x2d, p2d, g12d, ya2d, wba, wo):
    t = x2d.shape[0]
    row = pl.BlockSpec((MERGE_TM, D_MODEL), lambda i: (i, 0))
    return pl.pallas_call(
        _merge_kernel,
        grid=(t // MERGE_TM,),
        in_specs=[
            row, row, row,
            pl.BlockSpec((MERGE_TM, ATTN_WIDTH), lambda i: (i, 0)),
            pl.BlockSpec((ATTN_WIDTH, D_MODEL), lambda i: (0, 0)),
            pl.BlockSpec((D_MODEL, D_MODEL), lambda i: (0, 0)),
        ],
        out_specs=row,
        out_shape=jax.ShapeDtypeStruct((t, D_MODEL), F32),
        compiler_params=pltpu.CompilerParams(
            dimension_semantics=("arbitrary",), vmem_limit_bytes=VMEM_LIMIT),
        name="merge",
    )(x2d, p2d, g12d, ya2d, wba, wo)


def _rope_tables(s):
    half = HEAD_DIM // 2
    inv_freq = 1.0 / (ROPE_THETA ** (jnp.arange(half, dtype=F32) * (2.0 / HEAD_DIM)))
    ang = jnp.arange(s, dtype=F32)[:, None] * inv_freq[None, :]
    cos, sin = jnp.cos(ang), jnp.sin(ang)
    cos = jnp.tile(cos, (1, LANES // half))
    sin_signed = jnp.tile(jnp.concatenate([-sin, sin], axis=-1), (1, LANES // HEAD_DIM))
    return cos, sin_signed


def kernel(x, ffn1_norm, ffn1_w_gate_up, ffn1_w_down, mix_norm, w_in, b_gate, pool_w, pool_scale,
           q_norm, k_norm, w_branch_pool, w_branch_attn, w_out, ffn2_norm, ffn2_w_gate_up,
           ffn2_w_down):
    b, s, d = x.shape
    depth = ffn1_norm.shape[0]
    cos, sin_signed = _rope_tables(s)
    head_of = jnp.arange(ATTN_WIDTH) // HEAD_DIM
    gmat = ((head_of[:, None] == head_of[None, :]).astype(F32) / HEAD_DIM).astype(BF16)
    x2d = x.reshape(b * s, d)
    for l in range(depth):
        x2d = _ffn(x2d, ffn1_norm[l][None], ffn1_w_gate_up[l].astype(BF16),
                   ffn1_w_down[l].astype(BF16))
        qa, ka, va, p, g1 = _inproj(
            x2d.reshape(b, s, d), mix_norm[l][None], w_in[l].astype(BF16), b_gate[l][None],
            pool_w[l].astype(BF16), pool_scale[l][None],
            jnp.tile(q_norm[l], N_HEADS)[None], jnp.tile(k_norm[l], N_HEADS)[None],
            cos, sin_signed, gmat, w_branch_pool[l].astype(BF16))
        ya = _attn(qa, ka, va)
        x2d = _merge(x2d, p.reshape(b * s, d), g1.reshape(b * s, d),
                     ya.reshape(b * s, ATTN_WIDTH), w_branch_attn[l].astype(BF16),
                     w_out[l].astype(BF16))
        x2d = _ffn(x2d, ffn2_norm[l][None], ffn2_w_gate_up[l].astype(BF16),
                   ffn2_w_down[l].astype(BF16))
    return x2d.reshape(b, s, d)
```

```python
import functools

import jax
import jax.numpy as jnp
from jax import lax
from jax.experimental import pallas as pl
from jax.experimental.pallas import tpu as pltpu

F32 = jnp.float32
BF16 = jnp.bfloat16

D_MODEL = 1024
D_FF = 2816
POOL_WINDOWS = (2, 4, 8, 16)
POOL_WIDTH = 512
POOL_GROUP = 128
N_HEADS = 8
HEAD_DIM = 64
ATTN_WIDTH = 512
MOBA_BLOCK = 256
MOBA_TOP_K = 3
ROPE_THETA = 10000.0
EPS = 1e-6
NEG = -1e30

LANES = 128
HALO = 16
N_SLOTS = 16
FFN_TM = 512
FFN_TF = 256
MERGE_TM = 512
ATTN_GROUP = 4
VMEM_LIMIT = 56 * 1024 * 1024


def _rms(x, g):
    ms = jnp.mean(x * x, axis=-1, keepdims=True)
    return x * lax.rsqrt(ms + EPS) * g


def _lane_iota(shape):
    return lax.broadcasted_iota(jnp.int32, shape, len(shape) - 1)


def _ffn_kernel(x_ref, g_ref, wgu_ref, wd_ref, o_ref):
    x = x_ref[...]
    h = _rms(x, g_ref[...]).astype(BF16)
    acc = jnp.zeros(x.shape, F32)
    for c in range(D_FF // FFN_TF):
        a = jnp.dot(h, wgu_ref[:, c * FFN_TF:(c + 1) * FFN_TF], preferred_element_type=F32)
        b = jnp.dot(h, wgu_ref[:, D_FF + c * FFN_TF:D_FF + (c + 1) * FFN_TF],
                    preferred_element_type=F32)
        act = (a * jax.nn.sigmoid(a) * b).astype(BF16)
        acc = acc + jnp.dot(act, wd_ref[c * FFN_TF:(c + 1) * FFN_TF, :],
                            preferred_element_type=F32)
    o_ref[...] = x + 0.5 * acc


def _ffn(x2d, g, wgu, wd):
    t = x2d.shape[0]
    return pl.pallas_call(
        _ffn_kernel,
        grid=(t // FFN_TM,),
        in_specs=[
            pl.BlockSpec((FFN_TM, D_MODEL), lambda i: (i, 0)),
            pl.BlockSpec((1, D_MODEL), lambda i: (0, 0)),
            pl.BlockSpec((D_MODEL, 2 * D_FF), lambda i: (0, 0), pipeline_mode=pl.Buffered(1)),
            pl.BlockSpec((D_FF, D_MODEL), lambda i: (0, 0), pipeline_mode=pl.Buffered(1)),
        ],
        out_specs=pl.BlockSpec((FFN_TM, D_MODEL), lambda i: (i, 0)),
        out_shape=jax.ShapeDtypeStruct((t, D_MODEL), F32),
        compiler_params=pltpu.CompilerParams(
            dimension_semantics=("arbitrary",), vmem_limit_bytes=VMEM_LIMIT),
        name="ffn",
    )(x2d, g, wgu, wd)


def _head_rms_rope(x, gmat, gain, cos, sin_signed):
    sq = x * x
    hi = sq.astype(BF16)
    lo = (sq - hi.astype(F32)).astype(BF16)
    ms = (jnp.dot(hi, gmat, preferred_element_type=F32)
          + jnp.dot(lo, gmat, preferred_element_type=F32))
    y = x * lax.rsqrt(ms + EPS) * gain
    outs = []
    first_half = (_lane_iota((x.shape[0], LANES)) % HEAD_DIM) < (HEAD_DIM // 2)
    for p in range(ATTN_WIDTH // LANES):
        yp = y[:, p * LANES:(p + 1) * LANES]
        rot = jnp.where(first_half,
                        pltpu.roll(yp, LANES - HEAD_DIM // 2, 1),
                        pltpu.roll(yp, HEAD_DIM // 2, 1))
        outs.append(yp * cos + rot * sin_signed)
    return outs


def _inproj_kernel(x_ref, mixg_ref, win_ref, bg_ref, poolw_ref, pools_ref, qg_ref, kg_ref,
                   cos_ref, sin_ref, gmat_ref, wbp_ref,
                   qa_ref, ka_ref, va_ref, p_ref, g1_ref,
                   ucat_ref, kbt_ref):
    i = pl.program_id(1)
    tm = x_ref.shape[1]
    x = x_ref[0]
    h = _rms(x, mixg_ref[...]).astype(BF16)

    @pl.when(i == 0)
    def _():
        ucat_ref[0:HALO, :] = jnp.zeros((HALO, POOL_WIDTH), F32)
        kbt_ref[...] = jnp.zeros(kbt_ref.shape, F32)

    u = jnp.dot(h, win_ref[:, 0:POOL_WIDTH], preferred_element_type=F32)
    ucat_ref[HALO:HALO + tm, :] = u
    pos = i * tm + lax.broadcasted_iota(jnp.int32, (tm, 1), 0)
    ys = []
    for g, w in enumerate(POOL_WINDOWS):
        lanes = slice(g * POOL_GROUP, (g + 1) * POOL_GROUP)
        wsum = u[:, lanes]
        for j in range(1, w):
            wsum = wsum + ucat_ref[HALO - j:HALO - j + tm, lanes]
        cnt = jnp.minimum(pos + 1, w).astype(F32)
        d = (wsum / cnt - u[:, lanes]).astype(BF16)
        ys.append(jnp.dot(d, poolw_ref[g], preferred_element_type=F32))
    y_pool = (jnp.concatenate(ys, axis=-1) * pools_ref[...]).astype(BF16)
    ucat_ref[0:HALO, :] = u[tm - HALO:, :]

    gl = jnp.dot(h, win_ref[:, POOL_WIDTH + 3 * ATTN_WIDTH:], preferred_element_type=F32)
    gates = jax.nn.sigmoid(gl + bg_ref[...])
    p_ref[0] = gates[:, :D_MODEL] * jnp.dot(y_pool, wbp_ref[...], preferred_element_type=F32)
    g1_ref[0] = gates[:, D_MODEL:]

    o1 = POOL_WIDTH
    q = jnp.dot(h, win_ref[:, o1:o1 + ATTN_WIDTH], preferred_element_type=F32)
    k = jnp.dot(h, win_ref[:, o1 + ATTN_WIDTH:o1 + 2 * ATTN_WIDTH], preferred_element_type=F32)
    v = jnp.dot(h, win_ref[:, o1 + 2 * ATTN_WIDTH:o1 + 3 * ATTN_WIDTH],
                preferred_element_type=F32)
    cos = cos_ref[...]
    sin_signed = sin_ref[...]
    q_tiles = _head_rms_rope(q, gmat_ref[...], qg_ref[...], cos, sin_signed)
    k_tiles = _head_rms_rope(k, gmat_ref[...], kg_ref[...], cos, sin_signed)
    qn = jnp.concatenate(q_tiles, axis=-1)
    kn = jnp.concatenate(k_tiles, axis=-1)

    gate = lax.dot_general(qn, kbt_ref[...], (((1,), (1,)), ((), ())),
                           precision=lax.Precision.HIGHEST, preferred_element_type=F32)
    lane = _lane_iota((tm, LANES))
    slot = lane % N_SLOTS
    gm = jnp.where(slot < i, gate, NEG)
    rank = jnp.zeros((tm, LANES), jnp.int32)
    for r in range(1, N_SLOTS):
        same_vreg_group = slot + r < N_SLOTS
        other = jnp.where(same_vreg_group,
                          pltpu.roll(gm, LANES - r, 1),
                          pltpu.roll(gm, N_SLOTS - r, 1))
        ahead = (other > gm) | ((other == gm) & jnp.logical_not(same_vreg_group))
        rank = rank + jnp.where(ahead, 1, 0)
    chosen = (rank < MOBA_TOP_K) & (slot < i)
    bias = jnp.where(chosen | (slot == i), 0.0, NEG)

    kbar = jnp.sum(kn, axis=0, keepdims=True) * (1.0 / MOBA_BLOCK)
    row = lax.broadcasted_iota(jnp.int32, kbt_ref.shape, 0)
    col = lax.broadcasted_iota(jnp.int32, kbt_ref.shape, 1)
    mine = (row % N_SLOTS == i) & (col // HEAD_DIM == row // N_SLOTS)
    kbt_ref[...] = jnp.where(mine, kbar, kbt_ref[...])

    for hd in range(N_HEADS):
        p, odd = hd // 2, hd % 2
        qp, kp = q_tiles[p], k_tiles[p]
        vp = v[:, p * LANES:(p + 1) * LANES]
        if odd:
            qp = pltpu.roll(qp, HEAD_DIM, 1)
            kp = pltpu.roll(kp, HEAD_DIM, 1)
        bias_h = pltpu.roll(bias, (HEAD_DIM - N_SLOTS * hd) % LANES, 1)
        in_bias = (lane >= HEAD_DIM) & (lane < HEAD_DIM + N_SLOTS)
        qa = jnp.where(lane < HEAD_DIM, qp * (HEAD_DIM ** -0.5), jnp.where(in_bias, bias_h, 0.0))
        ka = jnp.where(lane < HEAD_DIM, kp, jnp.where(lane == HEAD_DIM + i, 1.0, 0.0))
        if odd:
            va = jnp.where(lane >= HEAD_DIM, vp, jnp.where(lane == 0, 1.0, 0.0))
        else:
            va = jnp.where(lane < HEAD_DIM, vp, jnp.where(lane == HEAD_DIM, 1.0, 0.0))
        qa_ref[0, hd] = qa.astype(BF16)
        ka_ref[0, hd] = ka.astype(BF16)
        va_ref[0, hd] = va.astype(BF16)


def _inproj(x, mixg, win, bg, poolw, pools, qg, kg, cos, sin_signed, gmat, wbp):
    b, s, _ = x.shape
    tm = MOBA_BLOCK
    nt = s // tm
    const = lambda *shape: pl.BlockSpec(shape, lambda bi, i: (0,) * len(shape))
    head_spec = pl.BlockSpec((1, N_HEADS, tm, LANES), lambda bi, i: (bi, 0, i, 0))
    row_spec = pl.BlockSpec((1, tm, D_MODEL), lambda bi, i: (bi, i, 0))
    head_shape = jax.ShapeDtypeStruct((b, N_HEADS, s, LANES), BF16)
    row_shape = jax.ShapeDtypeStruct((b, s, D_MODEL), F32)
    return pl.pallas_call(
        _inproj_kernel,
        grid=(b, nt),
        in_specs=[
            row_spec,
            const(1, D_MODEL),
            const(*win.shape),
            const(1, 2 * D_MODEL),
            const(*poolw.shape),
            const(1, POOL_WIDTH),
            const(1, ATTN_WIDTH),
            const(1, ATTN_WIDTH),
            pl.BlockSpec((tm, LANES), lambda bi, i: (i, 0)),
            pl.BlockSpec((tm, LANES), lambda bi, i: (i, 0)),
            const(ATTN_WIDTH, ATTN_WIDTH),
            const(POOL_WIDTH, D_MODEL),
        ],
        out_specs=[head_spec, head_spec, head_spec, row_spec, row_spec],
        out_shape=[head_shape, head_shape, head_shape, row_shape, row_shape],
        scratch_shapes=[
            pltpu.VMEM((HALO + tm, POOL_WIDTH), F32),
            pltpu.VMEM((N_HEADS * N_SLOTS, ATTN_WIDTH), F32),
        ],
        compiler_params=pltpu.CompilerParams(
            dimension_semantics=("arbitrary", "arbitrary"), vmem_limit_bytes=VMEM_LIMIT),
        name="inproj",
    )(x, mixg, win, bg, poolw, pools, qg, kg, cos, sin_signed, gmat, wbp)


def _attn_tile_group(i, nk, q_ref, k_ref, v_ref, o_ref, s_ref):
    tq = q_ref.shape[2]
    lane = _lane_iota((tq, LANES))
    nt_dims = (((1,), (1,)), ((), ()))
    r_id = lax.broadcasted_iota(jnp.int32, (tq, tq), 0)
    c_id = lax.broadcasted_iota(jnp.int32, (tq, tq), 1)
    for hh in range(2):
        q = q_ref[0, hh]
        for j in range(nk):
            s_ref[hh, j] = lax.dot_general(q, k_ref[0, hh, j * tq:(j + 1) * tq, :], nt_dims,
                                           preferred_element_type=F32)
    outs = []
    for hh in range(2):
        s_ref[hh, i] = jnp.where(c_id <= r_id, s_ref[hh, i], NEG)
        mrun = s_ref[hh, 0]
        for j in range(1, nk):
            mrun = jnp.maximum(mrun, s_ref[hh, j])
        m = jnp.broadcast_to(jnp.max(mrun, axis=-1, keepdims=True), (tq, tq))
        p = jnp.concatenate([jnp.exp(s_ref[hh, j] - m).astype(BF16) for j in range(nk)], axis=-1)
        acc = jnp.dot(p, v_ref[0, hh, 0:nk * tq, :], preferred_element_type=F32)
        ones_lane = 0 if hh else HEAD_DIM
        denom = jnp.sum(jnp.where(lane == ones_lane, acc, 0.0), axis=-1, keepdims=True)
        outs.append(acc / denom)
    o_ref[0] = jnp.where(lane < HEAD_DIM, outs[0], outs[1]).astype(o_ref.dtype)


def _attn_kernel(q_ref, k_ref, v_ref, o_ref, s_ref):
    i = pl.program_id(2)
    for c in range(N_SLOTS // ATTN_GROUP):
        @pl.when(i // ATTN_GROUP == c)
        def _(c=c):
            _attn_tile_group(i, ATTN_GROUP * (c + 1), q_ref, k_ref, v_ref, o_ref, s_ref)


def _attn(qa, ka, va):
    b, nh, s, _ = qa.shape
    tq = MOBA_BLOCK
    kv_spec = pl.BlockSpec((1, 2, s, LANES), lambda bi, p, i: (bi, p, 0, 0))
    return pl.pallas_call(
        _attn_kernel,
        grid=(b, nh // 2, s // tq),
        in_specs=[pl.BlockSpec((1, 2, tq, LANES), lambda bi, p, i: (bi, p, i, 0)), kv_spec, kv_spec],
        out_specs=pl.BlockSpec((1, tq, LANES), lambda bi, p, i: (bi, i, p)),
        out_shape=jax.ShapeDtypeStruct((b, s, ATTN_WIDTH), BF16),
        scratch_shapes=[pltpu.VMEM((2, s // tq, tq, tq), F32)],
        compiler_params=pltpu.CompilerParams(
            dimension_semantics=("arbitrary", "arbitrary", "arbitrary"),
            vmem_limit_bytes=VMEM_LIMIT),
        name="attn",
    )(qa, ka, va)


def _merge_kernel(x_ref, p_ref, g1_ref, ya_ref, wba_ref, wo_ref, o_ref):
    merged = p_ref[...] + g1_ref[...] * jnp.dot(ya_ref[...], wba_ref[...],
                                                preferred_element_type=F32)
    o_ref[...] = x_ref[...] + jnp.dot(merged.astype(BF16), wo_ref[...],
                                      preferred_element_type=F32)


def _merge(x2d, p2d, g12d, ya2d, wba, wo):
    t = x2d.shape[0]
    row = pl.BlockSpec((MERGE_TM, D_MODEL), lambda i: (i, 0))
    return pl.pallas_call(
        _merge_kernel,
        grid=(t // MERGE_TM,),
        in_specs=[
            row, row, row,
            pl.BlockSpec((MERGE_TM, ATTN_WIDTH), lambda i: (i, 0)),
            pl.BlockSpec((ATTN_WIDTH, D_MODEL), lambda i: (0, 0)),
            pl.BlockSpec((D_MODEL, D_MODEL), lambda i: (0, 0)),
        ],
        out_specs=row,
        out_shape=jax.ShapeDtypeStruct((t, D_MODEL), F32),
        compiler_params=pltpu.CompilerParams(
            dimension_semantics=("arbitrary",), vmem_limit_bytes=VMEM_LIMIT),
        name="merge",
    )(x2d, p2d, g12d, ya2d, wba, wo)


def _rope_tables(s):
    half = HEAD_DIM // 2
    inv_freq = 1.0 / (ROPE_THETA ** (jnp.arange(half, dtype=F32) * (2.0 / HEAD_DIM)))
    ang = jnp.arange(s, dtype=F32)[:, None] * inv_freq[None, :]
    cos, sin = jnp.cos(ang), jnp.sin(ang)
    cos = jnp.tile(cos, (1, LANES // half))
    sin_signed = jnp.tile(jnp.concatenate([-sin, sin], axis=-1), (1, LANES // HEAD_DIM))
    return cos, sin_signed


def kernel(x, ffn1_norm, ffn1_w_gate_up, ffn1_w_down, mix_norm, w_in, b_gate, pool_w, pool_scale,
           q_norm, k_norm, w_branch_pool, w_branch_attn, w_out, ffn2_norm, ffn2_w_gate_up,
           ffn2_w_down):
    b, s, d = x.shape
    depth = ffn1_norm.shape[0]
    cos, sin_signed = _rope_tables(s)
    head_of = jnp.arange(ATTN_WIDTH) // HEAD_DIM
    gmat = ((head_of[:, None] == head_of[None, :]).astype(F32) / HEAD_DIM).astype(BF16)
    x2d = x.reshape(b * s, d)
    for l in range(depth):
        x2d = _ffn(x2d, ffn1_norm[l][None], ffn1_w_gate_up[l].astype(BF16),
                   ffn1_w_down[l].astype(BF16))
        qa, ka, va, p, g1 = _inproj(
            x2d.reshape(b, s, d), mix_norm[l][None], w_in[l].astype(BF16), b_gate[l][None],
            pool_w[l].astype(BF16), pool_scale[l][None],
            jnp.tile(q_norm[l], N_HEADS)[None], jnp.tile(k_norm[l], N_HEADS)[None],
            cos, sin_signed, gmat, w_branch_pool[l].astype(BF16))
        ya = _attn(qa, ka, va)
        x2d = _merge(x2d, p.reshape(b * s, d), g1.reshape(b * s, d),
                     ya.reshape(b * s, ATTN_WIDTH), w_branch_attn[l].astype(BF16),
                     w_out[l].astype(BF16))
        x2d = _ffn(x2d, ffn2_norm[l][None], ffn2_w_gate_up[l].astype(BF16),
                   ffn2_w_down[l].astype(BF16))
    return x2d.reshape(b, s, d)
```

```python
import functools

import jax
import jax.numpy as jnp
from jax import lax
from jax.experimental import pallas as pl
from jax.experimental.pallas import tpu as pltpu

F32 = jnp.float32
BF16 = jnp.bfloat16

D_MODEL = 1024
D_FF = 2816
POOL_WINDOWS = (2, 4, 8, 16)
POOL_WIDTH = 512
POOL_GROUP = 128
N_HEADS = 8
HEAD_DIM = 64
ATTN_WIDTH = 512
MOBA_BLOCK = 256
MOBA_TOP_K = 3
ROPE_THETA = 10000.0
EPS = 1e-6
NEG = -1e30

LANES = 128
HALO = 16
N_SLOTS = 16
FFN_TM = 512
FFN_TF = 256
MERGE_TM = 512
ATTN_GROUP = 4
VMEM_LIMIT = 56 * 1024 * 1024


def _rms(x, g):
    ms = jnp.mean(x * x, axis=-1, keepdims=True)
    return x * lax.rsqrt(ms + EPS) * g


def _lane_iota(shape):
    return lax.broadcasted_iota(jnp.int32, shape, len(shape) - 1)


def _ffn_kernel(x_ref, g_ref, wgu_ref, wd_ref, o_ref):
    x = x_ref[...]
    h = _rms(x, g_ref[...]).astype(BF16)
    acc = jnp.zeros(x.shape, F32)
    for c in range(D_FF // FFN_TF):
        a = jnp.dot(h, wgu_ref[:, c * FFN_TF:(c + 1) * FFN_TF], preferred_element_type=F32)
        b = jnp.dot(h, wgu_ref[:, D_FF + c * FFN_TF:D_FF + (c + 1) * FFN_TF],
                    preferred_element_type=F32)
        act = (a * jax.nn.sigmoid(a) * b).astype(BF16)
        acc = acc + jnp.dot(act, wd_ref[c * FFN_TF:(c + 1) * FFN_TF, :],
                            preferred_element_type=F32)
    o_ref[...] = x + 0.5 * acc


def _ffn(x2d, g, wgu, wd):
    t = x2d.shape[0]
    return pl.pallas_call(
        _ffn_kernel,
        grid=(t // FFN_TM,),
        in_specs=[
            pl.BlockSpec((FFN_TM, D_MODEL), lambda i: (i, 0)),
            pl.BlockSpec((1, D_MODEL), lambda i: (0, 0)),
            pl.BlockSpec((D_MODEL, 2 * D_FF), lambda i: (0, 0), pipeline_mode=pl.Buffered(1)),
            pl.BlockSpec((D_FF, D_MODEL), lambda i: (0, 0), pipeline_mode=pl.Buffered(1)),
        ],
        out_specs=pl.BlockSpec((FFN_TM, D_MODEL), lambda i: (i, 0)),
        out_shape=jax.ShapeDtypeStruct((t, D_MODEL), F32),
        compiler_params=pltpu.CompilerParams(
            dimension_semantics=("arbitrary",), vmem_limit_bytes=VMEM_LIMIT),
        name="ffn",
    )(x2d, g, wgu, wd)


def _head_rms_rope(x, gmat, gain, cos, sin_signed):
    sq = x * x
    hi = sq.astype(BF16)
    lo = (sq - hi.astype(F32)).astype(BF16)
    ms = (jnp.dot(hi, gmat, preferred_element_type=F32)
          + jnp.dot(lo, gmat, preferred_element_type=F32))
    y = x * lax.rsqrt(ms + EPS) * gain
    outs = []
    first_half = (_lane_iota((x.shape[0], LANES)) % HEAD_DIM) < (HEAD_DIM // 2)
    for p in range(ATTN_WIDTH // LANES):
        yp = y[:, p * LANES:(p + 1) * LANES]
        rot = jnp.where(first_half,
                        pltpu.roll(yp, LANES - HEAD_DIM // 2, 1),
                        pltpu.roll(yp, HEAD_DIM // 2, 1))
        outs.append(yp * cos + rot * sin_signed)
    return outs


def _inproj_kernel(x_ref, mixg_ref, win_ref, bg_ref, poolw_ref, pools_ref, qg_ref, kg_ref,
                   cos_ref, sin_ref, gmat_ref, wbp_ref,
                   qa_ref, ka_ref, va_ref, p_ref, g1_ref,
                   ucat_ref, kbt_ref):
    i = pl.program_id(1)
    tm = x_ref.shape[1]
    x = x_ref[0]
    h = _rms(x, mixg_ref[...]).astype(BF16)

    @pl.when(i == 0)
    def _():
        ucat_ref[0:HALO, :] = jnp.zeros((HALO, POOL_WIDTH), F32)
        kbt_ref[...] = jnp.zeros(kbt_ref.shape, F32)

    u = jnp.dot(h, win_ref[:, 0:POOL_WIDTH], preferred_element_type=F32)
    ucat_ref[HALO:HALO + tm, :] = u
    pos = i * tm + lax.broadcasted_iota(jnp.int32, (tm, 1), 0)
    ys = []
    for g, w in enumerate(POOL_WINDOWS):
        lanes = slice(g * POOL_GROUP, (g + 1) * POOL_GROUP)
        wsum = u[:, lanes]
        for j in range(1, w):
            wsum = wsum + ucat_ref[HALO - j:HALO - j + tm, lanes]
        cnt = jnp.minimum(pos + 1, w).astype(F32)
        d = (wsum / cnt - u[:, lanes]).astype(BF16)
        ys.append(jnp.dot(d, poolw_ref[g], preferred_element_type=F32))
    y_pool = (jnp.concatenate(ys, axis=-1) * pools_ref[...]).astype(BF16)
    ucat_ref[0:HALO, :] = u[tm - HALO:, :]

    gl = jnp.dot(h, win_ref[:, POOL_WIDTH + 3 * ATTN_WIDTH:], preferred_element_type=F32)
    gates = jax.nn.sigmoid(gl + bg_ref[...])
    p_ref[0] = gates[:, :D_MODEL] * jnp.dot(y_pool, wbp_ref[...], preferred_element_type=F32)
    g1_ref[0] = gates[:, D_MODEL:]

    o1 = POOL_WIDTH
    q = jnp.dot(h, win_ref[:, o1:o1 + ATTN_WIDTH], preferred_element_type=F32)
    k = jnp.dot(h, win_ref[:, o1 + ATTN_WIDTH:o1 + 2 * ATTN_WIDTH], preferred_element_type=F32)
    v = jnp.dot(h, win_ref[:, o1 + 2 * ATTN_WIDTH:o1 + 3 * ATTN_WIDTH],
                preferred_element_type=F32)
    cos = cos_ref[...]
    sin_signed = sin_ref[...]
    q_tiles = _head_rms_rope(q, gmat_ref[...], qg_ref[...], cos, sin_signed)
    k_tiles = _head_rms_rope(k, gmat_ref[...], kg_ref[...], cos, sin_signed)
    qn = jnp.concatenate(q_tiles, axis=-1)
    kn = jnp.concatenate(k_tiles, axis=-1)

    gate = lax.dot_general(qn, kbt_ref[...], (((1,), (1,)), ((), ())),
                           precision=lax.Precision.HIGHEST, preferred_element_type=F32)
    lane = _lane_iota((tm, LANES))
    slot = lane % N_SLOTS
    gm = jnp.where(slot < i, gate, NEG)
    rank = jnp.zeros((tm, LANES), jnp.int32)
    for r in range(1, N_SLOTS):
        same_vreg_group = slot + r < N_SLOTS
        other = jnp.where(same_vreg_group,
                          pltpu.roll(gm, LANES - r, 1),
                          pltpu.roll(gm, N_SLOTS - r, 1))
        ahead = (other > gm) | ((other == gm) & jnp.logical_not(same_vreg_group))
        rank = rank + jnp.where(ahead, 1, 0)
    chosen = (rank < MOBA_TOP_K) & (slot < i)
    bias = jnp.where(chosen | (slot == i), 0.0, NEG)

    kbar = jnp.sum(kn, axis=0, keepdims=True) * (1.0 / MOBA_BLOCK)
    row = lax.broadcasted_iota(jnp.int32, kbt_ref.shape, 0)
    col = lax.broadcasted_iota(jnp.int32, kbt_ref.shape, 1)
    mine = (row % N_SLOTS == i) & (col // HEAD_DIM == row // N_SLOTS)
    kbt_ref[...] = jnp.where(mine, kbar, kbt_ref[...])

    for hd in range(N_HEADS):
        p, odd = hd // 2, hd % 2
        qp, kp = q_tiles[p], k_tiles[p]
        vp = v[:, p * LANES:(p + 1) * LANES]
        if odd:
            qp = pltpu.roll(qp, HEAD_DIM, 1)
            kp = pltpu.roll(kp, HEAD_DIM, 1)
        bias_h = pltpu.roll(bias, (HEAD_DIM - N_SLOTS * hd) % LANES, 1)
        in_bias = (lane >= HEAD_DIM) & (lane < HEAD_DIM + N_SLOTS)
        qa = jnp.where(lane < HEAD_DIM, qp * (HEAD_DIM ** -0.5), jnp.where(in_bias, bias_h, 0.0))
        ka = jnp.where(lane < HEAD_DIM, kp, jnp.where(lane == HEAD_DIM + i, 1.0, 0.0))
        if odd:
            va = jnp.where(lane >= HEAD_DIM, vp, jnp.where(lane == 0, 1.0, 0.0))
        else:
            va = jnp.where(lane < HEAD_DIM, vp, jnp.where(lane == HEAD_DIM, 1.0, 0.0))
        qa_ref[0, hd] = qa.astype(BF16)
        ka_ref[0, hd] = ka.astype(BF16)
        va_ref[0, hd] = va.astype(BF16)


def _inproj(x, mixg, win, bg, poolw, pools, qg, kg, cos, sin_signed, gmat, wbp):
    b, s, _ = x.shape
    tm = MOBA_BLOCK
    nt = s // tm
    const = lambda *shape: pl.BlockSpec(shape, lambda bi, i: (0,) * len(shape))
    head_spec = pl.BlockSpec((1, N_HEADS, tm, LANES), lambda bi, i: (bi, 0, i, 0))
    row_spec = pl.BlockSpec((1, tm, D_MODEL), lambda bi, i: (bi, i, 0))
    head_shape = jax.ShapeDtypeStruct((b, N_HEADS, s, LANES), BF16)
    row_shape = jax.ShapeDtypeStruct((b, s, D_MODEL), F32)
    return pl.pallas_call(
        _inproj_kernel,
        grid=(b, nt),
        in_specs=[
            row_spec,
            const(1, D_MODEL),
            const(*win.shape),
            const(1, 2 * D_MODEL),
            const(*poolw.shape),
            const(1, POOL_WIDTH),
            const(1, ATTN_WIDTH),
            const(1, ATTN_WIDTH),
            pl.BlockSpec((tm, LANES), lambda bi, i: (i, 0)),
            pl.BlockSpec((tm, LANES), lambda bi, i: (i, 0)),
            const(ATTN_WIDTH, ATTN_WIDTH),
            const(POOL_WIDTH, D_MODEL),
        ],
        out_specs=[head_spec, head_spec, head_spec, row_spec, row_spec],
        out_shape=[head_shape, head_shape, head_shape, row_shape, row_shape],
        scratch_shapes=[
            pltpu.VMEM((HALO + tm, POOL_WIDTH), F32),
            pltpu.VMEM((N_HEADS * N_SLOTS, ATTN_WIDTH), F32),
        ],
        compiler_params=pltpu.CompilerParams(
            dimension_semantics=("arbitrary", "arbitrary"), vmem_limit_bytes=VMEM_LIMIT),
        name="inproj",
    )(x, mixg, win, bg, poolw, pools, qg, kg, cos, sin_signed, gmat, wbp)


def _attn_tile_group(i, nk, q_ref, k_ref, vt_ref, o_ref, s0_ref, s1_ref):
    tq = q_ref.shape[2]
    nt_dims = (((1,), (1,)), ((), ()))
    rel = (lax.broadcasted_iota(jnp.int32, (tq, tq), 0)
           - lax.broadcasted_iota(jnp.int32, (tq, tq), 1))
    s_refs = (s0_ref, s1_ref)
    for hh in range(2):
        q = q_ref[0, hh]
        for j in range(nk):
            s = lax.dot_general(k_ref[0, hh, j * tq:(j + 1) * tq, :], q, nt_dims,
                                preferred_element_type=F32)
            if j >= nk - ATTN_GROUP:
                s = jnp.where(rel <= (i - j) * tq, s, NEG)
            s_refs[hh][j] = s
    outs = []
    for hh in range(2):
        s_ref = s_refs[hh]
        mrun = s_ref[0]
        for j in range(1, nk):
            mrun = jnp.maximum(mrun, s_ref[j])
        m = jnp.broadcast_to(jnp.max(mrun, axis=0, keepdims=True), (tq, tq))
        pt = jnp.concatenate([jnp.exp(s_ref[j] - m).astype(BF16) for j in range(nk)], axis=0)
        acc = jnp.dot(vt_ref[0, hh, :, 0:nk * tq], pt, preferred_element_type=F32)
        ones_row = 0 if hh else HEAD_DIM
        outs.append(acc / acc[ones_row:ones_row + 1, :])
    row = lax.broadcasted_iota(jnp.int32, (LANES, tq), 0)
    o_ref[0] = jnp.where(row < HEAD_DIM, outs[0], outs[1]).T.astype(o_ref.dtype)


def _attn_kernel(q_ref, k_ref, vt_ref, o_ref, s0_ref, s1_ref):
    i = pl.program_id(2)
    for c in range(N_SLOTS // ATTN_GROUP):
        @pl.when(i // ATTN_GROUP == c)
        def _(c=c):
            _attn_tile_group(i, ATTN_GROUP * (c + 1), q_ref, k_ref, vt_ref, o_ref, s0_ref, s1_ref)


def _attn(qa, ka, vat):
    b, nh, s, _ = qa.shape
    tq = MOBA_BLOCK
    k_spec = pl.BlockSpec((1, 2, s, LANES), lambda bi, p, i: (bi, p, 0, 0))
    vt_spec = pl.BlockSpec((1, 2, LANES, s), lambda bi, p, i: (bi, p, 0, 0))
    return pl.pallas_call(
        _attn_kernel,
        grid=(b, nh // 2, s // tq),
        in_specs=[pl.BlockSpec((1, 2, tq, LANES), lambda bi, p, i: (bi, p, i, 0)), k_spec, vt_spec],
        out_specs=pl.BlockSpec((1, tq, LANES), lambda bi, p, i: (bi, i, p)),
        out_shape=jax.ShapeDtypeStruct((b, s, ATTN_WIDTH), BF16),
        scratch_shapes=[pltpu.VMEM((s // tq, tq, tq), F32)] * 2,
        compiler_params=pltpu.CompilerParams(
            dimension_semantics=("arbitrary", "arbitrary", "arbitrary"),
            vmem_limit_bytes=VMEM_LIMIT),
        name="attn",
    )(qa, ka, vat)


def _merge_kernel(x_ref, p_ref, g1_ref, ya_ref, wba_ref, wo_ref, o_ref):
    merged = p_ref[...] + g1_ref[...] * jnp.dot(ya_ref[...], wba_ref[...],
                                                preferred_element_type=F32)
    o_ref[...] = x_ref[...] + jnp.dot(merged.astype(BF16), wo_ref[...],
                                      preferred_element_type=F32)


def _merge(x2d, p2d, g12d, ya2d, wba, wo):
    t = x2d.shape[0]
    row = pl.BlockSpec((MERGE_TM, D_MODEL), lambda i: (i, 0))
    return pl.pallas_call(
        _merge_kernel,
        grid=(t // MERGE_TM,),
        in_specs=[
            row, row, row,
            pl.BlockSpec((MERGE_TM, ATTN_WIDTH), lambda i: (i, 0)),
            pl.BlockSpec((ATTN_WIDTH, D_MODEL), lambda i: (0, 0)),
            pl.BlockSpec((D_MODEL, D_MODEL), lambda i: (0, 0)),
        ],
        out_specs=row,
        out_shape=jax.ShapeDtypeStruct((t, D_MODEL), F32),
        compiler_params=pltpu.CompilerParams(
            dimension_semantics=("arbitrary",), vmem_limit_bytes=VMEM_LIMIT),
        name="merge",
    )(x2d, p2d, g12d, ya2d, wba, wo)


def _rope_tables(s):
    half = HEAD_DIM // 2
    inv_freq = 1.0 / (ROPE_THETA ** (jnp.arange(half, dtype=F32) * (2.0 / HEAD_DIM)))
    ang = jnp.arange(s, dtype=F32)[:, None] * inv_freq[None, :]
    cos, sin = jnp.cos(ang), jnp.sin(ang)
    cos = jnp.tile(cos, (1, LANES // half))
    sin_signed = jnp.tile(jnp.concatenate([-sin, sin], axis=-1), (1, LANES // HEAD_DIM))
    return cos, sin_signed


def kernel(x, ffn1_norm, ffn1_w_gate_up, ffn1_w_down, mix_norm, w_in, b_gate, pool_w, pool_scale,
           q_norm, k_norm, w_branch_pool, w_branch_attn, w_out, ffn2_norm, ffn2_w_gate_up,
           ffn2_w_down):
    b, s, d = x.shape
    depth = ffn1_norm.shape[0]
    cos, sin_signed = _rope_tables(s)
    head_of = jnp.arange(ATTN_WIDTH) // HEAD_DIM
    gmat = ((head_of[:, None] == head_of[None, :]).astype(F32) / HEAD_DIM).astype(BF16)
    x2d = x.reshape(b * s, d)
    for l in range(depth):
        x2d = _ffn(x2d, ffn1_norm[l][None], ffn1_w_gate_up[l].astype(BF16),
                   ffn1_w_down[l].astype(BF16))
        qa, ka, va, p, g1 = _inproj(
            x2d.reshape(b, s, d), mix_norm[l][None], w_in[l].astype(BF16), b_gate[l][None],
            pool_w[l].astype(BF16), pool_scale[l][None],
            jnp.tile(q_norm[l], N_HEADS)[None], jnp.tile(k_norm[l], N_HEADS)[None],
            cos, sin_signed, gmat, w_branch_pool[l].astype(BF16))
        ya = _attn(qa, ka, jnp.swapaxes(va, 2, 3))
        x2d = _merge(x2d, p.reshape(b * s, d), g1.reshape(b * s, d),
                     ya.reshape(b * s, ATTN_WIDTH), w_branch_attn[l].astype(BF16),
                     w_out[l].astype(BF16))
        x2d = _ffn(x2d, ffn2_norm[l][None], ffn2_w_gate_up[l].astype(BF16),
                   ffn2_w_down[l].astype(BF16))
    return x2d.reshape(b, s, d)
```

```python
import functools

import jax
import jax.numpy as jnp
from jax import lax
from jax.experimental import pallas as pl
from jax.experimental.pallas import tpu as pltpu

F32 = jnp.float32
BF16 = jnp.bfloat16

D_MODEL = 1024
D_FF = 2816
POOL_WINDOWS = (2, 4, 8, 16)
POOL_WIDTH = 512
POOL_GROUP = 128
N_HEADS = 8
HEAD_DIM = 64
ATTN_WIDTH = 512
MOBA_BLOCK = 256
MOBA_TOP_K = 3
ROPE_THETA = 10000.0
EPS = 1e-6
NEG = -1e30

LANES = 128
HALO = 16
N_SLOTS = 16
FFN_TM = 512
FFN_TF = 256
MERGE_TM = 512
ATTN_GROUP = 2
ATTN_HEADS = 4
VMEM_LIMIT = 56 * 1024 * 1024


def _rms(x, g):
    ms = jnp.mean(x * x, axis=-1, keepdims=True)
    return x * lax.rsqrt(ms + EPS) * g


def _lane_iota(shape):
    return lax.broadcasted_iota(jnp.int32, shape, len(shape) - 1)


def _ffn_kernel(x_ref, g_ref, wgu_ref, wd_ref, o_ref):
    x = x_ref[...]
    h = _rms(x, g_ref[...]).astype(BF16)
    acc = jnp.zeros(x.shape, F32)
    for c in range(D_FF // FFN_TF):
        a = jnp.dot(h, wgu_ref[:, c * FFN_TF:(c + 1) * FFN_TF], preferred_element_type=F32)
        b = jnp.dot(h, wgu_ref[:, D_FF + c * FFN_TF:D_FF + (c + 1) * FFN_TF],
                    preferred_element_type=F32)
        act = (a * jax.nn.sigmoid(a) * b).astype(BF16)
        acc = acc + jnp.dot(act, wd_ref[c * FFN_TF:(c + 1) * FFN_TF, :],
                            preferred_element_type=F32)
    o_ref[...] = x + 0.5 * acc


def _ffn(x2d, g, wgu, wd):
    t = x2d.shape[0]
    return pl.pallas_call(
        _ffn_kernel,
        grid=(t // FFN_TM,),
        in_specs=[
            pl.BlockSpec((FFN_TM, D_MODEL), lambda i: (i, 0)),
            pl.BlockSpec((1, D_MODEL), lambda i: (0, 0)),
            pl.BlockSpec((D_MODEL, 2 * D_FF), lambda i: (0, 0), pipeline_mode=pl.Buffered(1)),
            pl.BlockSpec((D_FF, D_MODEL), lambda i: (0, 0), pipeline_mode=pl.Buffered(1)),
        ],
        out_specs=pl.BlockSpec((FFN_TM, D_MODEL), lambda i: (i, 0)),
        out_shape=jax.ShapeDtypeStruct((t, D_MODEL), F32),
        compiler_params=pltpu.CompilerParams(
            dimension_semantics=("arbitrary",), vmem_limit_bytes=VMEM_LIMIT),
        name="ffn",
    )(x2d, g, wgu, wd)


def _head_rms_rope(x, gmat, gain, cos, sin_signed):
    sq = x * x
    hi = sq.astype(BF16)
    lo = (sq - hi.astype(F32)).astype(BF16)
    ms = (jnp.dot(hi, gmat, preferred_element_type=F32)
          + jnp.dot(lo, gmat, preferred_element_type=F32))
    y = x * lax.rsqrt(ms + EPS) * gain
    outs = []
    first_half = (_lane_iota((x.shape[0], LANES)) % HEAD_DIM) < (HEAD_DIM // 2)
    for p in range(ATTN_WIDTH // LANES):
        yp = y[:, p * LANES:(p + 1) * LANES]
        rot = jnp.where(first_half,
                        pltpu.roll(yp, LANES - HEAD_DIM // 2, 1),
                        pltpu.roll(yp, HEAD_DIM // 2, 1))
        outs.append(yp * cos + rot * sin_signed)
    return outs


def _inproj_kernel(x_ref, mixg_ref, win_ref, bg_ref, poolw_ref, pools_ref, qg_ref, kg_ref,
                   cos_ref, sin_ref, gmat_ref, wbp_ref,
                   qa_ref, ka_ref, va_ref, p_ref, g1_ref,
                   ucat_ref, kbt_ref):
    i = pl.program_id(1)
    tm = x_ref.shape[1]
    x = x_ref[0]
    h = _rms(x, mixg_ref[...]).astype(BF16)

    @pl.when(i == 0)
    def _():
        ucat_ref[0:HALO, :] = jnp.zeros((HALO, POOL_WIDTH), F32)
        kbt_ref[...] = jnp.zeros(kbt_ref.shape, F32)

    u = jnp.dot(h, win_ref[:, 0:POOL_WIDTH], preferred_element_type=F32)
    ucat_ref[HALO:HALO + tm, :] = u
    pos = i * tm + lax.broadcasted_iota(jnp.int32, (tm, 1), 0)
    ys = []
    for g, w in enumerate(POOL_WINDOWS):
        lanes = slice(g * POOL_GROUP, (g + 1) * POOL_GROUP)
        wsum = u[:, lanes]
        for j in range(1, w):
            wsum = wsum + ucat_ref[HALO - j:HALO - j + tm, lanes]
        cnt = jnp.minimum(pos + 1, w).astype(F32)
        d = (wsum / cnt - u[:, lanes]).astype(BF16)
        ys.append(jnp.dot(d, poolw_ref[g], preferred_element_type=F32))
    y_pool = (jnp.concatenate(ys, axis=-1) * pools_ref[...]).astype(BF16)
    ucat_ref[0:HALO, :] = u[tm - HALO:, :]

    gl = jnp.dot(h, win_ref[:, POOL_WIDTH + 3 * ATTN_WIDTH:], preferred_element_type=F32)
    gates = jax.nn.sigmoid(gl + bg_ref[...])
    p_ref[0] = gates[:, :D_MODEL] * jnp.dot(y_pool, wbp_ref[...], preferred_element_type=F32)
    g1_ref[0] = gates[:, D_MODEL:]

    o1 = POOL_WIDTH
    q = jnp.dot(h, win_ref[:, o1:o1 + ATTN_WIDTH], preferred_element_type=F32)
    k = jnp.dot(h, win_ref[:, o1 + ATTN_WIDTH:o1 + 2 * ATTN_WIDTH], preferred_element_type=F32)
    v = jnp.dot(h, win_ref[:, o1 + 2 * ATTN_WIDTH:o1 + 3 * ATTN_WIDTH],
                preferred_element_type=F32)
    cos = cos_ref[...]
    sin_signed = sin_ref[...]
    q_tiles = _head_rms_rope(q, gmat_ref[...], qg_ref[...], cos, sin_signed)
    k_tiles = _head_rms_rope(k, gmat_ref[...], kg_ref[...], cos, sin_signed)
    qn = jnp.concatenate(q_tiles, axis=-1)
    kn = jnp.concatenate(k_tiles, axis=-1)

    gate = lax.dot_general(qn, kbt_ref[...], (((1,), (1,)), ((), ())),
                           precision=lax.Precision.HIGHEST, preferred_element_type=F32)
    lane = _lane_iota((tm, LANES))
    slot = lane % N_SLOTS
    gm = jnp.where(slot < i, gate, NEG)
    rank = jnp.zeros((tm, LANES), jnp.int32)
    for r in range(1, N_SLOTS):
        same_vreg_group = slot + r < N_SLOTS
        other = jnp.where(same_vreg_group,
                          pltpu.roll(gm, LANES - r, 1),
                          pltpu.roll(gm, N_SLOTS - r, 1))
        ahead = (other > gm) | ((other == gm) & jnp.logical_not(same_vreg_group))
        rank = rank + jnp.where(ahead, 1, 0)
    chosen = (rank < MOBA_TOP_K) & (slot < i)
    bias = jnp.where(chosen | (slot == i), 0.0, NEG)

    kbar = jnp.sum(kn, axis=0, keepdims=True) * (1.0 / MOBA_BLOCK)
    row = lax.broadcasted_iota(jnp.int32, kbt_ref.shape, 0)
    col = lax.broadcasted_iota(jnp.int32, kbt_ref.shape, 1)
    mine = (row % N_SLOTS == i) & (col // HEAD_DIM == row // N_SLOTS)
    kbt_ref[...] = jnp.where(mine, kbar, kbt_ref[...])

    for hd in range(N_HEADS):
        p, odd = hd // 2, hd % 2
        qp, kp = q_tiles[p], k_tiles[p]
        vp = v[:, p * LANES:(p + 1) * LANES]
        if odd:
            qp = pltpu.roll(qp, HEAD_DIM, 1)
            kp = pltpu.roll(kp, HEAD_DIM, 1)
        bias_h = pltpu.roll(bias, (HEAD_DIM - N_SLOTS * hd) % LANES, 1)
        in_bias = (lane >= HEAD_DIM) & (lane < HEAD_DIM + N_SLOTS)
        qa = jnp.where(lane < HEAD_DIM, qp * (HEAD_DIM ** -0.5), jnp.where(in_bias, bias_h, 0.0))
        ka = jnp.where(lane < HEAD_DIM, kp, jnp.where(lane == HEAD_DIM + i, 1.0, 0.0))
        if odd:
            va = jnp.where(lane >= HEAD_DIM, vp, jnp.where(lane == 0, 1.0, 0.0))
        else:
            va = jnp.where(lane < HEAD_DIM, vp, jnp.where(lane == HEAD_DIM, 1.0, 0.0))
        qa_ref[0, hd] = qa.astype(BF16)
        ka_ref[0, hd] = ka.astype(BF16)
        va_ref[0, hd] = va.astype(BF16)


def _inproj(x, mixg, win, bg, poolw, pools, qg, kg, cos, sin_signed, gmat, wbp):
    b, s, _ = x.shape
    tm = MOBA_BLOCK
    nt = s // tm
    const = lambda *shape: pl.BlockSpec(shape, lambda bi, i: (0,) * len(shape))
    head_spec = pl.BlockSpec((1, N_HEADS, tm, LANES), lambda bi, i: (bi, 0, i, 0))
    row_spec = pl.BlockSpec((1, tm, D_MODEL), lambda bi, i: (bi, i, 0))
    head_shape = jax.ShapeDtypeStruct((b, N_HEADS, s, LANES), BF16)
    row_shape = jax.ShapeDtypeStruct((b, s, D_MODEL), F32)
    return pl.pallas_call(
        _inproj_kernel,
        grid=(b, nt),
        in_specs=[
            row_spec,
            const(1, D_MODEL),
            const(*win.shape),
            const(1, 2 * D_MODEL),
            const(*poolw.shape),
            const(1, POOL_WIDTH),
            const(1, ATTN_WIDTH),
            const(1, ATTN_WIDTH),
            pl.BlockSpec((tm, LANES), lambda bi, i: (i, 0)),
            pl.BlockSpec((tm, LANES), lambda bi, i: (i, 0)),
            const(ATTN_WIDTH, ATTN_WIDTH),
            const(POOL_WIDTH, D_MODEL),
        ],
        out_specs=[head_spec, head_spec, head_spec, row_spec, row_spec],
        out_shape=[head_shape, head_shape, head_shape, row_shape, row_shape],
        scratch_shapes=[
            pltpu.VMEM((HALO + tm, POOL_WIDTH), F32),
            pltpu.VMEM((N_HEADS * N_SLOTS, ATTN_WIDTH), F32),
        ],
        compiler_params=pltpu.CompilerParams(
            dimension_semantics=("arbitrary", "arbitrary"), vmem_limit_bytes=VMEM_LIMIT),
        name="inproj",
    )(x, mixg, win, bg, poolw, pools, qg, kg, cos, sin_signed, gmat, wbp)


def _attn_tile_group(i, nk, q_ref, k_ref, vt_ref, o_ref, s_refs):
    tq = q_ref.shape[2]
    nt_dims = (((1,), (1,)), ((), ()))
    rel = (lax.broadcasted_iota(jnp.int32, (tq, tq), 0)
           - lax.broadcasted_iota(jnp.int32, (tq, tq), 1))

    def scores(hh):
        q = q_ref[0, hh]
        for j in range(nk):
            s = lax.dot_general(k_ref[0, hh, j * tq:(j + 1) * tq, :], q, nt_dims,
                                preferred_element_type=F32)
            if j >= nk - ATTN_GROUP:
                s = jnp.where(rel <= (i - j) * tq, s, NEG)
            s_refs[hh][j] = s

    def softmax_pv(hh):
        s_ref = s_refs[hh]
        mrun = s_ref[0]
        for j in range(1, nk):
            mrun = jnp.maximum(mrun, s_ref[j])
        m = jnp.broadcast_to(jnp.max(mrun, axis=0, keepdims=True), (tq, tq))
        pt = jnp.concatenate([jnp.exp(s_ref[j] - m).astype(BF16) for j in range(nk)], axis=0)
        acc = jnp.dot(vt_ref[0, hh, :, 0:nk * tq], pt, preferred_element_type=F32)
        ones_row = HEAD_DIM * (1 - hh % 2)
        return acc / acc[ones_row:ones_row + 1, :]

    row = lax.broadcasted_iota(jnp.int32, (LANES, tq), 0)
    outs = []
    scores(0)
    for hh in range(ATTN_HEADS):
        if hh + 1 < ATTN_HEADS:
            scores(hh + 1)
        outs.append(softmax_pv(hh))
        if hh % 2:
            pair = jnp.where(row < HEAD_DIM, outs[hh - 1], outs[hh])
            o_ref[0, :, (hh // 2) * LANES:(hh // 2 + 1) * LANES] = pair.T.astype(o_ref.dtype)


def _attn_kernel(q_ref, k_ref, vt_ref, o_ref, *s_refs):
    i = pl.program_id(2)
    for c in range(N_SLOTS // ATTN_GROUP):
        @pl.when(i // ATTN_GROUP == c)
        def _(c=c):
            _attn_tile_group(i, ATTN_GROUP * (c + 1), q_ref, k_ref, vt_ref, o_ref, s_refs)


def _attn(qa, ka, vat):
    b, nh, s, _ = qa.shape
    tq = MOBA_BLOCK
    k_spec = pl.BlockSpec((1, ATTN_HEADS, s, LANES), lambda bi, p, i: (bi, p, 0, 0))
    vt_spec = pl.BlockSpec((1, ATTN_HEADS, LANES, s), lambda bi, p, i: (bi, p, 0, 0))
    return pl.pallas_call(
        _attn_kernel,
        grid=(b, nh // ATTN_HEADS, s // tq),
        in_specs=[pl.BlockSpec((1, ATTN_HEADS, tq, LANES), lambda bi, p, i: (bi, p, i, 0)),
                  k_spec, vt_spec],
        out_specs=pl.BlockSpec((1, tq, HEAD_DIM * ATTN_HEADS), lambda bi, p, i: (bi, i, p)),
        out_shape=jax.ShapeDtypeStruct((b, s, ATTN_WIDTH), BF16),
        scratch_shapes=[pltpu.VMEM((s // tq, tq, tq), F32)] * ATTN_HEADS,
        compiler_params=pltpu.CompilerParams(
            dimension_semantics=("arbitrary", "arbitrary", "arbitrary"),
            vmem_limit_bytes=VMEM_LIMIT),
        name="attn",
    )(qa, ka, vat)


def _merge_kernel(x_ref, p_ref, g1_ref, ya_ref, wba_ref, wo_ref, o_ref):
    merged = p_ref[...] + g1_ref[...] * jnp.dot(ya_ref[...], wba_ref[...],
                                                preferred_element_type=F32)
    o_ref[...] = x_ref[...] + jnp.dot(merged.astype(BF16), wo_ref[...],
                                      preferred_element_type=F32)


def _merge(x2d, p2d, g12d, ya2d, wba, wo):
    t = x2d.shape[0]
    row = pl.BlockSpec((MERGE_TM, D_MODEL), lambda i: (i, 0))
    return pl.pallas_call(
        _merge_kernel,
        grid=(t // MERGE_TM,),
        in_specs=[
            row, row, row,
            pl.BlockSpec((MERGE_TM, ATTN_WIDTH), lambda i: (i, 0)),
            pl.BlockSpec((ATTN_WIDTH, D_MODEL), lambda i: (0, 0)),
            pl.BlockSpec((D_MODEL, D_MODEL), lambda i: (0, 0)),
        ],
        out_specs=row,
        out_shape=jax.ShapeDtypeStruct((t, D_MODEL), F32),
        compiler_params=pltpu.CompilerParams(
            dimension_semantics=("arbitrary",), vmem_limit_bytes=VMEM_LIMIT),
        name="merge",
    )(x2d, p2d, g12d, ya2d, wba, wo)


def _rope_tables(s):
    half = HEAD_DIM // 2
    inv_freq = 1.0 / (ROPE_THETA ** (jnp.arange(half, dtype=F32) * (2.0 / HEAD_DIM)))
    ang = jnp.arange(s, dtype=F32)[:, None] * inv_freq[None, :]
    cos, sin = jnp.cos(ang), jnp.sin(ang)
    cos = jnp.tile(cos, (1, LANES // half))
    sin_signed = jnp.tile(jnp.concatenate([-sin, sin], axis=-1), (1, LANES // HEAD_DIM))
    return cos, sin_signed


def kernel(x, ffn1_norm, ffn1_w_gate_up, ffn1_w_down, mix_norm, w_in, b_gate, pool_w, pool_scale,
           q_norm, k_norm, w_branch_pool, w_branch_attn, w_out, ffn2_norm, ffn2_w_gate_up,
           ffn2_w_down):
    b, s, d = x.shape
    depth = ffn1_norm.shape[0]
    cos, sin_signed = _rope_tables(s)
    head_of = jnp.arange(ATTN_WIDTH) // HEAD_DIM
    gmat = ((head_of[:, None] == head_of[None, :]).astype(F32) / HEAD_DIM).astype(BF16)
    x2d = x.reshape(b * s, d)
    for l in range(depth):
        x2d = _ffn(x2d, ffn1_norm[l][None], ffn1_w_gate_up[l].astype(BF16),
                   ffn1_w_down[l].astype(BF16))
        qa, ka, va, p, g1 = _inproj(
            x2d.reshape(b, s, d), mix_norm[l][None], w_in[l].astype(BF16), b_gate[l][None],
            pool_w[l].astype(BF16), pool_scale[l][None],
            jnp.tile(q_norm[l], N_HEADS)[None], jnp.tile(k_norm[l], N_HEADS)[None],
            cos, sin_signed, gmat, w_branch_pool[l].astype(BF16))
        ya = _attn(qa, ka, jnp.swapaxes(va, 2, 3))
        x2d = _merge(x2d, p.reshape(b * s, d), g1.reshape(b * s, d),
                     ya.reshape(b * s, ATTN_WIDTH), w_branch_attn[l].astype(BF16),
                     w_out[l].astype(BF16))
        x2d = _ffn(x2d, ffn2_norm[l][None], ffn2_w_gate_up[l].astype(BF16),
                   ffn2_w_down[l].astype(BF16))
    return x2d.reshape(b, s, d)
```

```python
import numpy as np

import jax
import jax.numpy as jnp
from jax import lax
from jax.experimental import pallas as pl
from jax.experimental.pallas import tpu as pltpu

F32 = jnp.float32
BF16 = jnp.bfloat16

D_MODEL = 1024
D_FF = 2816
POOL_WINDOWS = (2, 4, 8, 16)
POOL_WIDTH = 512
POOL_GROUP = 128
N_HEADS = 8
HEAD_DIM = 64
ATTN_WIDTH = 512
MOBA_BLOCK = 256
MOBA_TOP_K = 3
ROPE_THETA = 10000.0
EPS = 1e-6
NEG = -1e30

LANES = 128
HALO = 16
N_SLOTS = 16
FFN_TM = 512
FFN_TF = 256
MERGE_TM = 512
INPROJ_TILES = 2
ATTN_GROUP = 2
ATTN_HEADS = 4
VMEM_LIMIT = 56 * 1024 * 1024

_HALF = HEAD_DIM // 2
_GROUP_HEAD = (1, 3, 0, 2, 5, 7, 4, 6)
_HEAD_GROUP = tuple(_GROUP_HEAD.index(h) for h in range(N_HEADS))


def _qk_layout():
    c = np.arange(ATTN_WIDTH)
    lane = c % LANES
    head = 2 * (c // LANES) + (lane // _HALF) % 2
    dim = lane % _HALF + _HALF * (lane // HEAD_DIM)
    return head, dim


def _rms(x, g):
    ms = jnp.mean(x * x, axis=-1, keepdims=True)
    return x * lax.rsqrt(ms + EPS) * g


def _sigmoid(x):
    return 0.5 * jnp.tanh(0.5 * x) + 0.5


def _lane_iota(shape):
    return lax.broadcasted_iota(jnp.int32, shape, len(shape) - 1)


def _split_bf16(x):
    hi = x.astype(BF16)
    return hi, (x - hi.astype(F32)).astype(BF16)


def _ffn_kernel(x_ref, g_ref, wgu_ref, wd_ref, o_ref):
    x = x_ref[...]
    h = _rms(x, g_ref[...]).astype(BF16)
    acc = jnp.zeros(x.shape, F32)
    for c in range(D_FF // FFN_TF):
        a = jnp.dot(h, wgu_ref[:, c * FFN_TF:(c + 1) * FFN_TF], preferred_element_type=F32)
        b = jnp.dot(h, wgu_ref[:, D_FF + c * FFN_TF:D_FF + (c + 1) * FFN_TF],
                    preferred_element_type=F32)
        act = (a * jax.nn.sigmoid(a) * b).astype(BF16)
        acc = acc + jnp.dot(act, wd_ref[c * FFN_TF:(c + 1) * FFN_TF, :],
                            preferred_element_type=F32)
    o_ref[...] = x + 0.5 * acc


def _ffn(x2d, g, wgu, wd):
    t = x2d.shape[0]
    return pl.pallas_call(
        _ffn_kernel,
        grid=(t // FFN_TM,),
        in_specs=[
            pl.BlockSpec((FFN_TM, D_MODEL), lambda i: (i, 0)),
            pl.BlockSpec((1, D_MODEL), lambda i: (0, 0)),
            pl.BlockSpec((D_MODEL, 2 * D_FF), lambda i: (0, 0), pipeline_mode=pl.Buffered(1)),
            pl.BlockSpec((D_FF, D_MODEL), lambda i: (0, 0), pipeline_mode=pl.Buffered(1)),
        ],
        out_specs=pl.BlockSpec((FFN_TM, D_MODEL), lambda i: (i, 0)),
        out_shape=jax.ShapeDtypeStruct((t, D_MODEL), F32),
        compiler_params=pltpu.CompilerParams(
            dimension_semantics=("arbitrary",), vmem_limit_bytes=VMEM_LIMIT),
        name="ffn",
    )(x2d, g, wgu, wd)


def _head_rms_rope(x, gmat, gain, cos, sin_signed):
    hi, lo = _split_bf16(x * x)
    ms = (jnp.dot(hi, gmat, preferred_element_type=F32)
          + jnp.dot(lo, gmat, preferred_element_type=F32))
    y = x * lax.rsqrt(ms + EPS) * gain
    outs = []
    for p in range(ATTN_WIDTH // LANES):
        yp = y[:, p * LANES:(p + 1) * LANES]
        outs.append(yp * cos + pltpu.roll(yp, HEAD_DIM, 1) * sin_signed)
    return outs


def _window_sums(halo, u, steps):
    s = jnp.concatenate([halo, u], axis=0)
    for t in range(steps):
        s = s + pltpu.roll(s, 1 << t, 0)
    return s[HALO:, :]


def _choose_blocks(gate_t, i):
    nq = gate_t.shape[1]
    g3 = gate_t.reshape(N_HEADS, N_SLOTS, nq)
    slot = lax.broadcasted_iota(jnp.int32, g3.shape, 1)
    gm = jnp.where(slot < i, g3, NEG)
    rank = jnp.zeros(g3.shape, jnp.int32)
    for j in range(N_SLOTS):
        other = jnp.broadcast_to(gm[:, j:j + 1, :], g3.shape)
        ahead = (other > gm) | ((other == gm) & (slot > j))
        rank = rank + jnp.where(ahead, 1, 0)
    chosen = (rank < MOBA_TOP_K) & (slot < i)
    bias = jnp.where(chosen | (slot == i), 0.0, NEG)
    return bias.reshape(N_HEADS * N_SLOTS, nq)


def _inproj_tile(i, x, mixg, win_ref, bg, poolw_ref, pools, qg, kg, cos, sin_signed, gmat_ref,
                 wbp_ref, hmask_ref, halo_ref, kbt_ref):
    tm = x.shape[0]
    h = _rms(x, mixg).astype(BF16)

    u = jnp.dot(h, win_ref[:, 0:POOL_WIDTH], preferred_element_type=F32)
    pos = i * tm + lax.broadcasted_iota(jnp.int32, (tm, 1), 0)
    ys = []
    for g, w in enumerate(POOL_WINDOWS):
        lanes = slice(g * POOL_GROUP, (g + 1) * POOL_GROUP)
        wsum = _window_sums(halo_ref[:, lanes], u[:, lanes], g + 1)
        cnt = jnp.minimum(pos + 1, w).astype(F32)
        d = (wsum / cnt - u[:, lanes]).astype(BF16)
        ys.append(jnp.dot(d, poolw_ref[g], preferred_element_type=F32))
    y_pool = (jnp.concatenate(ys, axis=-1) * pools).astype(BF16)
    halo_ref[...] = u[tm - HALO:, :]

    gl = jnp.dot(h, win_ref[:, POOL_WIDTH + 3 * ATTN_WIDTH:], preferred_element_type=F32)
    gates = _sigmoid(gl + bg)
    p_out = gates[:, :D_MODEL] * jnp.dot(y_pool, wbp_ref[...], preferred_element_type=F32)
    g1_out = gates[:, D_MODEL:]

    o1 = POOL_WIDTH
    q = jnp.dot(h, win_ref[:, o1:o1 + ATTN_WIDTH], preferred_element_type=F32)
    k = jnp.dot(h, win_ref[:, o1 + ATTN_WIDTH:o1 + 2 * ATTN_WIDTH], preferred_element_type=F32)
    v = jnp.dot(h, win_ref[:, o1 + 2 * ATTN_WIDTH:o1 + 3 * ATTN_WIDTH],
                preferred_element_type=F32)
    q_tiles = _head_rms_rope(q, gmat_ref[...], qg, cos, sin_signed)
    k_tiles = _head_rms_rope(k, gmat_ref[...], kg, cos, sin_signed)
    qn = jnp.concatenate(q_tiles, axis=-1)
    kn = jnp.concatenate(k_tiles, axis=-1)

    nt_dims = (((1,), (1,)), ((), ()))
    q_hi, q_lo = _split_bf16(qn)
    kb_hi, kb_lo = _split_bf16(kbt_ref[...])
    gate_t = (lax.dot_general(kb_hi, q_hi, nt_dims, preferred_element_type=F32)
              + lax.dot_general(kb_hi, q_lo, nt_dims, preferred_element_type=F32)
              + lax.dot_general(kb_lo, q_hi, nt_dims, preferred_element_type=F32))
    bias = _choose_blocks(gate_t, i).T

    kbar = jnp.sum(kn, axis=0, keepdims=True) * (1.0 / MOBA_BLOCK)
    row = lax.broadcasted_iota(jnp.int32, kbt_ref.shape, 0)
    mine = (row % N_SLOTS == i) & (hmask_ref[...] > 0.0)
    kbt_ref[...] = jnp.where(mine, kbar, kbt_ref[...])

    lane = _lane_iota((tm, LANES))
    qa, ka, va = [], [], []
    for hd in range(N_HEADS):
        p, odd, grp = hd // 2, hd % 2, _HEAD_GROUP[hd]
        own = (lane // _HALF) % 2 == odd
        vp = v[:, p * LANES:(p + 1) * LANES]
        qa.append(jnp.where(own, q_tiles[p] * (HEAD_DIM ** -0.5),
                            jnp.where(lane // N_SLOTS == grp, bias, 0.0)).astype(BF16))
        ka.append(jnp.where(own, k_tiles[p],
                            jnp.where(lane == grp * N_SLOTS + i, 1.0, 0.0)).astype(BF16))
        if odd:
            va.append(jnp.where(lane >= HEAD_DIM, vp, jnp.where(lane == 0, 1.0, 0.0)).astype(BF16))
        else:
            va.append(jnp.where(lane < HEAD_DIM, vp,
                                jnp.where(lane == HEAD_DIM, 1.0, 0.0)).astype(BF16))
    return qa, ka, va, p_out, g1_out


def _inproj_kernel(x_ref, mixg_ref, win_ref, bg_ref, poolw_ref, pools_ref, qg_ref, kg_ref,
                   cos_ref, sin_ref, gmat_ref, wbp_ref, hmask_ref,
                   qa_ref, ka_ref, va_ref, p_ref, g1_ref,
                   halo_ref, kbt_ref):
    step = pl.program_id(1)
    tm = MOBA_BLOCK

    @pl.when(step == 0)
    def _():
        halo_ref[...] = jnp.zeros(halo_ref.shape, F32)
        kbt_ref[...] = jnp.zeros(kbt_ref.shape, F32)

    for t in range(x_ref.shape[1] // tm):
        rows = slice(t * tm, (t + 1) * tm)
        qa, ka, va, p_out, g1_out = _inproj_tile(
            step * (x_ref.shape[1] // tm) + t, x_ref[0, rows, :], mixg_ref[...], win_ref,
            bg_ref[...], poolw_ref, pools_ref[...], qg_ref[...], kg_ref[...],
            cos_ref[rows, :], sin_ref[rows, :], gmat_ref, wbp_ref, hmask_ref, halo_ref, kbt_ref)
        p_ref[0, rows, :] = p_out
        g1_ref[0, rows, :] = g1_out
        for hd in range(N_HEADS):
            qa_ref[0, hd, rows, :] = qa[hd]
            ka_ref[0, hd, rows, :] = ka[hd]
            va_ref[0, hd, rows, :] = va[hd]


def _inproj(x, mixg, win, bg, poolw, pools, qg, kg, cos, sin_signed, gmat, wbp, hmask):
    b, s, _ = x.shape
    tm = INPROJ_TILES * MOBA_BLOCK
    const = lambda *shape: pl.BlockSpec(shape, lambda bi, i: (0,) * len(shape))
    head_spec = pl.BlockSpec((1, N_HEADS, tm, LANES), lambda bi, i: (bi, 0, i, 0))
    row_spec = pl.BlockSpec((1, tm, D_MODEL), lambda bi, i: (bi, i, 0))
    head_shape = jax.ShapeDtypeStruct((b, N_HEADS, s, LANES), BF16)
    row_shape = jax.ShapeDtypeStruct((b, s, D_MODEL), F32)
    return pl.pallas_call(
        _inproj_kernel,
        grid=(b, s // tm),
        in_specs=[
            row_spec,
            const(1, D_MODEL),
            const(*win.shape),
            const(1, 2 * D_MODEL),
            const(*poolw.shape),
            const(1, POOL_WIDTH),
            const(1, ATTN_WIDTH),
            const(1, ATTN_WIDTH),
            pl.BlockSpec((tm, LANES), lambda bi, i: (i, 0)),
            pl.BlockSpec((tm, LANES), lambda bi, i: (i, 0)),
            const(ATTN_WIDTH, ATTN_WIDTH),
            const(POOL_WIDTH, D_MODEL),
            const(N_HEADS * N_SLOTS, ATTN_WIDTH),
        ],
        out_specs=[head_spec, head_spec, head_spec, row_spec, row_spec],
        out_shape=[head_shape, head_shape, head_shape, row_shape, row_shape],
        scratch_shapes=[
            pltpu.VMEM((HALO, POOL_WIDTH), F32),
            pltpu.VMEM((N_HEADS * N_SLOTS, ATTN_WIDTH), F32),
        ],
        compiler_params=pltpu.CompilerParams(
            dimension_semantics=("arbitrary", "arbitrary"), vmem_limit_bytes=VMEM_LIMIT),
        name="inproj",
    )(x, mixg, win, bg, poolw, pools, qg, kg, cos, sin_signed, gmat, wbp, hmask)


def _attn_tile_group(i, nk, q_ref, k_ref, vt_ref, o_ref, s_refs):
    tq = q_ref.shape[2]
    nt_dims = (((1,), (1,)), ((), ()))
    rel = (lax.broadcasted_iota(jnp.int32, (tq, tq), 0)
           - lax.broadcasted_iota(jnp.int32, (tq, tq), 1))

    def scores(hh):
        q = q_ref[0, hh]
        for j in range(nk):
            s = lax.dot_general(k_ref[0, hh, j * tq:(j + 1) * tq, :], q, nt_dims,
                                preferred_element_type=F32)
            if j >= nk - ATTN_GROUP:
                s = jnp.where(rel <= (i - j) * tq, s, NEG)
            s_refs[hh][j] = s

    def softmax_pv(hh):
        s_ref = s_refs[hh]
        mrun = s_ref[0]
        for j in range(1, nk):
            mrun = jnp.maximum(mrun, s_ref[j])
        m = jnp.broadcast_to(jnp.max(mrun, axis=0, keepdims=True), (tq, tq))
        pt = jnp.concatenate([jnp.exp(s_ref[j] - m).astype(BF16) for j in range(nk)], axis=0)
        acc = jnp.dot(vt_ref[0, hh, :, 0:nk * tq], pt, preferred_element_type=F32)
        ones_row = HEAD_DIM * (1 - hh % 2)
        return acc / acc[ones_row:ones_row + 1, :]

    row = lax.broadcasted_iota(jnp.int32, (LANES, tq), 0)
    outs = []
    scores(0)
    for hh in range(ATTN_HEADS):
        if hh + 1 < ATTN_HEADS:
            scores(hh + 1)
        outs.append(softmax_pv(hh))
        if hh % 2:
            pair = jnp.where(row < HEAD_DIM, outs[hh - 1], outs[hh])
            o_ref[0, :, (hh // 2) * LANES:(hh // 2 + 1) * LANES] = pair.T.astype(o_ref.dtype)


def _attn_kernel(q_ref, k_ref, vt_ref, o_ref, *s_refs):
    i = pl.program_id(2)
    for c in range(N_SLOTS // ATTN_GROUP):
        @pl.when(i // ATTN_GROUP == c)
        def _(c=c):
            _attn_tile_group(i, ATTN_GROUP * (c + 1), q_ref, k_ref, vt_ref, o_ref, s_refs)


def _attn(qa, ka, vat):
    b, nh, s, _ = qa.shape
    tq = MOBA_BLOCK
    k_spec = pl.BlockSpec((1, ATTN_HEADS, s, LANES), lambda bi, p, i: (bi, p, 0, 0))
    vt_spec = pl.BlockSpec((1, ATTN_HEADS, LANES, s), lambda bi, p, i: (bi, p, 0, 0))
    return pl.pallas_call(
        _attn_kernel,
        grid=(b, nh // ATTN_HEADS, s // tq),
        in_specs=[pl.BlockSpec((1, ATTN_HEADS, tq, LANES), lambda bi, p, i: (bi, p, i, 0)),
                  k_spec, vt_spec],
        out_specs=pl.BlockSpec((1, tq, HEAD_DIM * ATTN_HEADS), lambda bi, p, i: (bi, i, p)),
        out_shape=jax.ShapeDtypeStruct((b, s, ATTN_WIDTH), BF16),
        scratch_shapes=[pltpu.VMEM((s // tq, tq, tq), F32)] * ATTN_HEADS,
        compiler_params=pltpu.CompilerParams(
            dimension_semantics=("arbitrary", "arbitrary", "arbitrary"),
            vmem_limit_bytes=VMEM_LIMIT),
        name="attn",
    )(qa, ka, vat)


def _merge_kernel(x_ref, p_ref, g1_ref, ya_ref, wba_ref, wo_ref, o_ref):
    merged = p_ref[...] + g1_ref[...] * jnp.dot(ya_ref[...], wba_ref[...],
                                                preferred_element_type=F32)
    o_ref[...] = x_ref[...] + jnp.dot(merged.astype(BF16), wo_ref[...],
                                      preferred_element_type=F32)


def _merge(x2d, p2d, g12d, ya2d, wba, wo):
    t = x2d.shape[0]
    row = pl.BlockSpec((MERGE_TM, D_MODEL), lambda i: (i, 0))
    return pl.pallas_call(
        _merge_kernel,
        grid=(t // MERGE_TM,),
        in_specs=[
            row, row, row,
            pl.BlockSpec((MERGE_TM, ATTN_WIDTH), lambda i: (i, 0)),
            pl.BlockSpec((ATTN_WIDTH, D_MODEL), lambda i: (0, 0)),
            pl.BlockSpec((D_MODEL, D_MODEL), lambda i: (0, 0)),
        ],
        out_specs=row,
        out_shape=jax.ShapeDtypeStruct((t, D_MODEL), F32),
        compiler_params=pltpu.CompilerParams(
            dimension_semantics=("arbitrary",), vmem_limit_bytes=VMEM_LIMIT),
        name="merge",
    )(x2d, p2d, g12d, ya2d, wba, wo)


def _rope_tables(s):
    inv_freq = 1.0 / (ROPE_THETA ** (jnp.arange(_HALF, dtype=F32) * (2.0 / HEAD_DIM)))
    ang = jnp.arange(s, dtype=F32)[:, None] * inv_freq[None, :]
    cos, sin = jnp.cos(ang), jnp.sin(ang)
    cos = jnp.tile(cos, (1, LANES // _HALF))
    sin_signed = jnp.concatenate([-sin, -sin, sin, sin], axis=-1)
    return cos, sin_signed


def kernel(x, ffn1_norm, ffn1_w_gate_up, ffn1_w_down, mix_norm, w_in, b_gate, pool_w, pool_scale,
           q_norm, k_norm, w_branch_pool, w_branch_attn, w_out, ffn2_norm, ffn2_w_gate_up,
           ffn2_w_down):
    b, s, d = x.shape
    assert s == N_SLOTS * MOBA_BLOCK and d == D_MODEL
    depth = ffn1_norm.shape[0]
    cos, sin_signed = _rope_tables(s)
    head, dim = _qk_layout()
    perm = head * HEAD_DIM + dim
    gmat = jnp.asarray((head[:, None] == head[None, :]) / HEAD_DIM, BF16)
    hmask = jnp.asarray(np.repeat(np.asarray(_GROUP_HEAD), N_SLOTS)[:, None] == head[None, :], F32)
    o1, o2, o3 = POOL_WIDTH, POOL_WIDTH + ATTN_WIDTH, POOL_WIDTH + 2 * ATTN_WIDTH
    x2d = x.reshape(b * s, d)
    for l in range(depth):
        x2d = _ffn(x2d, ffn1_norm[l][None], ffn1_w_gate_up[l].astype(BF16),
                   ffn1_w_down[l].astype(BF16))
        win = jnp.concatenate([w_in[l][:, :o1], w_in[l][:, o1:o2][:, perm],
                               w_in[l][:, o2:o3][:, perm], w_in[l][:, o3:]], axis=1).astype(BF16)
        qa, ka, va, p, g1 = _inproj(
            x2d.reshape(b, s, d), mix_norm[l][None], win, b_gate[l][None],
            pool_w[l].astype(BF16), pool_scale[l][None],
            q_norm[l][dim][None], k_norm[l][dim][None],
            cos, sin_signed, gmat, w_branch_pool[l].astype(BF16), hmask)
        ya = _attn(qa, ka, jnp.swapaxes(va, 2, 3))
        x2d = _merge(x2d, p.reshape(b * s, d), g1.reshape(b * s, d),
                     ya.reshape(b * s, ATTN_WIDTH), w_branch_attn[l].astype(BF16),
                     w_out[l].astype(BF16))
        x2d = _ffn(x2d, ffn2_norm[l][None], ffn2_w_gate_up[l].astype(BF16),
                   ffn2_w_down[l].astype(BF16))
    return x2d.reshape(b, s, d)
```

```python
import numpy as np

import jax
import jax.numpy as jnp
from jax import lax
from jax.experimental import pallas as pl
from jax.experimental.pallas import tpu as pltpu

F32 = jnp.float32
BF16 = jnp.bfloat16

D_MODEL = 1024
D_FF = 2816
POOL_WINDOWS = (2, 4, 8, 16)
POOL_WIDTH = 512
POOL_GROUP = 128
N_HEADS = 8
HEAD_DIM = 64
ATTN_WIDTH = 512
MOBA_BLOCK = 256
MOBA_TOP_K = 3
ROPE_THETA = 10000.0
EPS = 1e-6
NEG = -1e30

LANES = 128
HALO = 16
N_SLOTS = 16
FFN_TM = 512
FFN_TF = 256
MERGE_TM = 512
INPROJ_TILES = 2
ATTN_GROUP = 2
ATTN_HEADS = 4
VMEM_LIMIT = 56 * 1024 * 1024

_HALF = HEAD_DIM // 2
_GROUP_HEAD = (1, 3, 0, 2, 5, 7, 4, 6)
_HEAD_GROUP = tuple(_GROUP_HEAD.index(h) for h in range(N_HEADS))


def _qk_layout():
    c = np.arange(ATTN_WIDTH)
    lane = c % LANES
    head = 2 * (c // LANES) + (lane // _HALF) % 2
    dim = lane % _HALF + _HALF * (lane // HEAD_DIM)
    return head, dim


def _rms(x, g):
    ms = jnp.mean(x * x, axis=-1, keepdims=True)
    return x * lax.rsqrt(ms + EPS) * g


def _sigmoid(x):
    return 0.5 * jnp.tanh(0.5 * x) + 0.5


def _lane_iota(shape):
    return lax.broadcasted_iota(jnp.int32, shape, len(shape) - 1)


def _layer_spec(l, shape, **kwargs):
    return pl.BlockSpec((None,) + tuple(shape), lambda *_: (l,) + (0,) * len(shape), **kwargs)


def _split_bf16(x):
    hi = x.astype(BF16)
    return hi, (x - hi.astype(F32)).astype(BF16)


def _ffn_kernel(x_ref, g_ref, wgu_ref, wd_ref, o_ref):
    x = x_ref[...]
    h = _rms(x, g_ref[...]).astype(BF16)
    acc = jnp.zeros(x.shape, F32)
    for c in range(D_FF // FFN_TF):
        a = jnp.dot(h, wgu_ref[:, c * FFN_TF:(c + 1) * FFN_TF], preferred_element_type=F32)
        b = jnp.dot(h, wgu_ref[:, D_FF + c * FFN_TF:D_FF + (c + 1) * FFN_TF],
                    preferred_element_type=F32)
        act = (a * jax.nn.sigmoid(a) * b).astype(BF16)
        acc = acc + jnp.dot(act, wd_ref[c * FFN_TF:(c + 1) * FFN_TF, :],
                            preferred_element_type=F32)
    o_ref[...] = x + 0.5 * acc


def _ffn(l, x2d, g, wgu, wd):
    t = x2d.shape[0]
    return pl.pallas_call(
        _ffn_kernel,
        grid=(t // FFN_TM,),
        in_specs=[
            pl.BlockSpec((FFN_TM, D_MODEL), lambda i: (i, 0)),
            _layer_spec(l, (1, D_MODEL)),
            _layer_spec(l, (D_MODEL, 2 * D_FF), pipeline_mode=pl.Buffered(1)),
            _layer_spec(l, (D_FF, D_MODEL), pipeline_mode=pl.Buffered(1)),
        ],
        out_specs=pl.BlockSpec((FFN_TM, D_MODEL), lambda i: (i, 0)),
        out_shape=jax.ShapeDtypeStruct((t, D_MODEL), F32),
        compiler_params=pltpu.CompilerParams(
            dimension_semantics=("arbitrary",), vmem_limit_bytes=VMEM_LIMIT),
        name="ffn",
    )(x2d, g, wgu, wd)


def _head_rms_rope(x, gmat, gain, cos, sin_signed):
    hi, lo = _split_bf16(x * x)
    ms = (jnp.dot(hi, gmat, preferred_element_type=F32)
          + jnp.dot(lo, gmat, preferred_element_type=F32))
    y = x * lax.rsqrt(ms + EPS) * gain
    outs = []
    for p in range(ATTN_WIDTH // LANES):
        yp = y[:, p * LANES:(p + 1) * LANES]
        outs.append(yp * cos + pltpu.roll(yp, HEAD_DIM, 1) * sin_signed)
    return outs


def _window_sums(halo, u, steps):
    s = jnp.concatenate([halo, u], axis=0)
    for t in range(steps):
        s = s + pltpu.roll(s, 1 << t, 0)
    return s[HALO:, :]


def _choose_blocks(gate_t, i):
    nq = gate_t.shape[1]
    g3 = gate_t.reshape(N_HEADS, N_SLOTS, nq)
    slot = lax.broadcasted_iota(jnp.int32, g3.shape, 1)
    gm = jnp.where(slot < i, g3, NEG)
    rank = jnp.zeros(g3.shape, jnp.int32)
    for j in range(N_SLOTS):
        other = jnp.broadcast_to(gm[:, j:j + 1, :], g3.shape)
        ahead = (other > gm) | ((other == gm) & (slot > j))
        rank = rank + jnp.where(ahead, 1, 0)
    chosen = (rank < MOBA_TOP_K) & (slot < i)
    bias = jnp.where(chosen | (slot == i), 0.0, NEG)
    return bias.reshape(N_HEADS * N_SLOTS, nq)


def _inproj_tile(i, x, mixg, win_ref, bg, poolw_ref, pools, qg, kg, cos, sin_signed, gmat_ref,
                 wbp_ref, hmask_ref, halo_ref, kbt_ref):
    tm = x.shape[0]
    h = _rms(x, mixg).astype(BF16)

    u = jnp.dot(h, win_ref[:, 0:POOL_WIDTH], preferred_element_type=F32)
    pos = i * tm + lax.broadcasted_iota(jnp.int32, (tm, 1), 0)
    ys = []
    for g, w in enumerate(POOL_WINDOWS):
        lanes = slice(g * POOL_GROUP, (g + 1) * POOL_GROUP)
        wsum = _window_sums(halo_ref[:, lanes], u[:, lanes], g + 1)
        cnt = jnp.minimum(pos + 1, w).astype(F32)
        d = (wsum / cnt - u[:, lanes]).astype(BF16)
        ys.append(jnp.dot(d, poolw_ref[g], preferred_element_type=F32))
    y_pool = (jnp.concatenate(ys, axis=-1) * pools).astype(BF16)
    halo_ref[...] = u[tm - HALO:, :]

    gl = jnp.dot(h, win_ref[:, POOL_WIDTH + 3 * ATTN_WIDTH:], preferred_element_type=F32)
    gates = _sigmoid(gl + bg)
    p_out = gates[:, :D_MODEL] * jnp.dot(y_pool, wbp_ref[...], preferred_element_type=F32)
    g1_out = gates[:, D_MODEL:]

    o1 = POOL_WIDTH
    q = jnp.dot(h, win_ref[:, o1:o1 + ATTN_WIDTH], preferred_element_type=F32)
    k = jnp.dot(h, win_ref[:, o1 + ATTN_WIDTH:o1 + 2 * ATTN_WIDTH], preferred_element_type=F32)
    v = jnp.dot(h, win_ref[:, o1 + 2 * ATTN_WIDTH:o1 + 3 * ATTN_WIDTH],
                preferred_element_type=F32)
    q_tiles = _head_rms_rope(q, gmat_ref[...], qg, cos, sin_signed)
    k_tiles = _head_rms_rope(k, gmat_ref[...], kg, cos, sin_signed)
    qn = jnp.concatenate(q_tiles, axis=-1)
    kn = jnp.concatenate(k_tiles, axis=-1)

    nt_dims = (((1,), (1,)), ((), ()))
    q_hi, q_lo = _split_bf16(qn)
    kb_hi, kb_lo = _split_bf16(kbt_ref[...])
    gate_t = (lax.dot_general(kb_hi, q_hi, nt_dims, preferred_element_type=F32)
              + lax.dot_general(kb_hi, q_lo, nt_dims, preferred_element_type=F32)
              + lax.dot_general(kb_lo, q_hi, nt_dims, preferred_element_type=F32))
    bias = _choose_blocks(gate_t, i).T

    kbar = jnp.sum(kn, axis=0, keepdims=True) * (1.0 / MOBA_BLOCK)
    row = lax.broadcasted_iota(jnp.int32, kbt_ref.shape, 0)
    mine = (row % N_SLOTS == i) & (hmask_ref[...] > 0.0)
    kbt_ref[...] = jnp.where(mine, kbar, kbt_ref[...])

    lane = _lane_iota((tm, LANES))
    vrow = lax.broadcasted_iota(jnp.int32, (LANES, tm), 0)
    qa, ka, vat = [], [], []
    for hd in range(N_HEADS):
        p, odd, grp = hd // 2, hd % 2, _HEAD_GROUP[hd]
        own = (lane // _HALF) % 2 == odd
        vp = v[:, p * LANES:(p + 1) * LANES]
        qa.append(jnp.where(own, q_tiles[p] * (HEAD_DIM ** -0.5),
                            jnp.where(lane // N_SLOTS == grp, bias, 0.0)).astype(BF16))
        ka.append(jnp.where(own, k_tiles[p],
                            jnp.where(lane == grp * N_SLOTS + i, 1.0, 0.0)).astype(BF16))
        vpt = vp.T
        if odd:
            vat.append(jnp.where(vrow >= HEAD_DIM, vpt,
                                 jnp.where(vrow == 0, 1.0, 0.0)).astype(BF16))
        else:
            vat.append(jnp.where(vrow < HEAD_DIM, vpt,
                                 jnp.where(vrow == HEAD_DIM, 1.0, 0.0)).astype(BF16))
    return qa, ka, vat, p_out, g1_out


def _inproj_kernel(x_ref, mixg_ref, win_ref, bg_ref, poolw_ref, pools_ref, qg_ref, kg_ref,
                   cos_ref, sin_ref, gmat_ref, wbp_ref, hmask_ref,
                   qa_ref, ka_ref, vat_ref, p_ref, g1_ref,
                   halo_ref, kbt_ref):
    step = pl.program_id(1)
    tm = MOBA_BLOCK

    @pl.when(step == 0)
    def _():
        halo_ref[...] = jnp.zeros(halo_ref.shape, F32)
        kbt_ref[...] = jnp.zeros(kbt_ref.shape, F32)

    for t in range(x_ref.shape[1] // tm):
        rows = slice(t * tm, (t + 1) * tm)
        qa, ka, vat, p_out, g1_out = _inproj_tile(
            step * (x_ref.shape[1] // tm) + t, x_ref[0, rows, :], mixg_ref[...], win_ref,
            bg_ref[...], poolw_ref, pools_ref[...], qg_ref[...], kg_ref[...],
            cos_ref[rows, :], sin_ref[rows, :], gmat_ref, wbp_ref, hmask_ref, halo_ref, kbt_ref)
        p_ref[0, rows, :] = p_out
        g1_ref[0, rows, :] = g1_out
        for hd in range(N_HEADS):
            qa_ref[0, hd, rows, :] = qa[hd]
            ka_ref[0, hd, rows, :] = ka[hd]
            vat_ref[0, hd, :, rows] = vat[hd]


def _inproj(l, x, mixg, win, bg, poolw, pools, qg, kg, cos, sin_signed, gmat, wbp, hmask):
    b, s, _ = x.shape
    tm = INPROJ_TILES * MOBA_BLOCK
    const = lambda *shape: pl.BlockSpec(shape, lambda bi, i: (0,) * len(shape))
    head_spec = pl.BlockSpec((1, N_HEADS, tm, LANES), lambda bi, i: (bi, 0, i, 0))
    headt_spec = pl.BlockSpec((1, N_HEADS, LANES, tm), lambda bi, i: (bi, 0, 0, i))
    headt_shape = jax.ShapeDtypeStruct((b, N_HEADS, LANES, s), BF16)
    row_spec = pl.BlockSpec((1, tm, D_MODEL), lambda bi, i: (bi, i, 0))
    head_shape = jax.ShapeDtypeStruct((b, N_HEADS, s, LANES), BF16)
    row_shape = jax.ShapeDtypeStruct((b, s, D_MODEL), F32)
    return pl.pallas_call(
        _inproj_kernel,
        grid=(b, s // tm),
        in_specs=[
            row_spec,
            _layer_spec(l, (1, D_MODEL)),
            _layer_spec(l, win.shape[1:]),
            _layer_spec(l, (1, 2 * D_MODEL)),
            _layer_spec(l, poolw.shape[1:]),
            _layer_spec(l, (1, POOL_WIDTH)),
            _layer_spec(l, (1, ATTN_WIDTH)),
            _layer_spec(l, (1, ATTN_WIDTH)),
            pl.BlockSpec((tm, LANES), lambda bi, i: (i, 0)),
            pl.BlockSpec((tm, LANES), lambda bi, i: (i, 0)),
            const(ATTN_WIDTH, ATTN_WIDTH),
            _layer_spec(l, (POOL_WIDTH, D_MODEL)),
            const(N_HEADS * N_SLOTS, ATTN_WIDTH),
        ],
        out_specs=[head_spec, head_spec, headt_spec, row_spec, row_spec],
        out_shape=[head_shape, head_shape, headt_shape, row_shape, row_shape],
        scratch_shapes=[
            pltpu.VMEM((HALO, POOL_WIDTH), F32),
            pltpu.VMEM((N_HEADS * N_SLOTS, ATTN_WIDTH), F32),
        ],
        compiler_params=pltpu.CompilerParams(
            dimension_semantics=("arbitrary", "arbitrary"), vmem_limit_bytes=VMEM_LIMIT),
        name="inproj",
    )(x, mixg, win, bg, poolw, pools, qg, kg, cos, sin_signed, gmat, wbp, hmask)


def _attn_tile_group(i, nk, q_ref, k_ref, vt_ref, o_ref, s_refs):
    tq = q_ref.shape[2]
    nt_dims = (((1,), (1,)), ((), ()))
    rel = (lax.broadcasted_iota(jnp.int32, (tq, tq), 0)
           - lax.broadcasted_iota(jnp.int32, (tq, tq), 1))

    def scores(hh):
        q = q_ref[0, hh]
        for j in range(nk):
            s = lax.dot_general(k_ref[0, hh, j * tq:(j + 1) * tq, :], q, nt_dims,
                                preferred_element_type=F32)
            if j >= nk - ATTN_GROUP:
                s = jnp.where(rel <= (i - j) * tq, s, NEG)
            s_refs[hh][j] = s

    def softmax_pv(hh):
        s_ref = s_refs[hh]
        mrun = s_ref[0]
        for j in range(1, nk):
            mrun = jnp.maximum(mrun, s_ref[j])
        m = jnp.broadcast_to(jnp.max(mrun, axis=0, keepdims=True), (tq, tq))
        pt = jnp.concatenate([jnp.exp(s_ref[j] - m).astype(BF16) for j in range(nk)], axis=0)
        acc = jnp.dot(vt_ref[0, hh, :, 0:nk * tq], pt, preferred_element_type=F32)
        ones_row = HEAD_DIM * (1 - hh % 2)
        return acc / acc[ones_row:ones_row + 1, :]

    row = lax.broadcasted_iota(jnp.int32, (LANES, tq), 0)
    outs = []
    scores(0)
    for hh in range(ATTN_HEADS):
        if hh + 1 < ATTN_HEADS:
            scores(hh + 1)
        outs.append(softmax_pv(hh))
        if hh % 2:
            pair = jnp.where(row < HEAD_DIM, outs[hh - 1], outs[hh])
            o_ref[0, :, (hh // 2) * LANES:(hh // 2 + 1) * LANES] = pair.T.astype(o_ref.dtype)


def _attn_kernel(q_ref, k_ref, vt_ref, o_ref, *s_refs):
    i = pl.program_id(2)
    for c in range(N_SLOTS // ATTN_GROUP):
        @pl.when(i // ATTN_GROUP == c)
        def _(c=c):
            _attn_tile_group(i, ATTN_GROUP * (c + 1), q_ref, k_ref, vt_ref, o_ref, s_refs)


def _attn(qa, ka, vat):
    b, nh, s, _ = qa.shape
    tq = MOBA_BLOCK
    k_spec = pl.BlockSpec((1, ATTN_HEADS, s, LANES), lambda bi, p, i: (bi, p, 0, 0))
    vt_spec = pl.BlockSpec((1, ATTN_HEADS, LANES, s), lambda bi, p, i: (bi, p, 0, 0))
    return pl.pallas_call(
        _attn_kernel,
        grid=(b, nh // ATTN_HEADS, s // tq),
        in_specs=[pl.BlockSpec((1, ATTN_HEADS, tq, LANES), lambda bi, p, i: (bi, p, i, 0)),
                  k_spec, vt_spec],
        out_specs=pl.BlockSpec((1, tq, HEAD_DIM * ATTN_HEADS), lambda bi, p, i: (bi, i, p)),
        out_shape=jax.ShapeDtypeStruct((b, s, ATTN_WIDTH), BF16),
        scratch_shapes=[pltpu.VMEM((s // tq, tq, tq), F32)] * ATTN_HEADS,
        compiler_params=pltpu.CompilerParams(
            dimension_semantics=("arbitrary", "arbitrary", "arbitrary"),
            vmem_limit_bytes=VMEM_LIMIT),
        name="attn",
    )(qa, ka, vat)


def _merge_kernel(x_ref, p_ref, g1_ref, ya_ref, wba_ref, wo_ref, o_ref):
    merged = p_ref[...] + g1_ref[...] * jnp.dot(ya_ref[...], wba_ref[...],
                                                preferred_element_type=F32)
    o_ref[...] = x_ref[...] + jnp.dot(merged.astype(BF16), wo_ref[...],
                                      preferred_element_type=F32)


def _merge(l, x2d, p2d, g12d, ya2d, wba, wo):
    t = x2d.shape[0]
    row = pl.BlockSpec((MERGE_TM, D_MODEL), lambda i: (i, 0))
    return pl.pallas_call(
        _merge_kernel,
        grid=(t // MERGE_TM,),
        in_specs=[
            row, row, row,
            pl.BlockSpec((MERGE_TM, ATTN_WIDTH), lambda i: (i, 0)),
            _layer_spec(l, (ATTN_WIDTH, D_MODEL)),
            _layer_spec(l, (D_MODEL, D_MODEL)),
        ],
        out_specs=row,
        out_shape=jax.ShapeDtypeStruct((t, D_MODEL), F32),
        compiler_params=pltpu.CompilerParams(
            dimension_semantics=("arbitrary",), vmem_limit_bytes=VMEM_LIMIT),
        name="merge",
    )(x2d, p2d, g12d, ya2d, wba, wo)


def _rope_tables(s):
    exponent = np.arange(_HALF, dtype=np.float32) * np.float32(2.0 / HEAD_DIM)
    inv_freq = (np.float32(1.0) / np.float32(ROPE_THETA) ** exponent).astype(np.float32)
    ang = np.arange(s, dtype=np.float32)[:, None] * inv_freq[None, :]
    cos, sin = np.cos(ang), np.sin(ang)
    cos = np.tile(cos, (1, LANES // _HALF))
    sin_signed = np.concatenate([-sin, -sin, sin, sin], axis=-1)
    return jnp.asarray(cos, F32), jnp.asarray(sin_signed, F32)


def _permute_heads(w):
    lead = w.shape[:-1]
    w = w.reshape(lead + (N_HEADS // 2, 2, 2, _HALF))
    return jnp.swapaxes(w, -3, -2).reshape(lead + (ATTN_WIDTH,))


def kernel(x, ffn1_norm, ffn1_w_gate_up, ffn1_w_down, mix_norm, w_in, b_gate, pool_w, pool_scale,
           q_norm, k_norm, w_branch_pool, w_branch_attn, w_out, ffn2_norm, ffn2_w_gate_up,
           ffn2_w_down):
    b, s, d = x.shape
    assert s == N_SLOTS * MOBA_BLOCK and d == D_MODEL
    depth = ffn1_norm.shape[0]
    cos, sin_signed = _rope_tables(s)
    head, dim = _qk_layout()
    gmat = jnp.asarray((head[:, None] == head[None, :]) / HEAD_DIM, BF16)
    hmask = jnp.asarray(np.repeat(np.asarray(_GROUP_HEAD), N_SLOTS)[:, None] == head[None, :], F32)
    o1, o2, o3 = POOL_WIDTH, POOL_WIDTH + ATTN_WIDTH, POOL_WIDTH + 2 * ATTN_WIDTH

    row = lambda p: p[:, None, :]
    win = jnp.concatenate([w_in[..., :o1], _permute_heads(w_in[..., o1:o2]),
                           _permute_heads(w_in[..., o2:o3]), w_in[..., o3:]], axis=-1).astype(BF16)
    qg = row(q_norm[:, dim])
    kg = row(k_norm[:, dim])
    wgu1, wd1 = ffn1_w_gate_up.astype(BF16), ffn1_w_down.astype(BF16)
    wgu2, wd2 = ffn2_w_gate_up.astype(BF16), ffn2_w_down.astype(BF16)
    poolw, wbp = pool_w.astype(BF16), w_branch_pool.astype(BF16)
    wba, wo = w_branch_attn.astype(BF16), w_out.astype(BF16)

    x2d = x.reshape(b * s, d)
    for l in range(depth):
        x2d = _ffn(l, x2d, row(ffn1_norm), wgu1, wd1)
        qa, ka, vat, p, g1 = _inproj(
            l, x2d.reshape(b, s, d), row(mix_norm), win, row(b_gate), poolw, row(pool_scale),
            qg, kg, cos, sin_signed, gmat, wbp, hmask)
        ya = _attn(qa, ka, vat)
        x2d = _merge(l, x2d, p.reshape(b * s, d), g1.reshape(b * s, d),
                     ya.reshape(b * s, ATTN_WIDTH), wba, wo)
        x2d = _ffn(l, x2d, row(ffn2_norm), wgu2, wd2)
    return x2d.reshape(b, s, d)
```

```python
import numpy as np

import jax
import jax.numpy as jnp
from jax import lax
from jax.experimental import pallas as pl
from jax.experimental.pallas import tpu as pltpu

F32 = jnp.float32
BF16 = jnp.bfloat16

D_MODEL = 1024
D_FF = 2816
POOL_WINDOWS = (2, 4, 8, 16)
POOL_WIDTH = 512
POOL_GROUP = 128
N_HEADS = 8
HEAD_DIM = 64
ATTN_WIDTH = 512
MOBA_BLOCK = 256
MOBA_TOP_K = 3
ROPE_THETA = 10000.0
EPS = 1e-6
NEG = -1e30

LANES = 128
HALO = 16
N_SLOTS = 16
FFN_TM = 512
FFN_TF = 256
MERGE_TM = 512
INPROJ_TILES = 4
ATTN_GROUP = 2
ATTN_HEADS = 8
ATTN_SCORE_BUFS = 3
VMEM_LIMIT = 56 * 1024 * 1024

_HALF = HEAD_DIM // 2
_GROUP_HEAD = (1, 3, 0, 2, 5, 7, 4, 6)
_HEAD_GROUP = tuple(_GROUP_HEAD.index(h) for h in range(N_HEADS))


def _qk_layout():
    c = np.arange(ATTN_WIDTH)
    lane = c % LANES
    head = 2 * (c // LANES) + (lane // _HALF) % 2
    dim = lane % _HALF + _HALF * (lane // HEAD_DIM)
    return head, dim


def _rms(x, g):
    ms = jnp.mean(x * x, axis=-1, keepdims=True)
    return x * lax.rsqrt(ms + EPS) * g


def _sigmoid(x):
    return 0.5 * jnp.tanh(0.5 * x) + 0.5


def _lane_iota(shape):
    return lax.broadcasted_iota(jnp.int32, shape, len(shape) - 1)


def _layer_spec(l, shape, **kwargs):
    return pl.BlockSpec((None,) + tuple(shape), lambda *_: (l,) + (0,) * len(shape), **kwargs)


def _split_bf16(x):
    hi = x.astype(BF16)
    return hi, (x - hi.astype(F32)).astype(BF16)


def _ffn_kernel(x_ref, g_ref, wgu_ref, wd_ref, o_ref):
    x = x_ref[...]
    h = _rms(x, g_ref[...]).astype(BF16)
    acc = jnp.zeros(x.shape, F32)
    for c in range(D_FF // FFN_TF):
        a = jnp.dot(h, wgu_ref[:, c * FFN_TF:(c + 1) * FFN_TF], preferred_element_type=F32)
        b = jnp.dot(h, wgu_ref[:, D_FF + c * FFN_TF:D_FF + (c + 1) * FFN_TF],
                    preferred_element_type=F32)
        act = (a * jax.nn.sigmoid(a) * b).astype(BF16)
        acc = acc + jnp.dot(act, wd_ref[c * FFN_TF:(c + 1) * FFN_TF, :],
                            preferred_element_type=F32)
    o_ref[...] = x + 0.5 * acc


def _ffn(l, x2d, g, wgu, wd):
    t = x2d.shape[0]
    return pl.pallas_call(
        _ffn_kernel,
        grid=(t // FFN_TM,),
        in_specs=[
            pl.BlockSpec((FFN_TM, D_MODEL), lambda i: (i, 0)),
            _layer_spec(l, (1, D_MODEL)),
            _layer_spec(l, (D_MODEL, 2 * D_FF), pipeline_mode=pl.Buffered(1)),
            _layer_spec(l, (D_FF, D_MODEL), pipeline_mode=pl.Buffered(1)),
        ],
        out_specs=pl.BlockSpec((FFN_TM, D_MODEL), lambda i: (i, 0)),
        out_shape=jax.ShapeDtypeStruct((t, D_MODEL), F32),
        compiler_params=pltpu.CompilerParams(
            dimension_semantics=("arbitrary",), vmem_limit_bytes=VMEM_LIMIT),
        name="ffn",
    )(x2d, g, wgu, wd)


def _head_rms_rope(x, gmat, gain, cos, sin_signed):
    hi, lo = _split_bf16(x * x)
    ms = (jnp.dot(hi, gmat, preferred_element_type=F32)
          + jnp.dot(lo, gmat, preferred_element_type=F32))
    y = x * lax.rsqrt(ms + EPS) * gain
    outs = []
    for p in range(ATTN_WIDTH // LANES):
        yp = y[:, p * LANES:(p + 1) * LANES]
        outs.append(yp * cos + pltpu.roll(yp, HEAD_DIM, 1) * sin_signed)
    return outs


def _window_sums(halo, u, steps):
    s = jnp.concatenate([halo, u], axis=0)
    for t in range(steps):
        s = s + pltpu.roll(s, 1 << t, 0)
    return s[HALO:, :]


def _choose_blocks(gate_t, i):
    nq = gate_t.shape[1]
    g3 = gate_t.reshape(N_HEADS, N_SLOTS, nq)
    slot = lax.broadcasted_iota(jnp.int32, g3.shape, 1)
    gm = jnp.where(slot < i, g3, NEG)
    rank = jnp.zeros(g3.shape, jnp.int32)
    for j in range(N_SLOTS):
        other = jnp.broadcast_to(gm[:, j:j + 1, :], g3.shape)
        ahead = (other > gm) | ((other == gm) & (slot > j))
        rank = rank + jnp.where(ahead, 1, 0)
    chosen = (rank < MOBA_TOP_K) & (slot < i)
    bias = jnp.where(chosen | (slot == i), 0.0, NEG)
    return bias.reshape(N_HEADS * N_SLOTS, nq)


def _inproj_tile(i, x, mixg, win_ref, bg, poolw_ref, pools, qg, kg, cos, sin_signed, gmat_ref,
                 wbp_ref, hmask_ref, halo_ref, kbt_ref):
    tm = x.shape[0]
    h = _rms(x, mixg).astype(BF16)

    u = jnp.dot(h, win_ref[:, 0:POOL_WIDTH], preferred_element_type=F32)
    pos = i * tm + lax.broadcasted_iota(jnp.int32, (tm, 1), 0)
    ys = []
    for g, w in enumerate(POOL_WINDOWS):
        lanes = slice(g * POOL_GROUP, (g + 1) * POOL_GROUP)
        wsum = _window_sums(halo_ref[:, lanes], u[:, lanes], g + 1)
        cnt = jnp.minimum(pos + 1, w).astype(F32)
        d = (wsum / cnt - u[:, lanes]).astype(BF16)
        ys.append(jnp.dot(d, poolw_ref[g], preferred_element_type=F32))
    y_pool = (jnp.concatenate(ys, axis=-1) * pools).astype(BF16)
    halo_ref[...] = u[tm - HALO:, :]

    gl = jnp.dot(h, win_ref[:, POOL_WIDTH + 3 * ATTN_WIDTH:], preferred_element_type=F32)
    gates = _sigmoid(gl + bg)
    p_out = gates[:, :D_MODEL] * jnp.dot(y_pool, wbp_ref[...], preferred_element_type=F32)
    g1_out = gates[:, D_MODEL:]

    o1 = POOL_WIDTH
    q = jnp.dot(h, win_ref[:, o1:o1 + ATTN_WIDTH], preferred_element_type=F32)
    k = jnp.dot(h, win_ref[:, o1 + ATTN_WIDTH:o1 + 2 * ATTN_WIDTH], preferred_element_type=F32)
    v = jnp.dot(h, win_ref[:, o1 + 2 * ATTN_WIDTH:o1 + 3 * ATTN_WIDTH],
                preferred_element_type=F32)
    q_tiles = _head_rms_rope(q, gmat_ref[...], qg, cos, sin_signed)
    k_tiles = _head_rms_rope(k, gmat_ref[...], kg, cos, sin_signed)
    qn = jnp.concatenate(q_tiles, axis=-1)
    kn = jnp.concatenate(k_tiles, axis=-1)

    nt_dims = (((1,), (1,)), ((), ()))
    q_hi, q_lo = _split_bf16(qn)
    kb_hi, kb_lo = _split_bf16(kbt_ref[...])
    gate_t = (lax.dot_general(kb_hi, q_hi, nt_dims, preferred_element_type=F32)
              + lax.dot_general(kb_hi, q_lo, nt_dims, preferred_element_type=F32)
              + lax.dot_general(kb_lo, q_hi, nt_dims, preferred_element_type=F32))
    bias = _choose_blocks(gate_t, i).T

    kbar = jnp.sum(kn, axis=0, keepdims=True) * (1.0 / MOBA_BLOCK)
    row = lax.broadcasted_iota(jnp.int32, kbt_ref.shape, 0)
    mine = (row % N_SLOTS == i) & (hmask_ref[...] > 0.0)
    kbt_ref[...] = jnp.where(mine, kbar, kbt_ref[...])

    lane = _lane_iota((tm, LANES))
    vrow = lax.broadcasted_iota(jnp.int32, (LANES, tm), 0)
    qa, ka, vat = [], [], []
    for hd in range(N_HEADS):
        p, odd, grp = hd // 2, hd % 2, _HEAD_GROUP[hd]
        own = (lane // _HALF) % 2 == odd
        vp = v[:, p * LANES:(p + 1) * LANES]
        qa.append(jnp.where(own, q_tiles[p] * (HEAD_DIM ** -0.5),
                            jnp.where(lane // N_SLOTS == grp, bias, 0.0)).astype(BF16))
        ka.append(jnp.where(own, k_tiles[p],
                            jnp.where(lane == grp * N_SLOTS + i, 1.0, 0.0)).astype(BF16))
        vpt = vp.T
        if odd:
            vat.append(jnp.where(vrow >= HEAD_DIM, vpt,
                                 jnp.where(vrow == 0, 1.0, 0.0)).astype(BF16))
        else:
            vat.append(jnp.where(vrow < HEAD_DIM, vpt,
                                 jnp.where(vrow == HEAD_DIM, 1.0, 0.0)).astype(BF16))
    return qa, ka, vat, p_out, g1_out


def _inproj_kernel(x_ref, mixg_ref, win_ref, bg_ref, poolw_ref, pools_ref, qg_ref, kg_ref,
                   cos_ref, sin_ref, gmat_ref, wbp_ref, hmask_ref,
                   qa_ref, ka_ref, vat_ref, p_ref, g1_ref,
                   halo_ref, kbt_ref):
    step = pl.program_id(1)
    tm = MOBA_BLOCK

    @pl.when(step == 0)
    def _():
        halo_ref[...] = jnp.zeros(halo_ref.shape, F32)
        kbt_ref[...] = jnp.zeros(kbt_ref.shape, F32)

    for t in range(x_ref.shape[1] // tm):
        rows = slice(t * tm, (t + 1) * tm)
        qa, ka, vat, p_out, g1_out = _inproj_tile(
            step * (x_ref.shape[1] // tm) + t, x_ref[0, rows, :], mixg_ref[...], win_ref,
            bg_ref[...], poolw_ref, pools_ref[...], qg_ref[...], kg_ref[...],
            cos_ref[rows, :], sin_ref[rows, :], gmat_ref, wbp_ref, hmask_ref, halo_ref, kbt_ref)
        p_ref[0, rows, :] = p_out
        g1_ref[0, rows, :] = g1_out
        for hd in range(N_HEADS):
            qa_ref[0, hd, rows, :] = qa[hd]
            ka_ref[0, hd, rows, :] = ka[hd]
            vat_ref[0, hd, :, rows] = vat[hd]


def _inproj(l, x, mixg, win, bg, poolw, pools, qg, kg, cos, sin_signed, gmat, wbp, hmask):
    b, s, _ = x.shape
    tm = INPROJ_TILES * MOBA_BLOCK
    const = lambda *shape: pl.BlockSpec(shape, lambda bi, i: (0,) * len(shape))
    head_spec = pl.BlockSpec((1, N_HEADS, tm, LANES), lambda bi, i: (bi, 0, i, 0))
    headt_spec = pl.BlockSpec((1, N_HEADS, LANES, tm), lambda bi, i: (bi, 0, 0, i))
    headt_shape = jax.ShapeDtypeStruct((b, N_HEADS, LANES, s), BF16)
    row_spec = pl.BlockSpec((1, tm, D_MODEL), lambda bi, i: (bi, i, 0))
    head_shape = jax.ShapeDtypeStruct((b, N_HEADS, s, LANES), BF16)
    row_shape = jax.ShapeDtypeStruct((b, s, D_MODEL), F32)
    return pl.pallas_call(
        _inproj_kernel,
        grid=(b, s // tm),
        in_specs=[
            row_spec,
            _layer_spec(l, (1, D_MODEL)),
            _layer_spec(l, win.shape[1:]),
            _layer_spec(l, (1, 2 * D_MODEL)),
            _layer_spec(l, poolw.shape[1:]),
            _layer_spec(l, (1, POOL_WIDTH)),
            _layer_spec(l, (1, ATTN_WIDTH)),
            _layer_spec(l, (1, ATTN_WIDTH)),
            pl.BlockSpec((tm, LANES), lambda bi, i: (i, 0)),
            pl.BlockSpec((tm, LANES), lambda bi, i: (i, 0)),
            const(ATTN_WIDTH, ATTN_WIDTH),
            _layer_spec(l, (POOL_WIDTH, D_MODEL)),
            const(N_HEADS * N_SLOTS, ATTN_WIDTH),
        ],
        out_specs=[head_spec, head_spec, headt_spec, row_spec, row_spec],
        out_shape=[head_shape, head_shape, headt_shape, row_shape, row_shape],
        scratch_shapes=[
            pltpu.VMEM((HALO, POOL_WIDTH), F32),
            pltpu.VMEM((N_HEADS * N_SLOTS, ATTN_WIDTH), F32),
        ],
        compiler_params=pltpu.CompilerParams(
            dimension_semantics=("arbitrary", "arbitrary"), vmem_limit_bytes=VMEM_LIMIT),
        name="inproj",
    )(x, mixg, win, bg, poolw, pools, qg, kg, cos, sin_signed, gmat, wbp, hmask)


def _attn_tile_group(i, nk, q_ref, k_ref, vt_ref, o_ref, s_refs):
    tq = q_ref.shape[2]
    nt_dims = (((1,), (1,)), ((), ()))
    rel = (lax.broadcasted_iota(jnp.int32, (tq, tq), 0)
           - lax.broadcasted_iota(jnp.int32, (tq, tq), 1))

    def scores(hh):
        q = q_ref[0, hh]
        for j in range(nk):
            s = lax.dot_general(k_ref[0, hh, j * tq:(j + 1) * tq, :], q, nt_dims,
                                preferred_element_type=F32)
            if j >= nk - ATTN_GROUP:
                s = jnp.where(rel <= (i - j) * tq, s, NEG)
            s_refs[hh % len(s_refs)][j] = s

    def softmax_pv(hh):
        s_ref = s_refs[hh % len(s_refs)]
        mrun = s_ref[0]
        for j in range(1, nk):
            mrun = jnp.maximum(mrun, s_ref[j])
        m = jnp.broadcast_to(jnp.max(mrun, axis=0, keepdims=True), (tq, tq))
        pt = jnp.concatenate([jnp.exp(s_ref[j] - m).astype(BF16) for j in range(nk)], axis=0)
        acc = jnp.dot(vt_ref[0, hh, :, 0:nk * tq], pt, preferred_element_type=F32)
        ones_row = HEAD_DIM * (1 - hh % 2)
        return acc / acc[ones_row:ones_row + 1, :]

    row = lax.broadcasted_iota(jnp.int32, (LANES, tq), 0)
    outs = []
    scores(0)
    for hh in range(ATTN_HEADS):
        if hh + 1 < ATTN_HEADS:
            scores(hh + 1)
        outs.append(softmax_pv(hh))
        if hh % 2:
            pair = jnp.where(row < HEAD_DIM, outs[hh - 1], outs[hh])
            o_ref[0, :, (hh // 2) * LANES:(hh // 2 + 1) * LANES] = pair.T.astype(o_ref.dtype)


def _attn_kernel(q_ref, k_ref, vt_ref, o_ref, *s_refs):
    i = pl.program_id(2)
    for c in range(N_SLOTS // ATTN_GROUP):
        @pl.when(i // ATTN_GROUP == c)
        def _(c=c):
            _attn_tile_group(i, ATTN_GROUP * (c + 1), q_ref, k_ref, vt_ref, o_ref, s_refs)


def _attn(qa, ka, vat):
    b, nh, s, _ = qa.shape
    tq = MOBA_BLOCK
    k_spec = pl.BlockSpec((1, ATTN_HEADS, s, LANES), lambda bi, p, i: (bi, p, 0, 0),
                          pipeline_mode=pl.Buffered(1))
    vt_spec = pl.BlockSpec((1, ATTN_HEADS, LANES, s), lambda bi, p, i: (bi, p, 0, 0),
                           pipeline_mode=pl.Buffered(1))
    return pl.pallas_call(
        _attn_kernel,
        grid=(b, nh // ATTN_HEADS, s // tq),
        in_specs=[pl.BlockSpec((1, ATTN_HEADS, tq, LANES), lambda bi, p, i: (bi, p, i, 0)),
                  k_spec, vt_spec],
        out_specs=pl.BlockSpec((1, tq, HEAD_DIM * ATTN_HEADS), lambda bi, p, i: (bi, i, p)),
        out_shape=jax.ShapeDtypeStruct((b, s, ATTN_WIDTH), BF16),
        scratch_shapes=[pltpu.VMEM((s // tq, tq, tq), F32)] * ATTN_SCORE_BUFS,
        compiler_params=pltpu.CompilerParams(
            dimension_semantics=("arbitrary", "arbitrary", "arbitrary"),
            vmem_limit_bytes=VMEM_LIMIT),
        name="attn",
    )(qa, ka, vat)


def _merge_kernel(x_ref, p_ref, g1_ref, ya_ref, wba_ref, wo_ref, o_ref):
    merged = p_ref[...] + g1_ref[...] * jnp.dot(ya_ref[...], wba_ref[...],
                                                preferred_element_type=F32)
    o_ref[...] = x_ref[...] + jnp.dot(merged.astype(BF16), wo_ref[...],
                                      preferred_element_type=F32)


def _merge(l, x2d, p2d, g12d, ya2d, wba, wo):
    t = x2d.shape[0]
    row = pl.BlockSpec((MERGE_TM, D_MODEL), lambda i: (i, 0))
    return pl.pallas_call(
        _merge_kernel,
        grid=(t // MERGE_TM,),
        in_specs=[
            row, row, row,
            pl.BlockSpec((MERGE_TM, ATTN_WIDTH), lambda i: (i, 0)),
            _layer_spec(l, (ATTN_WIDTH, D_MODEL)),
            _layer_spec(l, (D_MODEL, D_MODEL)),
        ],
        out_specs=row,
        out_shape=jax.ShapeDtypeStruct((t, D_MODEL), F32),
        compiler_params=pltpu.CompilerParams(
            dimension_semantics=("arbitrary",), vmem_limit_bytes=VMEM_LIMIT),
        name="merge",
    )(x2d, p2d, g12d, ya2d, wba, wo)


def _rope_tables(s):
    inv_freq = 1.0 / ROPE_THETA ** (np.arange(_HALF, dtype=np.float64) * (2.0 / HEAD_DIM))
    ang = np.arange(s, dtype=np.float64)[:, None] * inv_freq[None, :]
    cos, sin = np.cos(ang), np.sin(ang)
    cos = np.tile(cos, (1, LANES // _HALF))
    sin_signed = np.concatenate([-sin, -sin, sin, sin], axis=-1)
    return jnp.asarray(cos, F32), jnp.asarray(sin_signed, F32)


def _permute_heads(w):
    lead = w.shape[:-1]
    w = w.reshape(lead + (N_HEADS // 2, 2, 2, _HALF))
    return jnp.swapaxes(w, -3, -2).reshape(lead + (ATTN_WIDTH,))


def kernel(x, ffn1_norm, ffn1_w_gate_up, ffn1_w_down, mix_norm, w_in, b_gate, pool_w, pool_scale,
           q_norm, k_norm, w_branch_pool, w_branch_attn, w_out, ffn2_norm, ffn2_w_gate_up,
           ffn2_w_down):
    b, s, d = x.shape
    assert s == N_SLOTS * MOBA_BLOCK and d == D_MODEL
    depth = ffn1_norm.shape[0]
    cos, sin_signed = _rope_tables(s)
    head, dim = _qk_layout()
    gmat = jnp.asarray((head[:, None] == head[None, :]) / HEAD_DIM, BF16)
    hmask = jnp.asarray(np.repeat(np.asarray(_GROUP_HEAD), N_SLOTS)[:, None] == head[None, :], F32)
    o1, o2, o3 = POOL_WIDTH, POOL_WIDTH + ATTN_WIDTH, POOL_WIDTH + 2 * ATTN_WIDTH

    row = lambda p: p[:, None, :]
    win = jnp.concatenate([w_in[..., :o1], _permute_heads(w_in[..., o1:o2]),
                           _permute_heads(w_in[..., o2:o3]), w_in[..., o3:]], axis=-1).astype(BF16)
    qg = row(q_norm[:, dim])
    kg = row(k_norm[:, dim])
    wgu1, wd1 = ffn1_w_gate_up.astype(BF16), ffn1_w_down.astype(BF16)
    wgu2, wd2 = ffn2_w_gate_up.astype(BF16), ffn2_w_down.astype(BF16)
    poolw, wbp = pool_w.astype(BF16), w_branch_pool.astype(BF16)
    wba, wo = w_branch_attn.astype(BF16), w_out.astype(BF16)

    x2d = x.reshape(b * s, d)
    for l in range(depth):
        x2d = _ffn(l, x2d, row(ffn1_norm), wgu1, wd1)
        qa, ka, vat, p, g1 = _inproj(
            l, x2d.reshape(b, s, d), row(mix_norm), win, row(b_gate), poolw, row(pool_scale),
            qg, kg, cos, sin_signed, gmat, wbp, hmask)
        ya = _attn(qa, ka, vat)
        x2d = _merge(l, x2d, p.reshape(b * s, d), g1.reshape(b * s, d),
                     ya.reshape(b * s, ATTN_WIDTH), wba, wo)
        x2d = _ffn(l, x2d, row(ffn2_norm), wgu2, wd2)
    return x2d.reshape(b, s, d)
```

```python
import numpy as np

import jax
import jax.numpy as jnp
from jax import lax
from jax.experimental import pallas as pl
from jax.experimental.pallas import tpu as pltpu

F32 = jnp.float32
BF16 = jnp.bfloat16

D_MODEL = 1024
D_FF = 2816
POOL_WINDOWS = (2, 4, 8, 16)
POOL_WIDTH = 512
POOL_GROUP = 128
N_HEADS = 8
HEAD_DIM = 64
ATTN_WIDTH = 512
MOBA_BLOCK = 256
MOBA_TOP_K = 3
ROPE_THETA = 10000.0
EPS = 1e-6
NEG = -1e30

LANES = 128
HALO = 16
N_SLOTS = 16
FFN_TM = 512
FFN_TF = 256
INPROJ_TILES = 4
ATTN_GROUP = 2
ATTN_HEADS = 4
ATTN_SCORE_BUFS = 4
VMEM_LIMIT = 56 * 1024 * 1024

_HALF = HEAD_DIM // 2
_GROUP_HEAD = (1, 3, 0, 2, 5, 7, 4, 6)
_HEAD_GROUP = tuple(_GROUP_HEAD.index(h) for h in range(N_HEADS))


def _qk_layout():
    c = np.arange(ATTN_WIDTH)
    lane = c % LANES
    head = 2 * (c // LANES) + (lane // _HALF) % 2
    dim = lane % _HALF + _HALF * (lane // HEAD_DIM)
    return head, dim


def _rms(x, g):
    ms = jnp.mean(x * x, axis=-1, keepdims=True)
    return x * lax.rsqrt(ms + EPS) * g


def _sigmoid(x):
    return 0.5 * jnp.tanh(0.5 * x) + 0.5


def _lane_iota(shape):
    return lax.broadcasted_iota(jnp.int32, shape, len(shape) - 1)


def _layer_spec(l, shape, **kwargs):
    return pl.BlockSpec((None,) + tuple(shape), lambda *_: (l,) + (0,) * len(shape), **kwargs)


def _split_bf16(x):
    hi = x.astype(BF16)
    return hi, (x - hi.astype(F32)).astype(BF16)


def _swiglu_residual(x, g_ref, wgu_ref, wd_ref, o_ref):
    h = _rms(x, g_ref[...]).astype(BF16)
    acc = jnp.zeros(x.shape, F32)
    for c in range(D_FF // FFN_TF):
        a = jnp.dot(h, wgu_ref[:, c * FFN_TF:(c + 1) * FFN_TF], preferred_element_type=F32)
        b = jnp.dot(h, wgu_ref[:, D_FF + c * FFN_TF:D_FF + (c + 1) * FFN_TF],
                    preferred_element_type=F32)
        act = (a * jax.nn.sigmoid(a) * b).astype(BF16)
        acc = acc + jnp.dot(act, wd_ref[c * FFN_TF:(c + 1) * FFN_TF, :],
                            preferred_element_type=F32)
    o_ref[...] = x + 0.5 * acc


def _ffn_kernel(x_ref, g_ref, wgu_ref, wd_ref, o_ref):
    _swiglu_residual(x_ref[...], g_ref, wgu_ref, wd_ref, o_ref)


def _merge_ffn_kernel(x_ref, p_ref, g1_ref, ya_ref, wba_ref, wo_ref, g_ref, wgu_ref, wd_ref,
                      o_ref):
    merged = p_ref[...] + g1_ref[...] * jnp.dot(ya_ref[...], wba_ref[...],
                                                preferred_element_type=F32)
    x = x_ref[...] + jnp.dot(merged.astype(BF16), wo_ref[...], preferred_element_type=F32)
    _swiglu_residual(x, g_ref, wgu_ref, wd_ref, o_ref)


def _merge_ffn(l, x2d, p2d, g12d, ya2d, wba, wo, g, wgu, wd):
    t = x2d.shape[0]
    row = pl.BlockSpec((FFN_TM, D_MODEL), lambda i: (i, 0))
    return pl.pallas_call(
        _merge_ffn_kernel,
        grid=(t // FFN_TM,),
        in_specs=[
            row, row, row,
            pl.BlockSpec((FFN_TM, ATTN_WIDTH), lambda i: (i, 0)),
            _layer_spec(l, (ATTN_WIDTH, D_MODEL), pipeline_mode=pl.Buffered(1)),
            _layer_spec(l, (D_MODEL, D_MODEL), pipeline_mode=pl.Buffered(1)),
            _layer_spec(l, (1, D_MODEL)),
            _layer_spec(l, (D_MODEL, 2 * D_FF), pipeline_mode=pl.Buffered(1)),
            _layer_spec(l, (D_FF, D_MODEL), pipeline_mode=pl.Buffered(1)),
        ],
        out_specs=row,
        out_shape=jax.ShapeDtypeStruct((t, D_MODEL), F32),
        compiler_params=pltpu.CompilerParams(
            dimension_semantics=("arbitrary",), vmem_limit_bytes=VMEM_LIMIT),
        name="merge_ffn",
    )(x2d, p2d, g12d, ya2d, wba, wo, g, wgu, wd)


def _ffn(l, x2d, g, wgu, wd):
    t = x2d.shape[0]
    return pl.pallas_call(
        _ffn_kernel,
        grid=(t // FFN_TM,),
        in_specs=[
            pl.BlockSpec((FFN_TM, D_MODEL), lambda i: (i, 0)),
            _layer_spec(l, (1, D_MODEL)),
            _layer_spec(l, (D_MODEL, 2 * D_FF), pipeline_mode=pl.Buffered(1)),
            _layer_spec(l, (D_FF, D_MODEL), pipeline_mode=pl.Buffered(1)),
        ],
        out_specs=pl.BlockSpec((FFN_TM, D_MODEL), lambda i: (i, 0)),
        out_shape=jax.ShapeDtypeStruct((t, D_MODEL), F32),
        compiler_params=pltpu.CompilerParams(
            dimension_semantics=("arbitrary",), vmem_limit_bytes=VMEM_LIMIT),
        name="ffn",
    )(x2d, g, wgu, wd)


def _head_rms_rope(x, gmat, gain, cos, sin_signed):
    hi, lo = _split_bf16(x * x)
    ms = (jnp.dot(hi, gmat, preferred_element_type=F32)
          + jnp.dot(lo, gmat, preferred_element_type=F32))
    y = x * lax.rsqrt(ms + EPS) * gain
    outs = []
    for p in range(ATTN_WIDTH // LANES):
        yp = y[:, p * LANES:(p + 1) * LANES]
        outs.append(yp * cos + pltpu.roll(yp, HEAD_DIM, 1) * sin_signed)
    return outs


def _window_sums(halo, u, steps):
    s = jnp.concatenate([halo, u], axis=0)
    for t in range(steps):
        s = s + pltpu.roll(s, 1 << t, 0)
    return s[HALO:, :]


def _choose_blocks(gate_t, i):
    nq = gate_t.shape[1]
    g3 = gate_t.reshape(N_HEADS, N_SLOTS, nq)
    slot = lax.broadcasted_iota(jnp.int32, g3.shape, 1)
    gm = jnp.where(slot < i, g3, NEG)
    rank = jnp.zeros(g3.shape, jnp.int32)
    for j in range(N_SLOTS):
        other = jnp.broadcast_to(gm[:, j:j + 1, :], g3.shape)
        ahead = (other > gm) | ((other == gm) & (slot > j))
        rank = rank + jnp.where(ahead, 1, 0)
    chosen = (rank < MOBA_TOP_K) & (slot < i)
    bias = jnp.where(chosen | (slot == i), 0.0, NEG)
    return bias.reshape(N_HEADS * N_SLOTS, nq)


def _inproj_tile(i, x, mixg, win_ref, bg, poolw_ref, pools, qg, kg, cos, sin_signed, gmat_ref,
                 wbp_ref, hmask_ref, halo_ref, kbt_ref):
    tm = x.shape[0]
    h = _rms(x, mixg).astype(BF16)

    u = jnp.dot(h, win_ref[:, 0:POOL_WIDTH], preferred_element_type=F32)
    pos = i * tm + lax.broadcasted_iota(jnp.int32, (tm, 1), 0)
    ys = []
    for g, w in enumerate(POOL_WINDOWS):
        lanes = slice(g * POOL_GROUP, (g + 1) * POOL_GROUP)
        wsum = _window_sums(halo_ref[:, lanes], u[:, lanes], g + 1)
        cnt = jnp.minimum(pos + 1, w).astype(F32)
        d = (wsum / cnt - u[:, lanes]).astype(BF16)
        ys.append(jnp.dot(d, poolw_ref[g], preferred_element_type=F32))
    y_pool = (jnp.concatenate(ys, axis=-1) * pools).astype(BF16)
    halo_ref[...] = u[tm - HALO:, :]

    gl = jnp.dot(h, win_ref[:, POOL_WIDTH + 3 * ATTN_WIDTH:], preferred_element_type=F32)
    gates = _sigmoid(gl + bg)
    p_out = gates[:, :D_MODEL] * jnp.dot(y_pool, wbp_ref[...], preferred_element_type=F32)
    g1_out = gates[:, D_MODEL:]

    o1 = POOL_WIDTH
    q = jnp.dot(h, win_ref[:, o1:o1 + ATTN_WIDTH], preferred_element_type=F32)
    k = jnp.dot(h, win_ref[:, o1 + ATTN_WIDTH:o1 + 2 * ATTN_WIDTH], preferred_element_type=F32)
    v = jnp.dot(h, win_ref[:, o1 + 2 * ATTN_WIDTH:o1 + 3 * ATTN_WIDTH],
                preferred_element_type=F32)
    q_tiles = _head_rms_rope(q, gmat_ref[...], qg, cos, sin_signed)
    k_tiles = _head_rms_rope(k, gmat_ref[...], kg, cos, sin_signed)
    qn = jnp.concatenate(q_tiles, axis=-1)
    kn = jnp.concatenate(k_tiles, axis=-1)

    nt_dims = (((1,), (1,)), ((), ()))
    q_hi, q_lo = _split_bf16(qn)
    kb_hi, kb_lo = _split_bf16(kbt_ref[...])
    gate_t = (lax.dot_general(kb_hi, q_hi, nt_dims, preferred_element_type=F32)
              + lax.dot_general(kb_hi, q_lo, nt_dims, preferred_element_type=F32)
              + lax.dot_general(kb_lo, q_hi, nt_dims, preferred_element_type=F32))
    bias = _choose_blocks(gate_t, i).T

    kbar = jnp.sum(kn, axis=0, keepdims=True) * (1.0 / MOBA_BLOCK)
    row = lax.broadcasted_iota(jnp.int32, kbt_ref.shape, 0)
    mine = (row % N_SLOTS == i) & (hmask_ref[...] > 0.0)
    kbt_ref[...] = jnp.where(mine, kbar, kbt_ref[...])

    lane = _lane_iota((tm, LANES))
    vrow = lax.broadcasted_iota(jnp.int32, (LANES, tm), 0)
    qa, ka, vat = [], [], []
    for hd in range(N_HEADS):
        p, odd, grp = hd // 2, hd % 2, _HEAD_GROUP[hd]
        own = (lane // _HALF) % 2 == odd
        vp = v[:, p * LANES:(p + 1) * LANES]
        qa.append(jnp.where(own, q_tiles[p] * (HEAD_DIM ** -0.5),
                            jnp.where(lane // N_SLOTS == grp, bias, 0.0)).astype(BF16))
        ka.append(jnp.where(own, k_tiles[p],
                            jnp.where(lane == grp * N_SLOTS + i, 1.0, 0.0)).astype(BF16))
        vpt = vp.T
        if odd:
            vat.append(jnp.where(vrow >= HEAD_DIM, vpt,
                                 jnp.where(vrow == 0, 1.0, 0.0)).astype(BF16))
        else:
            vat.append(jnp.where(vrow < HEAD_DIM, vpt,
                                 jnp.where(vrow == HEAD_DIM, 1.0, 0.0)).astype(BF16))
    return qa, ka, vat, p_out, g1_out


def _inproj_kernel(x_ref, mixg_ref, win_ref, bg_ref, poolw_ref, pools_ref, qg_ref, kg_ref,
                   cos_ref, sin_ref, gmat_ref, wbp_ref, hmask_ref,
                   qa_ref, ka_ref, vat_ref, p_ref, g1_ref,
                   halo_ref, kbt_ref):
    step = pl.program_id(1)
    tm = MOBA_BLOCK

    @pl.when(step == 0)
    def _():
        halo_ref[...] = jnp.zeros(halo_ref.shape, F32)
        kbt_ref[...] = jnp.zeros(kbt_ref.shape, F32)

    for t in range(x_ref.shape[1] // tm):
        rows = slice(t * tm, (t + 1) * tm)
        qa, ka, vat, p_out, g1_out = _inproj_tile(
            step * (x_ref.shape[1] // tm) + t, x_ref[0, rows, :], mixg_ref[...], win_ref,
            bg_ref[...], poolw_ref, pools_ref[...], qg_ref[...], kg_ref[...],
            cos_ref[rows, :], sin_ref[rows, :], gmat_ref, wbp_ref, hmask_ref, halo_ref, kbt_ref)
        p_ref[0, rows, :] = p_out
        g1_ref[0, rows, :] = g1_out
        for hd in range(N_HEADS):
            qa_ref[0, hd, rows, :] = qa[hd]
            ka_ref[0, hd, rows, :] = ka[hd]
            vat_ref[0, hd, :, rows] = vat[hd]


def _inproj(l, x, mixg, win, bg, poolw, pools, qg, kg, cos, sin_signed, gmat, wbp, hmask):
    b, s, _ = x.shape
    tm = INPROJ_TILES * MOBA_BLOCK
    const = lambda *shape: pl.BlockSpec(shape, lambda bi, i: (0,) * len(shape))
    head_spec = pl.BlockSpec((1, N_HEADS, tm, LANES), lambda bi, i: (bi, 0, i, 0))
    headt_spec = pl.BlockSpec((1, N_HEADS, LANES, tm), lambda bi, i: (bi, 0, 0, i))
    headt_shape = jax.ShapeDtypeStruct((b, N_HEADS, LANES, s), BF16)
    row_spec = pl.BlockSpec((1, tm, D_MODEL), lambda bi, i: (bi, i, 0))
    head_shape = jax.ShapeDtypeStruct((b, N_HEADS, s, LANES), BF16)
    row_shape = jax.ShapeDtypeStruct((b, s, D_MODEL), F32)
    return pl.pallas_call(
        _inproj_kernel,
        grid=(b, s // tm),
        in_specs=[
            row_spec,
            _layer_spec(l, (1, D_MODEL)),
            _layer_spec(l, win.shape[1:]),
            _layer_spec(l, (1, 2 * D_MODEL)),
            _layer_spec(l, poolw.shape[1:]),
            _layer_spec(l, (1, POOL_WIDTH)),
            _layer_spec(l, (1, ATTN_WIDTH)),
            _layer_spec(l, (1, ATTN_WIDTH)),
            pl.BlockSpec((tm, LANES), lambda bi, i: (i, 0)),
            pl.BlockSpec((tm, LANES), lambda bi, i: (i, 0)),
            const(ATTN_WIDTH, ATTN_WIDTH),
            _layer_spec(l, (POOL_WIDTH, D_MODEL)),
            const(N_HEADS * N_SLOTS, ATTN_WIDTH),
        ],
        out_specs=[head_spec, head_spec, headt_spec, row_spec, row_spec],
        out_shape=[head_shape, head_shape, headt_shape, row_shape, row_shape],
        scratch_shapes=[
            pltpu.VMEM((HALO, POOL_WIDTH), F32),
            pltpu.VMEM((N_HEADS * N_SLOTS, ATTN_WIDTH), F32),
        ],
        compiler_params=pltpu.CompilerParams(
            dimension_semantics=("arbitrary", "arbitrary"), vmem_limit_bytes=VMEM_LIMIT),
        name="inproj",
    )(x, mixg, win, bg, poolw, pools, qg, kg, cos, sin_signed, gmat, wbp, hmask)


def _attn_tile_group(i, nk, q_ref, k_ref, vt_ref, o_ref, s_refs):
    tq = q_ref.shape[2]
    nt_dims = (((1,), (1,)), ((), ()))
    rel = (lax.broadcasted_iota(jnp.int32, (tq, tq), 0)
           - lax.broadcasted_iota(jnp.int32, (tq, tq), 1))

    def scores(hh):
        q = q_ref[0, hh]
        for j in range(nk):
            s = lax.dot_general(k_ref[0, hh, j * tq:(j + 1) * tq, :], q, nt_dims,
                                preferred_element_type=F32)
            if j >= nk - ATTN_GROUP:
                s = jnp.where(rel <= (i - j) * tq, s, NEG)
            s_refs[hh % len(s_refs)][j] = s

    def softmax_pv(hh):
        s_ref = s_refs[hh % len(s_refs)]
        mrun = s_ref[0]
        for j in range(1, nk):
            mrun = jnp.maximum(mrun, s_ref[j])
        m = jnp.broadcast_to(jnp.max(mrun, axis=0, keepdims=True), (tq, tq))
        pt = jnp.concatenate([jnp.exp(s_ref[j] - m).astype(BF16) for j in range(nk)], axis=0)
        acc = jnp.dot(vt_ref[0, hh, :, 0:nk * tq], pt, preferred_element_type=F32)
        ones_row = HEAD_DIM * (1 - hh % 2)
        return acc / acc[ones_row:ones_row + 1, :]

    row = lax.broadcasted_iota(jnp.int32, (LANES, tq), 0)
    outs = []
    scores(0)
    for hh in range(ATTN_HEADS):
        if hh + 1 < ATTN_HEADS:
            scores(hh + 1)
        outs.append(softmax_pv(hh))
        if hh % 2:
            pair = jnp.where(row < HEAD_DIM, outs[hh - 1], outs[hh])
            o_ref[0, :, (hh // 2) * LANES:(hh // 2 + 1) * LANES] = pair.T.astype(o_ref.dtype)


def _attn_kernel(q_ref, k_ref, vt_ref, o_ref, *s_refs):
    i = pl.program_id(2)
    for c in range(N_SLOTS // ATTN_GROUP):
        @pl.when(i // ATTN_GROUP == c)
        def _(c=c):
            _attn_tile_group(i, ATTN_GROUP * (c + 1), q_ref, k_ref, vt_ref, o_ref, s_refs)


def _attn(qa, ka, vat):
    b, nh, s, _ = qa.shape
    tq = MOBA_BLOCK
    k_spec = pl.BlockSpec((1, ATTN_HEADS, s, LANES), lambda bi, p, i: (bi, p, 0, 0))
    vt_spec = pl.BlockSpec((1, ATTN_HEADS, LANES, s), lambda bi, p, i: (bi, p, 0, 0))
    return pl.pallas_call(
        _attn_kernel,
        grid=(b, nh // ATTN_HEADS, s // tq),
        in_specs=[pl.BlockSpec((1, ATTN_HEADS, tq, LANES), lambda bi, p, i: (bi, p, i, 0)),
                  k_spec, vt_spec],
        out_specs=pl.BlockSpec((1, tq, HEAD_DIM * ATTN_HEADS), lambda bi, p, i: (bi, i, p)),
        out_shape=jax.ShapeDtypeStruct((b, s, ATTN_WIDTH), BF16),
        scratch_shapes=[pltpu.VMEM((s // tq, tq, tq), F32)] * ATTN_SCORE_BUFS,
        compiler_params=pltpu.CompilerParams(
            dimension_semantics=("arbitrary", "arbitrary", "arbitrary"),
            vmem_limit_bytes=VMEM_LIMIT),
        name="attn",
    )(qa, ka, vat)


def _rope_tables(s):
    inv_freq = 1.0 / ROPE_THETA ** (np.arange(_HALF, dtype=np.float64) * (2.0 / HEAD_DIM))
    ang = np.arange(s, dtype=np.float64)[:, None] * inv_freq[None, :]
    cos, sin = np.cos(ang), np.sin(ang)
    cos = np.tile(cos, (1, LANES // _HALF))
    sin_signed = np.concatenate([-sin, -sin, sin, sin], axis=-1)
    return jnp.asarray(cos, F32), jnp.asarray(sin_signed, F32)


def _permute_heads(w):
    lead = w.shape[:-1]
    w = w.reshape(lead + (N_HEADS // 2, 2, 2, _HALF))
    return jnp.swapaxes(w, -3, -2).reshape(lead + (ATTN_WIDTH,))


def kernel(x, ffn1_norm, ffn1_w_gate_up, ffn1_w_down, mix_norm, w_in, b_gate, pool_w, pool_scale,
           q_norm, k_norm, w_branch_pool, w_branch_attn, w_out, ffn2_norm, ffn2_w_gate_up,
           ffn2_w_down):
    b, s, d = x.shape
    assert s == N_SLOTS * MOBA_BLOCK and d == D_MODEL
    depth = ffn1_norm.shape[0]
    cos, sin_signed = _rope_tables(s)
    head, dim = _qk_layout()
    gmat = jnp.asarray((head[:, None] == head[None, :]) / HEAD_DIM, BF16)
    hmask = jnp.asarray(np.repeat(np.asarray(_GROUP_HEAD), N_SLOTS)[:, None] == head[None, :], F32)
    o1, o2, o3 = POOL_WIDTH, POOL_WIDTH + ATTN_WIDTH, POOL_WIDTH + 2 * ATTN_WIDTH

    row = lambda p: p[:, None, :]
    win = jnp.concatenate([w_in[..., :o1], _permute_heads(w_in[..., o1:o2]),
                           _permute_heads(w_in[..., o2:o3]), w_in[..., o3:]], axis=-1).astype(BF16)
    qg = row(q_norm[:, dim])
    kg = row(k_norm[:, dim])
    wgu1, wd1 = ffn1_w_gate_up.astype(BF16), ffn1_w_down.astype(BF16)
    wgu2, wd2 = ffn2_w_gate_up.astype(BF16), ffn2_w_down.astype(BF16)
    poolw, wbp = pool_w.astype(BF16), w_branch_pool.astype(BF16)
    wba, wo = w_branch_attn.astype(BF16), w_out.astype(BF16)

    x2d = x.reshape(b * s, d)
    for l in range(depth):
        x2d = _ffn(l, x2d, row(ffn1_norm), wgu1, wd1)
        qa, ka, vat, p, g1 = _inproj(
            l, x2d.reshape(b, s, d), row(mix_norm), win, row(b_gate), poolw, row(pool_scale),
            qg, kg, cos, sin_signed, gmat, wbp, hmask)
        ya = _attn(qa, ka, vat)
        x2d = _merge_ffn(l, x2d, p.reshape(b * s, d), g1.reshape(b * s, d),
                         ya.reshape(b * s, ATTN_WIDTH), wba, wo, row(ffn2_norm), wgu2, wd2)
    return x2d.reshape(b, s, d)
```

```python
import numpy as np

import jax
import jax.numpy as jnp
from jax import lax
from jax.experimental import pallas as pl
from jax.experimental.pallas import tpu as pltpu

F32 = jnp.float32
BF16 = jnp.bfloat16

D_MODEL = 1024
D_FF = 2816
POOL_WINDOWS = (2, 4, 8, 16)
POOL_WIDTH = 512
POOL_GROUP = 128
N_HEADS = 8
HEAD_DIM = 64
ATTN_WIDTH = 512
MOBA_BLOCK = 256
MOBA_TOP_K = 3
ROPE_THETA = 10000.0
EPS = 1e-6
NEG = -1e30

LANES = 128
MXU_TILE = 256
HALO = 16
N_SLOTS = 16
FFN_TM = 512
FFN_TF = 256
INPROJ_TILES = 4
ATTN_GROUP = 2
ATTN_HEADS = 4
ATTN_SCORE_BUFS = 4
VMEM_LIMIT = 56 * 1024 * 1024

_HALF = HEAD_DIM // 2
_GROUP_HEAD = (1, 3, 0, 2, 5, 7, 4, 6)
_HEAD_GROUP = tuple(_GROUP_HEAD.index(h) for h in range(N_HEADS))


def _qk_layout():
    c = np.arange(ATTN_WIDTH)
    lane = c % LANES
    head = 2 * (c // LANES) + (lane // _HALF) % 2
    dim = lane % _HALF + _HALF * (lane // HEAD_DIM)
    return head, dim


def _rms(x, g):
    ms = jnp.mean(x * x, axis=-1, keepdims=True)
    return x * lax.rsqrt(ms + EPS) * g


def _sigmoid(x):
    return 0.5 * jnp.tanh(0.5 * x) + 0.5


def _lane_iota(shape):
    return lax.broadcasted_iota(jnp.int32, shape, len(shape) - 1)


def _layer_spec(l, shape, **kwargs):
    return pl.BlockSpec((None,) + tuple(shape), lambda *_: (l,) + (0,) * len(shape), **kwargs)


def _split_bf16(x):
    hi = x.astype(BF16)
    return hi, (x - hi.astype(F32)).astype(BF16)


def _swiglu_residual(x, g_ref, wgu_ref, wd_ref, o_ref):
    h = _rms(x, g_ref[...]).astype(BF16)
    acc = jnp.zeros(x.shape, F32)
    for c in range(D_FF // FFN_TF):
        a = jnp.dot(h, wgu_ref[:, c * FFN_TF:(c + 1) * FFN_TF], preferred_element_type=F32)
        b = jnp.dot(h, wgu_ref[:, D_FF + c * FFN_TF:D_FF + (c + 1) * FFN_TF],
                    preferred_element_type=F32)
        act = (a * jax.nn.sigmoid(a) * b).astype(BF16)
        acc = acc + jnp.dot(act, wd_ref[c * FFN_TF:(c + 1) * FFN_TF, :],
                            preferred_element_type=F32)
    o_ref[...] = x + 0.5 * acc


def _ffn_kernel(x_ref, g_ref, wgu_ref, wd_ref, o_ref):
    _swiglu_residual(x_ref[...], g_ref, wgu_ref, wd_ref, o_ref)


def _merge_ffn_kernel(x_ref, p_ref, g1_ref, ya_ref, wba_ref, wo_ref, g_ref, wgu_ref, wd_ref,
                      o_ref):
    merged = p_ref[...] + g1_ref[...] * jnp.dot(ya_ref[...], wba_ref[...],
                                                preferred_element_type=F32)
    x = x_ref[...] + jnp.dot(merged.astype(BF16), wo_ref[...], preferred_element_type=F32)
    _swiglu_residual(x, g_ref, wgu_ref, wd_ref, o_ref)


def _merge_ffn(l, x2d, p2d, g12d, ya2d, wba, wo, g, wgu, wd):
    t = x2d.shape[0]
    row = pl.BlockSpec((FFN_TM, D_MODEL), lambda i: (i, 0))
    return pl.pallas_call(
        _merge_ffn_kernel,
        grid=(t // FFN_TM,),
        in_specs=[
            row, row, row,
            pl.BlockSpec((FFN_TM, ATTN_WIDTH), lambda i: (i, 0)),
            _layer_spec(l, (ATTN_WIDTH, D_MODEL), pipeline_mode=pl.Buffered(1)),
            _layer_spec(l, (D_MODEL, D_MODEL), pipeline_mode=pl.Buffered(1)),
            _layer_spec(l, (1, D_MODEL)),
            _layer_spec(l, (D_MODEL, 2 * D_FF), pipeline_mode=pl.Buffered(1)),
            _layer_spec(l, (D_FF, D_MODEL), pipeline_mode=pl.Buffered(1)),
        ],
        out_specs=row,
        out_shape=jax.ShapeDtypeStruct((t, D_MODEL), F32),
        compiler_params=pltpu.CompilerParams(
            dimension_semantics=("arbitrary",), vmem_limit_bytes=VMEM_LIMIT),
        name="merge_ffn",
    )(x2d, p2d, g12d, ya2d, wba, wo, g, wgu, wd)


def _ffn(l, x2d, g, wgu, wd):
    t = x2d.shape[0]
    return pl.pallas_call(
        _ffn_kernel,
        grid=(t // FFN_TM,),
        in_specs=[
            pl.BlockSpec((FFN_TM, D_MODEL), lambda i: (i, 0)),
            _layer_spec(l, (1, D_MODEL)),
            _layer_spec(l, (D_MODEL, 2 * D_FF), pipeline_mode=pl.Buffered(1)),
            _layer_spec(l, (D_FF, D_MODEL), pipeline_mode=pl.Buffered(1)),
        ],
        out_specs=pl.BlockSpec((FFN_TM, D_MODEL), lambda i: (i, 0)),
        out_shape=jax.ShapeDtypeStruct((t, D_MODEL), F32),
        compiler_params=pltpu.CompilerParams(
            dimension_semantics=("arbitrary",), vmem_limit_bytes=VMEM_LIMIT),
        name="ffn",
    )(x2d, g, wgu, wd)


def _head_rms_rope(x, gmat, gain, cos, sin_signed):
    hi, lo = _split_bf16(x * x)
    ms = jnp.concatenate(
        [jnp.dot(hi[:, c:c + MXU_TILE], gmat, preferred_element_type=F32)
         + jnp.dot(lo[:, c:c + MXU_TILE], gmat, preferred_element_type=F32)
         for c in range(0, ATTN_WIDTH, MXU_TILE)], axis=-1)
    y = x * lax.rsqrt(ms + EPS) * gain
    outs = []
    for p in range(ATTN_WIDTH // LANES):
        yp = y[:, p * LANES:(p + 1) * LANES]
        outs.append(yp * cos + pltpu.roll(yp, HEAD_DIM, 1) * sin_signed)
    return outs


def _window_sums(halo, u, steps):
    s = jnp.concatenate([halo, u], axis=0)
    for t in range(steps):
        s = s + pltpu.roll(s, 1 << t, 0)
    return s[HALO:, :]


def _choose_blocks(gate_t, i):
    nq = gate_t.shape[1]
    g3 = gate_t.reshape(N_HEADS, N_SLOTS, nq)
    slot = lax.broadcasted_iota(jnp.int32, g3.shape, 1)
    gm = jnp.where(slot < i, g3, NEG)
    rank = jnp.zeros(g3.shape, jnp.int32)
    for j in range(N_SLOTS):
        other = jnp.broadcast_to(gm[:, j:j + 1, :], g3.shape)
        ahead = (other > gm) | ((other == gm) & (slot > j))
        rank = rank + jnp.where(ahead, 1, 0)
    chosen = (rank < MOBA_TOP_K) & (slot < i)
    bias = jnp.where(chosen | (slot == i), 0.0, NEG)
    return bias.reshape(N_HEADS * N_SLOTS, nq)


def _inproj_tile(i, x, mixg, win_ref, bg, poolw_ref, pools, qg, kg, cos, sin_signed, gmat_ref,
                 wbp_ref, hmask_ref, halo_ref, kbt_ref):
    tm = x.shape[0]
    h = _rms(x, mixg).astype(BF16)

    u = jnp.dot(h, win_ref[:, 0:POOL_WIDTH], preferred_element_type=F32)
    pos = i * tm + lax.broadcasted_iota(jnp.int32, (tm, 1), 0)
    ds = []
    for g, w in enumerate(POOL_WINDOWS):
        lanes = slice(g * POOL_GROUP, (g + 1) * POOL_GROUP)
        wsum = _window_sums(halo_ref[:, lanes], u[:, lanes], g + 1)
        cnt = jnp.minimum(pos + 1, w).astype(F32)
        ds.append((wsum / cnt - u[:, lanes]).astype(BF16))
    per_tile = MXU_TILE // POOL_GROUP
    ys = [jnp.dot(jnp.concatenate(ds[t * per_tile:(t + 1) * per_tile], axis=-1), poolw_ref[t],
                  preferred_element_type=F32) for t in range(len(ds) // per_tile)]
    y_pool = (jnp.concatenate(ys, axis=-1) * pools).astype(BF16)
    halo_ref[...] = u[tm - HALO:, :]

    gl = jnp.dot(h, win_ref[:, POOL_WIDTH + 3 * ATTN_WIDTH:], preferred_element_type=F32)
    gates = _sigmoid(gl + bg)
    p_out = gates[:, :D_MODEL] * jnp.dot(y_pool, wbp_ref[...], preferred_element_type=F32)
    g1_out = gates[:, D_MODEL:]

    o1 = POOL_WIDTH
    q = jnp.dot(h, win_ref[:, o1:o1 + ATTN_WIDTH], preferred_element_type=F32)
    k = jnp.dot(h, win_ref[:, o1 + ATTN_WIDTH:o1 + 2 * ATTN_WIDTH], preferred_element_type=F32)
    v = jnp.dot(h, win_ref[:, o1 + 2 * ATTN_WIDTH:o1 + 3 * ATTN_WIDTH],
                preferred_element_type=F32)
    q_tiles = _head_rms_rope(q, gmat_ref[...], qg, cos, sin_signed)
    k_tiles = _head_rms_rope(k, gmat_ref[...], kg, cos, sin_signed)
    qn = jnp.concatenate(q_tiles, axis=-1)
    kn = jnp.concatenate(k_tiles, axis=-1)

    nt_dims = (((1,), (1,)), ((), ()))
    q_hi, q_lo = _split_bf16(qn)
    kb_hi, kb_lo = _split_bf16(kbt_ref[...])
    gate_t = (lax.dot_general(kb_hi, q_hi, nt_dims, preferred_element_type=F32)
              + lax.dot_general(kb_hi, q_lo, nt_dims, preferred_element_type=F32)
              + lax.dot_general(kb_lo, q_hi, nt_dims, preferred_element_type=F32))
    bias = _choose_blocks(gate_t, i).T

    kbar = jnp.sum(kn, axis=0, keepdims=True) * (1.0 / MOBA_BLOCK)
    row = lax.broadcasted_iota(jnp.int32, kbt_ref.shape, 0)
    mine = (row % N_SLOTS == i) & (hmask_ref[...] > 0.0)
    kbt_ref[...] = jnp.where(mine, kbar, kbt_ref[...])

    lane = _lane_iota((tm, LANES))
    vrow = lax.broadcasted_iota(jnp.int32, (LANES, tm), 0)
    qa, ka, vat = [], [], []
    for hd in range(N_HEADS):
        p, odd, grp = hd // 2, hd % 2, _HEAD_GROUP[hd]
        own = (lane // _HALF) % 2 == odd
        vp = v[:, p * LANES:(p + 1) * LANES]
        qa.append(jnp.where(own, q_tiles[p] * (HEAD_DIM ** -0.5),
                            jnp.where(lane // N_SLOTS == grp, bias, 0.0)).astype(BF16))
        ka.append(jnp.where(own, k_tiles[p],
                            jnp.where(lane == grp * N_SLOTS + i, 1.0, 0.0)).astype(BF16))
        vpt = vp.T
        if odd:
            vat.append(jnp.where(vrow >= HEAD_DIM, vpt,
                                 jnp.where(vrow == 0, 1.0, 0.0)).astype(BF16))
        else:
            vat.append(jnp.where(vrow < HEAD_DIM, vpt,
                                 jnp.where(vrow == HEAD_DIM, 1.0, 0.0)).astype(BF16))
    return qa, ka, vat, p_out, g1_out


def _inproj_kernel(x_ref, mixg_ref, win_ref, bg_ref, poolw_ref, pools_ref, qg_ref, kg_ref,
                   cos_ref, sin_ref, gmat_ref, wbp_ref, hmask_ref,
                   qa_ref, ka_ref, vat_ref, p_ref, g1_ref,
                   halo_ref, kbt_ref):
    step = pl.program_id(1)
    tm = MOBA_BLOCK

    @pl.when(step == 0)
    def _():
        halo_ref[...] = jnp.zeros(halo_ref.shape, F32)
        kbt_ref[...] = jnp.zeros(kbt_ref.shape, F32)

    for t in range(x_ref.shape[1] // tm):
        rows = slice(t * tm, (t + 1) * tm)
        qa, ka, vat, p_out, g1_out = _inproj_tile(
            step * (x_ref.shape[1] // tm) + t, x_ref[0, rows, :], mixg_ref[...], win_ref,
            bg_ref[...], poolw_ref, pools_ref[...], qg_ref[...], kg_ref[...],
            cos_ref[rows, :], sin_ref[rows, :], gmat_ref, wbp_ref, hmask_ref, halo_ref, kbt_ref)
        p_ref[0, rows, :] = p_out
        g1_ref[0, rows, :] = g1_out
        for hd in range(N_HEADS):
            qa_ref[0, hd, rows, :] = qa[hd]
            ka_ref[0, hd, rows, :] = ka[hd]
            vat_ref[0, hd, :, rows] = vat[hd]


def _inproj(l, x, mixg, win, bg, poolw, pools, qg, kg, cos, sin_signed, gmat, wbp, hmask):
    b, s, _ = x.shape
    tm = INPROJ_TILES * MOBA_BLOCK
    const = lambda *shape: pl.BlockSpec(shape, lambda bi, i: (0,) * len(shape))
    head_spec = pl.BlockSpec((1, N_HEADS, tm, LANES), lambda bi, i: (bi, 0, i, 0))
    headt_spec = pl.BlockSpec((1, N_HEADS, LANES, tm), lambda bi, i: (bi, 0, 0, i))
    headt_shape = jax.ShapeDtypeStruct((b, N_HEADS, LANES, s), BF16)
    row_spec = pl.BlockSpec((1, tm, D_MODEL), lambda bi, i: (bi, i, 0))
    head_shape = jax.ShapeDtypeStruct((b, N_HEADS, s, LANES), BF16)
    row_shape = jax.ShapeDtypeStruct((b, s, D_MODEL), F32)
    return pl.pallas_call(
        _inproj_kernel,
        grid=(b, s // tm),
        in_specs=[
            row_spec,
            _layer_spec(l, (1, D_MODEL)),
            _layer_spec(l, win.shape[1:]),
            _layer_spec(l, (1, 2 * D_MODEL)),
            _layer_spec(l, poolw.shape[1:]),
            _layer_spec(l, (1, POOL_WIDTH)),
            _layer_spec(l, (1, ATTN_WIDTH)),
            _layer_spec(l, (1, ATTN_WIDTH)),
            pl.BlockSpec((tm, LANES), lambda bi, i: (i, 0)),
            pl.BlockSpec((tm, LANES), lambda bi, i: (i, 0)),
            const(MXU_TILE, MXU_TILE),
            _layer_spec(l, (POOL_WIDTH, D_MODEL)),
            const(N_HEADS * N_SLOTS, ATTN_WIDTH),
        ],
        out_specs=[head_spec, head_spec, headt_spec, row_spec, row_spec],
        out_shape=[head_shape, head_shape, headt_shape, row_shape, row_shape],
        scratch_shapes=[
            pltpu.VMEM((HALO, POOL_WIDTH), F32),
            pltpu.VMEM((N_HEADS * N_SLOTS, ATTN_WIDTH), F32),
        ],
        compiler_params=pltpu.CompilerParams(
            dimension_semantics=("arbitrary", "arbitrary"), vmem_limit_bytes=VMEM_LIMIT),
        name="inproj",
    )(x, mixg, win, bg, poolw, pools, qg, kg, cos, sin_signed, gmat, wbp, hmask)


def _attn_tile_group(i, nk, q_ref, k_ref, vt_ref, o_ref, s_refs):
    tq = q_ref.shape[2]
    nt_dims = (((1,), (1,)), ((), ()))
    rel = (lax.broadcasted_iota(jnp.int32, (tq, tq), 0)
           - lax.broadcasted_iota(jnp.int32, (tq, tq), 1))

    def scores(hh):
        q = q_ref[0, hh]
        for j in range(nk):
            s = lax.dot_general(k_ref[0, hh, j * tq:(j + 1) * tq, :], q, nt_dims,
                                preferred_element_type=F32)
            if j >= nk - ATTN_GROUP:
                s = jnp.where(rel <= (i - j) * tq, s, NEG)
            s_refs[hh % len(s_refs)][j] = s

    def softmax_pv(hh):
        s_ref = s_refs[hh % len(s_refs)]
        mrun = s_ref[0]
        for j in range(1, nk):
            mrun = jnp.maximum(mrun, s_ref[j])
        m = jnp.broadcast_to(jnp.max(mrun, axis=0, keepdims=True), (tq, tq))
        pt = jnp.concatenate([jnp.exp(s_ref[j] - m).astype(BF16) for j in range(nk)], axis=0)
        acc = jnp.dot(vt_ref[0, hh, :, 0:nk * tq], pt, preferred_element_type=F32)
        ones_row = HEAD_DIM * (1 - hh % 2)
        return acc / acc[ones_row:ones_row + 1, :]

    row = lax.broadcasted_iota(jnp.int32, (LANES, tq), 0)
    outs = []
    scores(0)
    for hh in range(ATTN_HEADS):
        if hh + 1 < ATTN_HEADS:
            scores(hh + 1)
        outs.append(softmax_pv(hh))
        if hh % 2:
            pair = jnp.where(row < HEAD_DIM, outs[hh - 1], outs[hh])
            o_ref[0, :, (hh // 2) * LANES:(hh // 2 + 1) * LANES] = pair.T.astype(o_ref.dtype)


def _attn_kernel(q_ref, k_ref, vt_ref, o_ref, *s_refs):
    i = pl.program_id(2)
    for c in range(N_SLOTS // ATTN_GROUP):
        @pl.when(i // ATTN_GROUP == c)
        def _(c=c):
            _attn_tile_group(i, ATTN_GROUP * (c + 1), q_ref, k_ref, vt_ref, o_ref, s_refs)


def _attn(qa, ka, vat):
    b, nh, s, _ = qa.shape
    tq = MOBA_BLOCK
    k_spec = pl.BlockSpec((1, ATTN_HEADS, s, LANES), lambda bi, p, i: (bi, p, 0, 0))
    vt_spec = pl.BlockSpec((1, ATTN_HEADS, LANES, s), lambda bi, p, i: (bi, p, 0, 0))
    return pl.pallas_call(
        _attn_kernel,
        grid=(b, nh // ATTN_HEADS, s // tq),
        in_specs=[pl.BlockSpec((1, ATTN_HEADS, tq, LANES), lambda bi, p, i: (bi, p, i, 0)),
                  k_spec, vt_spec],
        out_specs=pl.BlockSpec((1, tq, HEAD_DIM * ATTN_HEADS), lambda bi, p, i: (bi, i, p)),
        out_shape=jax.ShapeDtypeStruct((b, s, ATTN_WIDTH), BF16),
        scratch_shapes=[pltpu.VMEM((s // tq, tq, tq), F32)] * ATTN_SCORE_BUFS,
        compiler_params=pltpu.CompilerParams(
            dimension_semantics=("arbitrary", "arbitrary", "arbitrary"),
            vmem_limit_bytes=VMEM_LIMIT),
        name="attn",
    )(qa, ka, vat)


def _rope_tables(s):
    inv_freq = 1.0 / ROPE_THETA ** (np.arange(_HALF, dtype=np.float64) * (2.0 / HEAD_DIM))
    ang = np.arange(s, dtype=np.float64)[:, None] * inv_freq[None, :]
    cos, sin = np.cos(ang), np.sin(ang)
    cos = np.tile(cos, (1, LANES // _HALF))
    sin_signed = np.concatenate([-sin, -sin, sin, sin], axis=-1)
    return jnp.asarray(cos, F32), jnp.asarray(sin_signed, F32)


def _permute_heads(w):
    lead = w.shape[:-1]
    w = w.reshape(lead + (N_HEADS // 2, 2, 2, _HALF))
    return jnp.swapaxes(w, -3, -2).reshape(lead + (ATTN_WIDTH,))


def kernel(x, ffn1_norm, ffn1_w_gate_up, ffn1_w_down, mix_norm, w_in, b_gate, pool_w, pool_scale,
           q_norm, k_norm, w_branch_pool, w_branch_attn, w_out, ffn2_norm, ffn2_w_gate_up,
           ffn2_w_down):
    b, s, d = x.shape
    assert s == N_SLOTS * MOBA_BLOCK and d == D_MODEL
    depth = ffn1_norm.shape[0]
    cos, sin_signed = _rope_tables(s)
    head, dim = _qk_layout()
    gmat = jnp.asarray((head[:MXU_TILE, None] == head[None, :MXU_TILE]) / HEAD_DIM, BF16)
    hmask = jnp.asarray(np.repeat(np.asarray(_GROUP_HEAD), N_SLOTS)[:, None] == head[None, :], F32)
    o1, o2, o3 = POOL_WIDTH, POOL_WIDTH + ATTN_WIDTH, POOL_WIDTH + 2 * ATTN_WIDTH

    row = lambda p: p[:, None, :]
    win = jnp.concatenate([w_in[..., :o1], _permute_heads(w_in[..., o1:o2]),
                           _permute_heads(w_in[..., o2:o3]), w_in[..., o3:]], axis=-1).astype(BF16)
    qg = row(q_norm[:, dim])
    kg = row(k_norm[:, dim])
    wgu1, wd1 = ffn1_w_gate_up.astype(BF16), ffn1_w_down.astype(BF16)
    wgu2, wd2 = ffn2_w_gate_up.astype(BF16), ffn2_w_down.astype(BF16)
    per_tile = MXU_TILE // POOL_GROUP
    n_tiles = len(POOL_WINDOWS) // per_tile
    poolw = jnp.einsum('ltaij,ab->ltaibj',
                       pool_w.reshape(depth, n_tiles, per_tile, POOL_GROUP, POOL_GROUP),
                       jnp.eye(per_tile, dtype=F32)
                       ).reshape(depth, n_tiles, MXU_TILE, MXU_TILE).astype(BF16)
    wbp = w_branch_pool.astype(BF16)
    wba, wo = w_branch_attn.astype(BF16), w_out.astype(BF16)

    x2d = x.reshape(b * s, d)
    for l in range(depth):
        x2d = _ffn(l, x2d, row(ffn1_norm), wgu1, wd1)
        qa, ka, vat, p, g1 = _inproj(
            l, x2d.reshape(b, s, d), row(mix_norm), win, row(b_gate), poolw, row(pool_scale),
            qg, kg, cos, sin_signed, gmat, wbp, hmask)
        ya = _attn(qa, ka, vat)
        x2d = _merge_ffn(l, x2d, p.reshape(b * s, d), g1.reshape(b * s, d),
                         ya.reshape(b * s, ATTN_WIDTH), wba, wo, row(ffn2_norm), wgu2, wd2)
    return x2d.reshape(b, s, d)
```

```python
import numpy as np

import jax
import jax.numpy as jnp
from jax import lax
from jax.experimental import pallas as pl
from jax.experimental.pallas import tpu as pltpu

F32 = jnp.float32
BF16 = jnp.bfloat16

D_MODEL = 1024
D_FF = 2816
POOL_WINDOWS = (2, 4, 8, 16)
POOL_WIDTH = 512
POOL_GROUP = 128
N_HEADS = 8
HEAD_DIM = 64
ATTN_WIDTH = 512
MOBA_BLOCK = 256
MOBA_TOP_K = 3
ROPE_THETA = 10000.0
EPS = 1e-6
NEG = -1e30

LANES = 128
MXU_TILE = 256
BF16_TILE_ROWS = 16
VT_ROWS = 80
HALO = 16
N_SLOTS = 16
FFN_TM = 512
FFN_TF = 256
INPROJ_TILES = 4
INPROJ_MM_ROWS = 256
ATTN_GROUP = 2
ATTN_HEADS = 4
ATTN_SCORE_BUFS = 4
VMEM_LIMIT = 60 * 1024 * 1024

_HALF = HEAD_DIM // 2
_GROUP_HEAD = (1, 3, 0, 2, 5, 7, 4, 6)
_HEAD_GROUP = tuple(_GROUP_HEAD.index(h) for h in range(N_HEADS))


def _qk_layout():
    c = np.arange(ATTN_WIDTH)
    lane = c % LANES
    head = 2 * (c // LANES) + (lane // _HALF) % 2
    dim = lane % _HALF + _HALF * (lane // HEAD_DIM)
    return head, dim


def _rms(x, g):
    ms = jnp.mean(x * x, axis=-1, keepdims=True)
    return x * lax.rsqrt(ms + EPS) * g


def _sigmoid(x):
    return 0.5 * jnp.tanh(0.5 * x) + 0.5


def _lane_iota(shape):
    return lax.broadcasted_iota(jnp.int32, shape, len(shape) - 1)


def _layer_spec(l, shape, **kwargs):
    return pl.BlockSpec((None,) + tuple(shape), lambda *_: (l,) + (0,) * len(shape), **kwargs)


def _dot_cols(lhs, w_ref, start, stop):
    return jnp.concatenate(
        [jnp.dot(lhs, w_ref[:, c:c + MXU_TILE], preferred_element_type=F32)
         for c in range(start, stop, MXU_TILE)], axis=-1)


def _split_bf16(x):
    hi = x.astype(BF16)
    return hi, (x - hi.astype(F32)).astype(BF16)


def _swiglu_residual(x, g_ref, wgu_ref, wd_ref, o_ref):
    h = _rms(x, g_ref[...]).astype(BF16)
    acc = jnp.zeros(x.shape, F32)
    for c in range(D_FF // FFN_TF):
        a = jnp.dot(h, wgu_ref[:, c * FFN_TF:(c + 1) * FFN_TF], preferred_element_type=F32)
        b = jnp.dot(h, wgu_ref[:, D_FF + c * FFN_TF:D_FF + (c + 1) * FFN_TF],
                    preferred_element_type=F32)
        act = (a * jax.nn.sigmoid(a) * b).astype(BF16)
        acc = acc + jnp.dot(act, wd_ref[c * FFN_TF:(c + 1) * FFN_TF, :],
                            preferred_element_type=F32)
    o_ref[...] = x + 0.5 * acc


def _run_cast_jobs(src_refs, dst_refs):
    for src, dst in zip(src_refs, dst_refs):
        dst[...] = src[...].astype(dst.dtype)


def _ffn_kernel(x_ref, g_ref, wgu_ref, wd_ref, *rest):
    n_jobs = (len(rest) - 1) // 2
    o_ref = rest[n_jobs]
    _swiglu_residual(x_ref[...], g_ref, wgu_ref, wd_ref, o_ref)
    _run_cast_jobs(rest[:n_jobs], rest[n_jobs + 1:])


def _merge_ffn_kernel(x_ref, p_ref, g1_ref, ya_ref, wba_ref, wo_ref, g_ref, wgu_ref, wd_ref,
                      *rest):
    n_jobs = (len(rest) - 1) // 2
    o_ref = rest[n_jobs]
    merged = p_ref[...] + g1_ref[...] * jnp.dot(ya_ref[...], wba_ref[...],
                                                preferred_element_type=F32)
    x = x_ref[...] + jnp.dot(merged.astype(BF16), wo_ref[...], preferred_element_type=F32)
    _swiglu_residual(x, g_ref, wgu_ref, wd_ref, o_ref)
    _run_cast_jobs(rest[:n_jobs], rest[n_jobs + 1:])


def _cast_job_specs(jobs, n_steps):
    in_specs, out_specs, out_shapes = [], [], []
    for w, l in jobs:
        _, r, c = w.shape
        n_blocks = max(nb for nb in range(1, n_steps + 1)
                       if n_steps % nb == 0 and r % (nb * BF16_TILE_ROWS) == 0)
        rows = r // n_blocks
        in_specs.append(pl.BlockSpec(
            (None, rows, c), lambda i, l=l, n_blocks=n_blocks: (l, (i * n_blocks) // n_steps, 0)))
        out_specs.append(pl.BlockSpec(
            (rows, c), lambda i, n_blocks=n_blocks: ((i * n_blocks) // n_steps, 0)))
        out_shapes.append(jax.ShapeDtypeStruct((r, c), BF16))
    return in_specs, out_specs, out_shapes


def _resident(w):
    return pl.BlockSpec(w.shape, lambda *_: (0, 0), pipeline_mode=pl.Buffered(1))


def _merge_ffn(l, x2d, p2d, g12d, ya2d, wba, wo, g, wgu, wd, cast_jobs=()):
    t = x2d.shape[0]
    n_steps = t // FFN_TM
    row = pl.BlockSpec((FFN_TM, D_MODEL), lambda i: (i, 0))
    job_in, job_out, job_shapes = _cast_job_specs(cast_jobs, n_steps)
    outs = pl.pallas_call(
        _merge_ffn_kernel,
        grid=(n_steps,),
        in_specs=[
            row, row, row,
            pl.BlockSpec((FFN_TM, ATTN_WIDTH), lambda i: (i, 0)),
            _resident(wba), _resident(wo),
            _layer_spec(l, (1, D_MODEL)),
            _resident(wgu), _resident(wd),
        ] + job_in,
        out_specs=[row] + job_out,
        out_shape=[jax.ShapeDtypeStruct((t, D_MODEL), F32)] + job_shapes,
        compiler_params=pltpu.CompilerParams(
            dimension_semantics=("arbitrary",), vmem_limit_bytes=VMEM_LIMIT),
        name="merge_ffn",
    )(x2d, p2d, g12d, ya2d, wba, wo, g, wgu, wd, *[w for w, _ in cast_jobs])
    return outs[0], outs[1:]


def _ffn(l, x2d, g, wgu, wd, cast_jobs=()):
    t = x2d.shape[0]
    n_steps = t // FFN_TM
    row = pl.BlockSpec((FFN_TM, D_MODEL), lambda i: (i, 0))
    job_in, job_out, job_shapes = _cast_job_specs(cast_jobs, n_steps)
    outs = pl.pallas_call(
        _ffn_kernel,
        grid=(n_steps,),
        in_specs=[row, _layer_spec(l, (1, D_MODEL)), _resident(wgu), _resident(wd)] + job_in,
        out_specs=[row] + job_out,
        out_shape=[jax.ShapeDtypeStruct((t, D_MODEL), F32)] + job_shapes,
        compiler_params=pltpu.CompilerParams(
            dimension_semantics=("arbitrary",), vmem_limit_bytes=VMEM_LIMIT),
        name="ffn",
    )(x2d, g, wgu, wd, *[w for w, _ in cast_jobs])
    return outs[0], outs[1:]


def _head_rms_rope(x, gmat, gain, cos, sin_signed):
    hi, lo = _split_bf16(x * x)
    ms = jnp.concatenate(
        [jnp.dot(hi[:, c:c + MXU_TILE], gmat, preferred_element_type=F32)
         + jnp.dot(lo[:, c:c + MXU_TILE], gmat, preferred_element_type=F32)
         for c in range(0, ATTN_WIDTH, MXU_TILE)], axis=-1)
    y = x * lax.rsqrt(ms + EPS) * gain
    outs = []
    for p in range(ATTN_WIDTH // LANES):
        yp = y[:, p * LANES:(p + 1) * LANES]
        outs.append(yp * cos + pltpu.roll(yp, HEAD_DIM, 1) * sin_signed)
    return outs


def _window_sums(halo, u, steps):
    s = jnp.concatenate([halo, u], axis=0)
    for t in range(steps):
        s = s + pltpu.roll(s, 1 << t, 0)
    return s[HALO:, :]


def _choose_blocks(gate_t, i):
    nq = gate_t.shape[1]
    g3 = gate_t.reshape(N_HEADS, N_SLOTS, nq)
    slot = lax.broadcasted_iota(jnp.int32, g3.shape, 1)
    gm = jnp.where(slot < i, g3, NEG)
    rank = jnp.zeros(g3.shape, jnp.int32)
    for j in range(N_SLOTS):
        other = jnp.broadcast_to(gm[:, j:j + 1, :], g3.shape)
        ahead = (other > gm) | ((other == gm) & (slot > j))
        rank = rank + jnp.where(ahead, 1, 0)
    chosen = (rank < MOBA_TOP_K) & (slot < i)
    bias = jnp.where(chosen | (slot == i), 0.0, NEG)
    return bias.reshape(N_HEADS * N_SLOTS, nq)


def _inproj_rows(row0, x, mixg, win_ref, bg, poolw_ref, pools, qg, kg, cos, sin_signed, gmat_ref,
                 wbp_ref, halo_ref):
    tm = x.shape[0]
    h = _rms(x, mixg).astype(BF16)

    u = _dot_cols(h, win_ref, 0, POOL_WIDTH)
    pos = row0 + lax.broadcasted_iota(jnp.int32, (tm, 1), 0)
    ds = []
    for g, w in enumerate(POOL_WINDOWS):
        lanes = slice(g * POOL_GROUP, (g + 1) * POOL_GROUP)
        wsum = _window_sums(halo_ref[:, lanes], u[:, lanes], g + 1)
        cnt = jnp.minimum(pos + 1, w).astype(F32)
        ds.append((wsum / cnt - u[:, lanes]).astype(BF16))
    per_tile = MXU_TILE // POOL_GROUP
    ys = [jnp.dot(jnp.concatenate(ds[t * per_tile:(t + 1) * per_tile], axis=-1), poolw_ref[t],
                  preferred_element_type=F32) for t in range(len(ds) // per_tile)]
    y_pool = (jnp.concatenate(ys, axis=-1) * pools).astype(BF16)
    halo_ref[...] = u[tm - HALO:, :]

    gl = _dot_cols(h, win_ref, POOL_WIDTH + 3 * ATTN_WIDTH, win_ref.shape[1])
    gates = _sigmoid(gl + bg)
    p_out = gates[:, :D_MODEL] * _dot_cols(y_pool, wbp_ref, 0, D_MODEL)
    g1_out = gates[:, D_MODEL:]

    o1 = POOL_WIDTH
    q = _dot_cols(h, win_ref, o1, o1 + ATTN_WIDTH)
    k = _dot_cols(h, win_ref, o1 + ATTN_WIDTH, o1 + 2 * ATTN_WIDTH)
    v = _dot_cols(h, win_ref, o1 + 2 * ATTN_WIDTH, o1 + 3 * ATTN_WIDTH)
    q_tiles = _head_rms_rope(q, gmat_ref[...], qg, cos, sin_signed)
    k_tiles = _head_rms_rope(k, gmat_ref[...], kg, cos, sin_signed)
    return p_out, g1_out, q_tiles, k_tiles, v


def _inproj_block(i, q_tiles, k_tiles, v, hmask_ref, kbt_ref):
    tm = v.shape[0]
    qn = jnp.concatenate(q_tiles, axis=-1)
    kn = jnp.concatenate(k_tiles, axis=-1)

    nt_dims = (((1,), (1,)), ((), ()))
    q_hi, q_lo = _split_bf16(qn)
    kb_hi, kb_lo = _split_bf16(kbt_ref[...])
    gate_t = (lax.dot_general(kb_hi, q_hi, nt_dims, preferred_element_type=F32)
              + lax.dot_general(kb_hi, q_lo, nt_dims, preferred_element_type=F32)
              + lax.dot_general(kb_lo, q_hi, nt_dims, preferred_element_type=F32))
    bias = _choose_blocks(gate_t, i).T

    kbar = jnp.sum(kn, axis=0, keepdims=True) * (1.0 / MOBA_BLOCK)
    row = lax.broadcasted_iota(jnp.int32, kbt_ref.shape, 0)
    mine = (row % N_SLOTS == i) & (hmask_ref[...] > 0.0)
    kbt_ref[...] = jnp.where(mine, kbar, kbt_ref[...])

    lane = _lane_iota((tm, LANES))
    ones_rows = jnp.where(lax.broadcasted_iota(jnp.int32, (VT_ROWS - HEAD_DIM, tm), 0) == 0,
                          1.0, 0.0)
    qa, ka, vat = [], [], []
    for hd in range(N_HEADS):
        p, odd, grp = hd // 2, hd % 2, _HEAD_GROUP[hd]
        own = (lane // _HALF) % 2 == odd
        vp = v[:, p * LANES:(p + 1) * LANES]
        qa.append(jnp.where(own, q_tiles[p] * (HEAD_DIM ** -0.5),
                            jnp.where(lane // N_SLOTS == grp, bias, 0.0)).astype(BF16))
        ka.append(jnp.where(own, k_tiles[p],
                            jnp.where(lane == grp * N_SLOTS + i, 1.0, 0.0)).astype(BF16))
        vt = vp.T[odd * HEAD_DIM:(odd + 1) * HEAD_DIM, :]
        vat.append(jnp.concatenate([vt, ones_rows], axis=0).astype(BF16))
    return qa, ka, vat


def _inproj_kernel(x_ref, mixg_ref, win_ref, bg_ref, poolw_ref, pools_ref, qg_ref, kg_ref,
                   cos_ref, sin_ref, gmat_ref, wbp_ref, hmask_ref,
                   qa_ref, ka_ref, vat_ref, p_ref, g1_ref,
                   halo_ref, kbt_ref):
    step = pl.program_id(1)
    step_rows = x_ref.shape[1]

    @pl.when(step == 0)
    def _():
        halo_ref[...] = jnp.zeros(halo_ref.shape, F32)
        kbt_ref[...] = jnp.zeros(kbt_ref.shape, F32)

    for r0 in range(0, step_rows, INPROJ_MM_ROWS):
        rows = slice(r0, r0 + INPROJ_MM_ROWS)
        p_out, g1_out, q_tiles, k_tiles, v = _inproj_rows(
            step * step_rows + r0, x_ref[0, rows, :], mixg_ref[...], win_ref, bg_ref[...],
            poolw_ref, pools_ref[...], qg_ref[...], kg_ref[...], cos_ref[rows, :],
            sin_ref[rows, :], gmat_ref, wbp_ref, halo_ref)
        p_ref[0, rows, :] = p_out
        g1_ref[0, rows, :] = g1_out
        for t0 in range(0, INPROJ_MM_ROWS, MOBA_BLOCK):
            sub = slice(t0, t0 + MOBA_BLOCK)
            blk = slice(r0 + t0, r0 + t0 + MOBA_BLOCK)
            qa, ka, vat = _inproj_block(
                (step * step_rows + r0 + t0) // MOBA_BLOCK, [q[sub] for q in q_tiles],
                [k[sub] for k in k_tiles], v[sub], hmask_ref, kbt_ref)
            for hd in range(N_HEADS):
                qa_ref[0, hd, blk, :] = qa[hd]
                ka_ref[0, hd, blk, :] = ka[hd]
                vat_ref[0, hd, :, blk] = vat[hd]


def _inproj(l, x, mixg, win, bg, poolw, pools, qg, kg, cos, sin_signed, gmat, wbp, hmask):
    b, s, _ = x.shape
    tm = INPROJ_TILES * MOBA_BLOCK
    const = lambda *shape: pl.BlockSpec(shape, lambda bi, i: (0,) * len(shape))
    head_spec = pl.BlockSpec((1, N_HEADS, tm, LANES), lambda bi, i: (bi, 0, i, 0))
    headt_spec = pl.BlockSpec((1, N_HEADS, VT_ROWS, tm), lambda bi, i: (bi, 0, 0, i))
    headt_shape = jax.ShapeDtypeStruct((b, N_HEADS, VT_ROWS, s), BF16)
    row_spec = pl.BlockSpec((1, tm, D_MODEL), lambda bi, i: (bi, i, 0))
    head_shape = jax.ShapeDtypeStruct((b, N_HEADS, s, LANES), BF16)
    row_shape = jax.ShapeDtypeStruct((b, s, D_MODEL), F32)
    return pl.pallas_call(
        _inproj_kernel,
        grid=(b, s // tm),
        in_specs=[
            row_spec,
            _layer_spec(l, (1, D_MODEL)),
            _layer_spec(l, win.shape[1:], pipeline_mode=pl.Buffered(1)),
            _layer_spec(l, (1, 2 * D_MODEL)),
            _layer_spec(l, poolw.shape[1:]),
            _layer_spec(l, (1, POOL_WIDTH)),
            _layer_spec(l, (1, ATTN_WIDTH)),
            _layer_spec(l, (1, ATTN_WIDTH)),
            pl.BlockSpec((tm, LANES), lambda bi, i: (i, 0)),
            pl.BlockSpec((tm, LANES), lambda bi, i: (i, 0)),
            const(MXU_TILE, MXU_TILE),
            _resident(wbp),
            const(N_HEADS * N_SLOTS, ATTN_WIDTH),
        ],
        out_specs=[head_spec, head_spec, headt_spec, row_spec, row_spec],
        out_shape=[head_shape, head_shape, headt_shape, row_shape, row_shape],
        scratch_shapes=[
            pltpu.VMEM((HALO, POOL_WIDTH), F32),
            pltpu.VMEM((N_HEADS * N_SLOTS, ATTN_WIDTH), F32),
        ],
        compiler_params=pltpu.CompilerParams(
            dimension_semantics=("arbitrary", "arbitrary"), vmem_limit_bytes=VMEM_LIMIT),
        name="inproj",
    )(x, mixg, win, bg, poolw, pools, qg, kg, cos, sin_signed, gmat, wbp, hmask)


def _attn_tile_group(i, nk, q_ref, k_ref, vt_ref, o_ref, s_refs):
    tq = q_ref.shape[2]
    nt_dims = (((1,), (1,)), ((), ()))
    rel = (lax.broadcasted_iota(jnp.int32, (tq, tq), 0)
           - lax.broadcasted_iota(jnp.int32, (tq, tq), 1))

    def scores(hh):
        q = q_ref[0, hh]
        for j in range(nk):
            s = lax.dot_general(k_ref[0, hh, j * tq:(j + 1) * tq, :], q, nt_dims,
                                preferred_element_type=F32)
            if j >= nk - ATTN_GROUP:
                s = jnp.where(rel <= (i - j) * tq, s, NEG)
            s_refs[hh % len(s_refs)][j] = s

    def softmax_pv(hh):
        s_ref = s_refs[hh % len(s_refs)]
        mrun = s_ref[0]
        for j in range(1, nk):
            mrun = jnp.maximum(mrun, s_ref[j])
        m = jnp.broadcast_to(jnp.max(mrun, axis=0, keepdims=True), (tq, tq))
        pt = jnp.concatenate([jnp.exp(s_ref[j] - m).astype(BF16) for j in range(nk)], axis=0)
        acc = jnp.dot(vt_ref[0, hh, :, 0:nk * tq], pt, preferred_element_type=F32)
        return acc[:HEAD_DIM, :] / acc[HEAD_DIM:HEAD_DIM + 1, :]

    outs = []
    scores(0)
    for hh in range(ATTN_HEADS):
        if hh + 1 < ATTN_HEADS:
            scores(hh + 1)
        outs.append(softmax_pv(hh))
        if hh % 2:
            pair = jnp.concatenate([outs[hh - 1], outs[hh]], axis=0)
            o_ref[0, :, (hh // 2) * LANES:(hh // 2 + 1) * LANES] = pair.T.astype(o_ref.dtype)


def _attn_kernel(q_ref, k_ref, vt_ref, o_ref, *s_refs):
    i = pl.program_id(2)
    for c in range(N_SLOTS // ATTN_GROUP):
        @pl.when(i // ATTN_GROUP == c)
        def _(c=c):
            _attn_tile_group(i, ATTN_GROUP * (c + 1), q_ref, k_ref, vt_ref, o_ref, s_refs)


def _attn(qa, ka, vat):
    b, nh, s, _ = qa.shape
    tq = MOBA_BLOCK
    k_spec = pl.BlockSpec((1, ATTN_HEADS, s, LANES), lambda bi, p, i: (bi, p, 0, 0))
    vt_spec = pl.BlockSpec((1, ATTN_HEADS, VT_ROWS, s), lambda bi, p, i: (bi, p, 0, 0))
    return pl.pallas_call(
        _attn_kernel,
        grid=(b, nh // ATTN_HEADS, s // tq),
        in_specs=[pl.BlockSpec((1, ATTN_HEADS, tq, LANES), lambda bi, p, i: (bi, p, i, 0)),
                  k_spec, vt_spec],
        out_specs=pl.BlockSpec((1, tq, HEAD_DIM * ATTN_HEADS), lambda bi, p, i: (bi, i, p)),
        out_shape=jax.ShapeDtypeStruct((b, s, ATTN_WIDTH), BF16),
        scratch_shapes=[pltpu.VMEM((s // tq, tq, tq), F32)] * ATTN_SCORE_BUFS,
        compiler_params=pltpu.CompilerParams(
            dimension_semantics=("arbitrary", "arbitrary", "arbitrary"),
            vmem_limit_bytes=VMEM_LIMIT),
        name="attn",
    )(qa, ka, vat)


def _rope_tables(s):
    inv_freq = 1.0 / ROPE_THETA ** (np.arange(_HALF, dtype=np.float64) * (2.0 / HEAD_DIM))
    ang = np.arange(s, dtype=np.float64)[:, None] * inv_freq[None, :]
    cos, sin = np.cos(ang), np.sin(ang)
    cos = np.tile(cos, (1, LANES // _HALF))
    sin_signed = np.concatenate([-sin, -sin, sin, sin], axis=-1)
    return jnp.asarray(cos, F32), jnp.asarray(sin_signed, F32)


def _permute_heads(w):
    lead = w.shape[:-1]
    w = w.reshape(lead + (N_HEADS // 2, 2, 2, _HALF))
    return jnp.swapaxes(w, -3, -2).reshape(lead + (ATTN_WIDTH,))


def kernel(x, ffn1_norm, ffn1_w_gate_up, ffn1_w_down, mix_norm, w_in, b_gate, pool_w, pool_scale,
           q_norm, k_norm, w_branch_pool, w_branch_attn, w_out, ffn2_norm, ffn2_w_gate_up,
           ffn2_w_down):
    b, s, d = x.shape
    assert s == N_SLOTS * MOBA_BLOCK and d == D_MODEL
    depth = ffn1_norm.shape[0]
    cos, sin_signed = _rope_tables(s)
    head, dim = _qk_layout()
    gmat = jnp.asarray((head[:MXU_TILE, None] == head[None, :MXU_TILE]) / HEAD_DIM, BF16)
    hmask = jnp.asarray(np.repeat(np.asarray(_GROUP_HEAD), N_SLOTS)[:, None] == head[None, :], F32)
    o1, o2, o3 = POOL_WIDTH, POOL_WIDTH + ATTN_WIDTH, POOL_WIDTH + 2 * ATTN_WIDTH

    row = lambda p: p[:, None, :]
    win = jnp.concatenate([w_in[..., :o1], _permute_heads(w_in[..., o1:o2]),
                           _permute_heads(w_in[..., o2:o3]), w_in[..., o3:]], axis=-1).astype(BF16)
    qg = row(q_norm[:, dim])
    kg = row(k_norm[:, dim])
    per_tile = MXU_TILE // POOL_GROUP
    n_tiles = len(POOL_WINDOWS) // per_tile
    poolw = jnp.einsum('ltaij,ab->ltaibj',
                       pool_w.reshape(depth, n_tiles, per_tile, POOL_GROUP, POOL_GROUP),
                       jnp.eye(per_tile, dtype=F32)
                       ).reshape(depth, n_tiles, MXU_TILE, MXU_TILE).astype(BF16)

    wgu, wd = ffn1_w_gate_up[0].astype(BF16), ffn1_w_down[0].astype(BF16)
    x2d = x.reshape(b * s, d)
    for l in range(depth):
        mixer_jobs = [(ffn2_w_gate_up, l), (ffn2_w_down, l), (w_branch_pool, l),
                      (w_branch_attn, l), (w_out, l)]
        x2d, (wgu2, wd2, wbp, wba, wo) = _ffn(l, x2d, row(ffn1_norm), wgu, wd, mixer_jobs)
        qa, ka, vat, p, g1 = _inproj(
            l, x2d.reshape(b, s, d), row(mix_norm), win, row(b_gate), poolw, row(pool_scale),
            qg, kg, cos, sin_signed, gmat, wbp, hmask)
        ya = _attn(qa, ka, vat)
        next_jobs = [(ffn1_w_gate_up, l + 1), (ffn1_w_down, l + 1)] if l + 1 < depth else []
        x2d, nxt = _merge_ffn(l, x2d, p.reshape(b * s, d), g1.reshape(b * s, d),
                              ya.reshape(b * s, ATTN_WIDTH), wba, wo, row(ffn2_norm), wgu2, wd2,
                              next_jobs)
        if nxt:
            wgu, wd = nxt
    return x2d.reshape(b, s, d)
```

```python
import numpy as np

import jax
import jax.numpy as jnp
from jax import lax
from jax.experimental import pallas as pl
from jax.experimental.pallas import tpu as pltpu

F32 = jnp.float32
BF16 = jnp.bfloat16

D_MODEL = 1024
D_FF = 2816
POOL_WINDOWS = (2, 4, 8, 16)
POOL_WIDTH = 512
POOL_GROUP = 128
N_HEADS = 8
HEAD_DIM = 64
ATTN_WIDTH = 512
MOBA_BLOCK = 256
MOBA_TOP_K = 3
ROPE_THETA = 10000.0
EPS = 1e-6
NEG = -1e30

LANES = 128
MXU_TILE = 256
BF16_TILE_ROWS = 16
VT_ROWS = 80
HALO = 16
N_SLOTS = 16
FFN_TM = 512
FFN_TF = 256
INPROJ_TILES = 4
INPROJ_MM_ROWS = 256
ATTN_GROUP = 4
ATTN_HEADS = 4
ATTN_SCORE_BUFS = 4
VMEM_LIMIT = 60 * 1024 * 1024

_HALF = HEAD_DIM // 2
_GROUP_HEAD = (1, 3, 0, 2, 5, 7, 4, 6)
_HEAD_GROUP = tuple(_GROUP_HEAD.index(h) for h in range(N_HEADS))


def _qk_layout():
    c = np.arange(ATTN_WIDTH)
    lane = c % LANES
    head = 2 * (c // LANES) + (lane // _HALF) % 2
    dim = lane % _HALF + _HALF * (lane // HEAD_DIM)
    return head, dim


def _rms(x, g):
    ms = jnp.mean(x * x, axis=-1, keepdims=True)
    return x * lax.rsqrt(ms + EPS) * g


def _sigmoid(x):
    return 0.5 * jnp.tanh(0.5 * x) + 0.5


def _lane_iota(shape):
    return lax.broadcasted_iota(jnp.int32, shape, len(shape) - 1)


def _layer_spec(l, shape, **kwargs):
    return pl.BlockSpec((None,) + tuple(shape), lambda *_: (l,) + (0,) * len(shape), **kwargs)


def _dot_cols(lhs, w_ref, start, stop):
    return jnp.concatenate(
        [jnp.dot(lhs, w_ref[:, c:c + MXU_TILE], preferred_element_type=F32)
         for c in range(start, stop, MXU_TILE)], axis=-1)


def _split_bf16(x):
    hi = x.astype(BF16)
    return hi, (x - hi.astype(F32)).astype(BF16)


def _swiglu_residual(x, g_ref, wgu_ref, wd_ref, o_ref):
    h = _rms(x, g_ref[...]).astype(BF16)
    acc = jnp.zeros(x.shape, F32)
    for c in range(D_FF // FFN_TF):
        a = jnp.dot(h, wgu_ref[:, c * FFN_TF:(c + 1) * FFN_TF], preferred_element_type=F32)
        b = jnp.dot(h, wgu_ref[:, D_FF + c * FFN_TF:D_FF + (c + 1) * FFN_TF],
                    preferred_element_type=F32)
        act = (a * jax.nn.sigmoid(a) * b).astype(BF16)
        acc = acc + jnp.dot(act, wd_ref[c * FFN_TF:(c + 1) * FFN_TF, :],
                            preferred_element_type=F32)
    o_ref[...] = x + 0.5 * acc


def _run_cast_jobs(src_refs, dst_refs):
    for src, dst in zip(src_refs, dst_refs):
        dst[...] = src[...].astype(dst.dtype)


def _ffn_kernel(x_ref, g_ref, wgu_ref, wd_ref, *rest):
    n_jobs = (len(rest) - 1) // 2
    o_ref = rest[n_jobs]
    _swiglu_residual(x_ref[...], g_ref, wgu_ref, wd_ref, o_ref)
    _run_cast_jobs(rest[:n_jobs], rest[n_jobs + 1:])


def _merge_ffn_kernel(x_ref, p_ref, g1_ref, ya_ref, wba_ref, wo_ref, g_ref, wgu_ref, wd_ref,
                      *rest):
    n_jobs = (len(rest) - 1) // 2
    o_ref = rest[n_jobs]
    merged = p_ref[...] + g1_ref[...] * jnp.dot(ya_ref[...], wba_ref[...],
                                                preferred_element_type=F32)
    x = x_ref[...] + jnp.dot(merged.astype(BF16), wo_ref[...], preferred_element_type=F32)
    _swiglu_residual(x, g_ref, wgu_ref, wd_ref, o_ref)
    _run_cast_jobs(rest[:n_jobs], rest[n_jobs + 1:])


def _cast_job_specs(jobs, n_steps):
    in_specs, out_specs, out_shapes = [], [], []
    for w, l in jobs:
        _, r, c = w.shape
        n_blocks = max(nb for nb in range(1, n_steps + 1)
                       if n_steps % nb == 0 and r % (nb * BF16_TILE_ROWS) == 0)
        rows = r // n_blocks
        in_specs.append(pl.BlockSpec(
            (None, rows, c), lambda i, l=l, n_blocks=n_blocks: (l, (i * n_blocks) // n_steps, 0)))
        out_specs.append(pl.BlockSpec(
            (rows, c), lambda i, n_blocks=n_blocks: ((i * n_blocks) // n_steps, 0)))
        out_shapes.append(jax.ShapeDtypeStruct((r, c), BF16))
    return in_specs, out_specs, out_shapes


def _resident(w):
    return pl.BlockSpec(w.shape, lambda *_: (0, 0), pipeline_mode=pl.Buffered(1))


def _merge_ffn(l, x2d, p2d, g12d, ya2d, wba, wo, g, wgu, wd, cast_jobs=()):
    t = x2d.shape[0]
    n_steps = t // FFN_TM
    row = pl.BlockSpec((FFN_TM, D_MODEL), lambda i: (i, 0))
    job_in, job_out, job_shapes = _cast_job_specs(cast_jobs, n_steps)
    outs = pl.pallas_call(
        _merge_ffn_kernel,
        grid=(n_steps,),
        in_specs=[
            row, row, row,
            pl.BlockSpec((FFN_TM, ATTN_WIDTH), lambda i: (i, 0)),
            _resident(wba), _resident(wo),
            _layer_spec(l, (1, D_MODEL)),
            _resident(wgu), _resident(wd),
        ] + job_in,
        out_specs=[row] + job_out,
        out_shape=[jax.ShapeDtypeStruct((t, D_MODEL), F32)] + job_shapes,
        compiler_params=pltpu.CompilerParams(
            dimension_semantics=("arbitrary",), vmem_limit_bytes=VMEM_LIMIT),
        name="merge_ffn",
    )(x2d, p2d, g12d, ya2d, wba, wo, g, wgu, wd, *[w for w, _ in cast_jobs])
    return outs[0], outs[1:]


def _ffn(l, x2d, g, wgu, wd, cast_jobs=()):
    t = x2d.shape[0]
    n_steps = t // FFN_TM
    row = pl.BlockSpec((FFN_TM, D_MODEL), lambda i: (i, 0))
    job_in, job_out, job_shapes = _cast_job_specs(cast_jobs, n_steps)
    outs = pl.pallas_call(
        _ffn_kernel,
        grid=(n_steps,),
        in_specs=[row, _layer_spec(l, (1, D_MODEL)), _resident(wgu), _resident(wd)] + job_in,
        out_specs=[row] + job_out,
        out_shape=[jax.ShapeDtypeStruct((t, D_MODEL), F32)] + job_shapes,
        compiler_params=pltpu.CompilerParams(
            dimension_semantics=("arbitrary",), vmem_limit_bytes=VMEM_LIMIT),
        name="ffn",
    )(x2d, g, wgu, wd, *[w for w, _ in cast_jobs])
    return outs[0], outs[1:]


def _head_rms_rope(x, gmat, gain, cos, sin_signed):
    hi, lo = _split_bf16(x * x)
    ms = jnp.concatenate(
        [jnp.dot(hi[:, c:c + MXU_TILE], gmat, preferred_element_type=F32)
         + jnp.dot(lo[:, c:c + MXU_TILE], gmat, preferred_element_type=F32)
         for c in range(0, ATTN_WIDTH, MXU_TILE)], axis=-1)
    y = x * lax.rsqrt(ms + EPS) * gain
    outs = []
    for p in range(ATTN_WIDTH // LANES):
        yp = y[:, p * LANES:(p + 1) * LANES]
        outs.append(yp * cos + pltpu.roll(yp, HEAD_DIM, 1) * sin_signed)
    return outs


def _window_sums(halo, u, steps):
    s = jnp.concatenate([halo, u], axis=0)
    for t in range(steps):
        s = s + pltpu.roll(s, 1 << t, 0)
    return s[HALO:, :]


def _choose_blocks(gate_t, i):
    nq = gate_t.shape[1]
    g3 = gate_t.reshape(N_HEADS, N_SLOTS, nq)
    slot = lax.broadcasted_iota(jnp.int32, g3.shape, 1)
    gm = jnp.where(slot < i, g3, NEG)
    rank = jnp.zeros(g3.shape, jnp.int32)
    for j in range(N_SLOTS):
        other = jnp.broadcast_to(gm[:, j:j + 1, :], g3.shape)
        ahead = (other > gm) | ((other == gm) & (slot > j))
        rank = rank + jnp.where(ahead, 1, 0)
    chosen = (rank < MOBA_TOP_K) & (slot < i)
    bias = jnp.where(chosen | (slot == i), 0.0, NEG)
    return bias.reshape(N_HEADS * N_SLOTS, nq)


def _inproj_rows(row0, x, mixg, win_ref, bg, poolw_ref, pools, qg, kg, cos, sin_signed, gmat_ref,
                 wbp_ref, halo_ref):
    tm = x.shape[0]
    h = _rms(x, mixg).astype(BF16)

    u = _dot_cols(h, win_ref, 0, POOL_WIDTH)
    pos = row0 + lax.broadcasted_iota(jnp.int32, (tm, 1), 0)
    ds = []
    for g, w in enumerate(POOL_WINDOWS):
        lanes = slice(g * POOL_GROUP, (g + 1) * POOL_GROUP)
        wsum = _window_sums(halo_ref[:, lanes], u[:, lanes], g + 1)
        cnt = jnp.minimum(pos + 1, w).astype(F32)
        ds.append((wsum / cnt - u[:, lanes]).astype(BF16))
    per_tile = MXU_TILE // POOL_GROUP
    ys = [jnp.dot(jnp.concatenate(ds[t * per_tile:(t + 1) * per_tile], axis=-1), poolw_ref[t],
                  preferred_element_type=F32) for t in range(len(ds) // per_tile)]
    y_pool = (jnp.concatenate(ys, axis=-1) * pools).astype(BF16)
    halo_ref[...] = u[tm - HALO:, :]

    gl = _dot_cols(h, win_ref, POOL_WIDTH + 3 * ATTN_WIDTH, win_ref.shape[1])
    gates = _sigmoid(gl + bg)
    p_out = gates[:, :D_MODEL] * _dot_cols(y_pool, wbp_ref, 0, D_MODEL)
    g1_out = gates[:, D_MODEL:]

    o1 = POOL_WIDTH
    q = _dot_cols(h, win_ref, o1, o1 + ATTN_WIDTH)
    k = _dot_cols(h, win_ref, o1 + ATTN_WIDTH, o1 + 2 * ATTN_WIDTH)
    v = _dot_cols(h, win_ref, o1 + 2 * ATTN_WIDTH, o1 + 3 * ATTN_WIDTH)
    q_tiles = _head_rms_rope(q, gmat_ref[...], qg, cos, sin_signed)
    k_tiles = _head_rms_rope(k, gmat_ref[...], kg, cos, sin_signed)
    return p_out, g1_out, q_tiles, k_tiles, v


def _inproj_block(i, q_tiles, k_tiles, v, hmask_ref, kbt_ref):
    tm = v.shape[0]
    qn = jnp.concatenate(q_tiles, axis=-1)
    kn = jnp.concatenate(k_tiles, axis=-1)

    nt_dims = (((1,), (1,)), ((), ()))
    q_hi, q_lo = _split_bf16(qn)
    kb_hi, kb_lo = _split_bf16(kbt_ref[...])
    gate_t = (lax.dot_general(kb_hi, q_hi, nt_dims, preferred_element_type=F32)
              + lax.dot_general(kb_hi, q_lo, nt_dims, preferred_element_type=F32)
              + lax.dot_general(kb_lo, q_hi, nt_dims, preferred_element_type=F32))
    bias = _choose_blocks(gate_t, i).T

    kbar = jnp.sum(kn, axis=0, keepdims=True) * (1.0 / MOBA_BLOCK)
    row = lax.broadcasted_iota(jnp.int32, kbt_ref.shape, 0)
    mine = (row % N_SLOTS == i) & (hmask_ref[...] > 0.0)
    kbt_ref[...] = jnp.where(mine, kbar, kbt_ref[...])

    lane = _lane_iota((tm, LANES))
    ones_rows = jnp.where(lax.broadcasted_iota(jnp.int32, (VT_ROWS - HEAD_DIM, tm), 0) == 0,
                          1.0, 0.0)
    qa, ka, vat = [], [], []
    for hd in range(N_HEADS):
        p, odd, grp = hd // 2, hd % 2, _HEAD_GROUP[hd]
        own = (lane // _HALF) % 2 == odd
        vp = v[:, p * LANES:(p + 1) * LANES]
        qa.append(jnp.where(own, q_tiles[p] * (HEAD_DIM ** -0.5),
                            jnp.where(lane // N_SLOTS == grp, bias, 0.0)).astype(BF16))
        ka.append(jnp.where(own, k_tiles[p],
                            jnp.where(lane == grp * N_SLOTS + i, 1.0, 0.0)).astype(BF16))
        vt = vp.T[odd * HEAD_DIM:(odd + 1) * HEAD_DIM, :]
        vat.append(jnp.concatenate([vt, ones_rows], axis=0).astype(BF16))
    return qa, ka, vat


def _inproj_kernel(x_ref, mixg_ref, win_ref, bg_ref, poolw_ref, pools_ref, qg_ref, kg_ref,
                   cos_ref, sin_ref, gmat_ref, wbp_ref, hmask_ref,
                   qa_ref, ka_ref, vat_ref, p_ref, g1_ref,
                   halo_ref, kbt_ref):
    step = pl.program_id(1)
    step_rows = x_ref.shape[1]

    @pl.when(step == 0)
    def _():
        halo_ref[...] = jnp.zeros(halo_ref.shape, F32)
        kbt_ref[...] = jnp.zeros(kbt_ref.shape, F32)

    for r0 in range(0, step_rows, INPROJ_MM_ROWS):
        rows = slice(r0, r0 + INPROJ_MM_ROWS)
        p_out, g1_out, q_tiles, k_tiles, v = _inproj_rows(
            step * step_rows + r0, x_ref[0, rows, :], mixg_ref[...], win_ref, bg_ref[...],
            poolw_ref, pools_ref[...], qg_ref[...], kg_ref[...], cos_ref[rows, :],
            sin_ref[rows, :], gmat_ref, wbp_ref, halo_ref)
        p_ref[0, rows, :] = p_out
        g1_ref[0, rows, :] = g1_out
        for t0 in range(0, INPROJ_MM_ROWS, MOBA_BLOCK):
            sub = slice(t0, t0 + MOBA_BLOCK)
            blk = slice(r0 + t0, r0 + t0 + MOBA_BLOCK)
            qa, ka, vat = _inproj_block(
                (step * step_rows + r0 + t0) // MOBA_BLOCK, [q[sub] for q in q_tiles],
                [k[sub] for k in k_tiles], v[sub], hmask_ref, kbt_ref)
            for hd in range(N_HEADS):
                qa_ref[0, hd, blk, :] = qa[hd]
                ka_ref[0, hd, blk, :] = ka[hd]
                vat_ref[0, hd, :, blk] = vat[hd]


def _inproj(l, x, mixg, win, bg, poolw, pools, qg, kg, cos, sin_signed, gmat, wbp, hmask):
    b, s, _ = x.shape
    tm = INPROJ_TILES * MOBA_BLOCK
    const = lambda *shape: pl.BlockSpec(shape, lambda bi, i: (0,) * len(shape))
    head_spec = pl.BlockSpec((1, N_HEADS, tm, LANES), lambda bi, i: (bi, 0, i, 0))
    headt_spec = pl.BlockSpec((1, N_HEADS, VT_ROWS, tm), lambda bi, i: (bi, 0, 0, i))
    headt_shape = jax.ShapeDtypeStruct((b, N_HEADS, VT_ROWS, s), BF16)
    row_spec = pl.BlockSpec((1, tm, D_MODEL), lambda bi, i: (bi, i, 0))
    head_shape = jax.ShapeDtypeStruct((b, N_HEADS, s, LANES), BF16)
    row_shape = jax.ShapeDtypeStruct((b, s, D_MODEL), F32)
    return pl.pallas_call(
        _inproj_kernel,
        grid=(b, s // tm),
        in_specs=[
            row_spec,
            _layer_spec(l, (1, D_MODEL)),
            _layer_spec(l, win.shape[1:], pipeline_mode=pl.Buffered(1)),
            _layer_spec(l, (1, 2 * D_MODEL)),
            _layer_spec(l, poolw.shape[1:]),
            _layer_spec(l, (1, POOL_WIDTH)),
            _layer_spec(l, (1, ATTN_WIDTH)),
            _layer_spec(l, (1, ATTN_WIDTH)),
            pl.BlockSpec((tm, LANES), lambda bi, i: (i, 0)),
            pl.BlockSpec((tm, LANES), lambda bi, i: (i, 0)),
            const(MXU_TILE, MXU_TILE),
            _resident(wbp),
            const(N_HEADS * N_SLOTS, ATTN_WIDTH),
        ],
        out_specs=[head_spec, head_spec, headt_spec, row_spec, row_spec],
        out_shape=[head_shape, head_shape, headt_shape, row_shape, row_shape],
        scratch_shapes=[
            pltpu.VMEM((HALO, POOL_WIDTH), F32),
            pltpu.VMEM((N_HEADS * N_SLOTS, ATTN_WIDTH), F32),
        ],
        compiler_params=pltpu.CompilerParams(
            dimension_semantics=("arbitrary", "arbitrary"), vmem_limit_bytes=VMEM_LIMIT),
        name="inproj",
    )(x, mixg, win, bg, poolw, pools, qg, kg, cos, sin_signed, gmat, wbp, hmask)


def _attn_tile_group(i, nk, q_ref, k_ref, vt_ref, o_ref, s_refs):
    tq = q_ref.shape[2]
    nt_dims = (((1,), (1,)), ((), ()))
    rel = (lax.broadcasted_iota(jnp.int32, (tq, tq), 0)
           - lax.broadcasted_iota(jnp.int32, (tq, tq), 1))

    def scores(hh):
        q = q_ref[0, hh]
        for j in range(nk):
            s = lax.dot_general(k_ref[0, hh, j * tq:(j + 1) * tq, :], q, nt_dims,
                                preferred_element_type=F32)
            if j >= nk - ATTN_GROUP:
                s = jnp.where(rel <= (i - j) * tq, s, NEG)
            s_refs[hh % len(s_refs)][j] = s

    def softmax_pv(hh):
        s_ref = s_refs[hh % len(s_refs)]
        mrun = s_ref[0]
        for j in range(1, nk):
            mrun = jnp.maximum(mrun, s_ref[j])
        m = jnp.broadcast_to(jnp.max(mrun, axis=0, keepdims=True), (tq, tq))
        pt = jnp.concatenate([jnp.exp(s_ref[j] - m).astype(BF16) for j in range(nk)], axis=0)
        acc = jnp.dot(vt_ref[0, hh, :, 0:nk * tq], pt, preferred_element_type=F32)
        return acc[:HEAD_DIM, :] / acc[HEAD_DIM:HEAD_DIM + 1, :]

    outs = []
    scores(0)
    for hh in range(ATTN_HEADS):
        if hh + 1 < ATTN_HEADS:
            scores(hh + 1)
        outs.append(softmax_pv(hh))
        if hh % 2:
            pair = jnp.concatenate([outs[hh - 1], outs[hh]], axis=0)
            o_ref[0, :, (hh // 2) * LANES:(hh // 2 + 1) * LANES] = pair.T.astype(o_ref.dtype)


def _attn_kernel(q_ref, k_ref, vt_ref, o_ref, *s_refs):
    i = pl.program_id(2)
    for c in range(N_SLOTS // ATTN_GROUP):
        @pl.when(i // ATTN_GROUP == c)
        def _(c=c):
            _attn_tile_group(i, ATTN_GROUP * (c + 1), q_ref, k_ref, vt_ref, o_ref, s_refs)


def _attn(qa, ka, vat):
    b, nh, s, _ = qa.shape
    tq = MOBA_BLOCK
    k_spec = pl.BlockSpec((1, ATTN_HEADS, s, LANES), lambda bi, p, i: (bi, p, 0, 0))
    vt_spec = pl.BlockSpec((1, ATTN_HEADS, VT_ROWS, s), lambda bi, p, i: (bi, p, 0, 0))
    return pl.pallas_call(
        _attn_kernel,
        grid=(b, nh // ATTN_HEADS, s // tq),
        in_specs=[pl.BlockSpec((1, ATTN_HEADS, tq, LANES), lambda bi, p, i: (bi, p, i, 0)),
                  k_spec, vt_spec],
        out_specs=pl.BlockSpec((1, tq, HEAD_DIM * ATTN_HEADS), lambda bi, p, i: (bi, i, p)),
        out_shape=jax.ShapeDtypeStruct((b, s, ATTN_WIDTH), BF16),
        scratch_shapes=[pltpu.VMEM((s // tq, tq, tq), F32)] * ATTN_SCORE_BUFS,
        compiler_params=pltpu.CompilerParams(
            dimension_semantics=("arbitrary", "arbitrary", "arbitrary"),
            vmem_limit_bytes=VMEM_LIMIT),
        name="attn",
    )(qa, ka, vat)


def _rope_tables(s):
    inv_freq = 1.0 / ROPE_THETA ** (np.arange(_HALF, dtype=np.float64) * (2.0 / HEAD_DIM))
    ang = np.arange(s, dtype=np.float64)[:, None] * inv_freq[None, :]
    cos, sin = np.cos(ang), np.sin(ang)
    cos = np.tile(cos, (1, LANES // _HALF))
    sin_signed = np.concatenate([-sin, -sin, sin, sin], axis=-1)
    return jnp.asarray(cos, F32), jnp.asarray(sin_signed, F32)


def _permute_heads(w):
    lead = w.shape[:-1]
    w = w.reshape(lead + (N_HEADS // 2, 2, 2, _HALF))
    return jnp.swapaxes(w, -3, -2).reshape(lead + (ATTN_WIDTH,))


def kernel(x, ffn1_norm, ffn1_w_gate_up, ffn1_w_down, mix_norm, w_in, b_gate, pool_w, pool_scale,
           q_norm, k_norm, w_branch_pool, w_branch_attn, w_out, ffn2_norm, ffn2_w_gate_up,
           ffn2_w_down):
    b, s, d = x.shape
    assert s == N_SLOTS * MOBA_BLOCK and d == D_MODEL
    depth = ffn1_norm.shape[0]
    cos, sin_signed = _rope_tables(s)
    head, dim = _qk_layout()
    gmat = jnp.asarray((head[:MXU_TILE, None] == head[None, :MXU_TILE]) / HEAD_DIM, BF16)
    hmask = jnp.asarray(np.repeat(np.asarray(_GROUP_HEAD), N_SLOTS)[:, None] == head[None, :], F32)
    o1, o2, o3 = POOL_WIDTH, POOL_WIDTH + ATTN_WIDTH, POOL_WIDTH + 2 * ATTN_WIDTH

    row = lambda p: p[:, None, :]
    win = jnp.concatenate([w_in[..., :o1], _permute_heads(w_in[..., o1:o2]),
                           _permute_heads(w_in[..., o2:o3]), w_in[..., o3:]], axis=-1).astype(BF16)
    qg = row(q_norm[:, dim])
    kg = row(k_norm[:, dim])
    per_tile = MXU_TILE // POOL_GROUP
    n_tiles = len(POOL_WINDOWS) // per_tile
    poolw = jnp.einsum('ltaij,ab->ltaibj',
                       pool_w.reshape(depth, n_tiles, per_tile, POOL_GROUP, POOL_GROUP),
                       jnp.eye(per_tile, dtype=F32)
                       ).reshape(depth, n_tiles, MXU_TILE, MXU_TILE).astype(BF16)

    wgu, wd = ffn1_w_gate_up[0].astype(BF16), ffn1_w_down[0].astype(BF16)
    x2d = x.reshape(b * s, d)
    for l in range(depth):
        mixer_jobs = [(ffn2_w_gate_up, l), (ffn2_w_down, l), (w_branch_pool, l),
                      (w_branch_attn, l), (w_out, l)]
        x2d, (wgu2, wd2, wbp, wba, wo) = _ffn(l, x2d, row(ffn1_norm), wgu, wd, mixer_jobs)
        qa, ka, vat, p, g1 = _inproj(
            l, x2d.reshape(b, s, d), row(mix_norm), win, row(b_gate), poolw, row(pool_scale),
            qg, kg, cos, sin_signed, gmat, wbp, hmask)
        ya = _attn(qa, ka, vat)
        next_jobs = [(ffn1_w_gate_up, l + 1), (ffn1_w_down, l + 1)] if l + 1 < depth else []
        x2d, nxt = _merge_ffn(l, x2d, p.reshape(b * s, d), g1.reshape(b * s, d),
                              ya.reshape(b * s, ATTN_WIDTH), wba, wo, row(ffn2_norm), wgu2, wd2,
                              next_jobs)
        if nxt:
            wgu, wd = nxt
    return x2d.reshape(b, s, d)
```

```python
import numpy as np

import jax
import jax.numpy as jnp
from jax import lax
from jax.experimental import pallas as pl
from jax.experimental.pallas import tpu as pltpu

F32 = jnp.float32
BF16 = jnp.bfloat16

D_MODEL = 1024
D_FF = 2816
POOL_WINDOWS = (2, 4, 8, 16)
POOL_WIDTH = 512
POOL_GROUP = 128
N_HEADS = 8
HEAD_DIM = 64
ATTN_WIDTH = 512
MOBA_BLOCK = 256
MOBA_TOP_K = 3
ROPE_THETA = 10000.0
EPS = 1e-6
NEG = -1e30

LANES = 128
MXU_TILE = 256
BF16_TILE_ROWS = 16
VT_ROWS = 80
HALO = 16
N_SLOTS = 16
FFN_TM = 512
FFN_TF = 256
INPROJ_TILES = 4
INPROJ_MM_ROWS = 256
ATTN_GROUP = 2
ATTN_HEADS = 4
ATTN_SCORE_BUFS = 4
VMEM_LIMIT = 60 * 1024 * 1024

_HALF = HEAD_DIM // 2
_GROUP_HEAD = (1, 3, 0, 2, 5, 7, 4, 6)
_HEAD_GROUP = tuple(_GROUP_HEAD.index(h) for h in range(N_HEADS))


def _qk_layout():
    c = np.arange(ATTN_WIDTH)
    lane = c % LANES
    head = 2 * (c // LANES) + (lane // _HALF) % 2
    dim = lane % _HALF + _HALF * (lane // HEAD_DIM)
    return head, dim


def _rms(x, g):
    ms = jnp.mean(x * x, axis=-1, keepdims=True)
    return x * lax.rsqrt(ms + EPS) * g


def _sigmoid(x):
    return 0.5 * jnp.tanh(0.5 * x) + 0.5


def _lane_iota(shape):
    return lax.broadcasted_iota(jnp.int32, shape, len(shape) - 1)


def _layer_spec(l, shape, **kwargs):
    return pl.BlockSpec((None,) + tuple(shape), lambda *_: (l,) + (0,) * len(shape), **kwargs)


def _dot_cols(lhs, w_ref, start, stop):
    return jnp.concatenate(
        [jnp.dot(lhs, w_ref[:, c:c + MXU_TILE], preferred_element_type=F32)
         for c in range(start, stop, MXU_TILE)], axis=-1)


def _split_bf16(x):
    hi = x.astype(BF16)
    return hi, (x - hi.astype(F32)).astype(BF16)


def _swiglu_residual(x, g_ref, wgu_ref, wd_ref, o_ref):
    h = _rms(x, g_ref[...]).astype(BF16)
    acc = jnp.zeros(x.shape, F32)
    for c in range(D_FF // FFN_TF):
        a = jnp.dot(h, wgu_ref[:, c * FFN_TF:(c + 1) * FFN_TF], preferred_element_type=F32)
        b = jnp.dot(h, wgu_ref[:, D_FF + c * FFN_TF:D_FF + (c + 1) * FFN_TF],
                    preferred_element_type=F32)
        act = (a * jax.nn.sigmoid(a) * b).astype(BF16)
        acc = acc + jnp.dot(act, wd_ref[c * FFN_TF:(c + 1) * FFN_TF, :],
                            preferred_element_type=F32)
    o_ref[...] = x + 0.5 * acc


def _run_cast_jobs(src_refs, dst_refs):
    for src, dst in zip(src_refs, dst_refs):
        dst[...] = src[...].astype(dst.dtype)


def _ffn_kernel(x_ref, g_ref, wgu_ref, wd_ref, *rest):
    n_jobs = (len(rest) - 1) // 2
    o_ref = rest[n_jobs]
    _swiglu_residual(x_ref[...], g_ref, wgu_ref, wd_ref, o_ref)
    _run_cast_jobs(rest[:n_jobs], rest[n_jobs + 1:])


def _merge_ffn_kernel(x_ref, p_ref, g1_ref, ya_ref, wba_ref, wo_ref, g_ref, wgu_ref, wd_ref,
                      *rest):
    n_jobs = (len(rest) - 1) // 2
    o_ref = rest[n_jobs]
    merged = p_ref[...] + g1_ref[...] * jnp.dot(ya_ref[...], wba_ref[...],
                                                preferred_element_type=F32)
    x = x_ref[...] + jnp.dot(merged.astype(BF16), wo_ref[...], preferred_element_type=F32)
    _swiglu_residual(x, g_ref, wgu_ref, wd_ref, o_ref)
    _run_cast_jobs(rest[:n_jobs], rest[n_jobs + 1:])


def _cast_job_specs(jobs, n_steps):
    in_specs, out_specs, out_shapes = [], [], []
    for w, l in jobs:
        _, r, c = w.shape
        n_blocks = max(nb for nb in range(1, n_steps + 1)
                       if n_steps % nb == 0 and r % (nb * BF16_TILE_ROWS) == 0)
        rows = r // n_blocks
        in_specs.append(pl.BlockSpec(
            (None, rows, c), lambda i, l=l, n_blocks=n_blocks: (l, (i * n_blocks) // n_steps, 0)))
        out_specs.append(pl.BlockSpec(
            (rows, c), lambda i, n_blocks=n_blocks: ((i * n_blocks) // n_steps, 0)))
        out_shapes.append(jax.ShapeDtypeStruct((r, c), BF16))
    return in_specs, out_specs, out_shapes


def _resident(w):
    return pl.BlockSpec(w.shape, lambda *_: (0, 0), pipeline_mode=pl.Buffered(1))


def _merge_ffn(l, x2d, p2d, g12d, ya2d, wba, wo, g, wgu, wd, cast_jobs=()):
    t = x2d.shape[0]
    n_steps = t // FFN_TM
    row = pl.BlockSpec((FFN_TM, D_MODEL), lambda i: (i, 0))
    job_in, job_out, job_shapes = _cast_job_specs(cast_jobs, n_steps)
    outs = pl.pallas_call(
        _merge_ffn_kernel,
        grid=(n_steps,),
        in_specs=[
            row, row, row,
            pl.BlockSpec((FFN_TM, ATTN_WIDTH), lambda i: (i, 0)),
            _resident(wba), _resident(wo),
            _layer_spec(l, (1, D_MODEL)),
            _resident(wgu), _resident(wd),
        ] + job_in,
        out_specs=[row] + job_out,
        out_shape=[jax.ShapeDtypeStruct((t, D_MODEL), F32)] + job_shapes,
        compiler_params=pltpu.CompilerParams(
            dimension_semantics=("arbitrary",), vmem_limit_bytes=VMEM_LIMIT),
        name="merge_ffn",
    )(x2d, p2d, g12d, ya2d, wba, wo, g, wgu, wd, *[w for w, _ in cast_jobs])
    return outs[0], outs[1:]


def _ffn(l, x2d, g, wgu, wd, cast_jobs=()):
    t = x2d.shape[0]
    n_steps = t // FFN_TM
    row = pl.BlockSpec((FFN_TM, D_MODEL), lambda i: (i, 0))
    job_in, job_out, job_shapes = _cast_job_specs(cast_jobs, n_steps)
    outs = pl.pallas_call(
        _ffn_kernel,
        grid=(n_steps,),
        in_specs=[row, _layer_spec(l, (1, D_MODEL)), _resident(wgu), _resident(wd)] + job_in,
        out_specs=[row] + job_out,
        out_shape=[jax.ShapeDtypeStruct((t, D_MODEL), F32)] + job_shapes,
        compiler_params=pltpu.CompilerParams(
            dimension_semantics=("arbitrary",), vmem_limit_bytes=VMEM_LIMIT),
        name="ffn",
    )(x2d, g, wgu, wd, *[w for w, _ in cast_jobs])
    return outs[0], outs[1:]


def _head_rms_rope(x, gmat, gain, cos, sin_signed):
    hi, lo = _split_bf16(x * x)
    ms = jnp.concatenate(
        [jnp.dot(hi[:, c:c + MXU_TILE], gmat, preferred_element_type=F32)
         + jnp.dot(lo[:, c:c + MXU_TILE], gmat, preferred_element_type=F32)
         for c in range(0, ATTN_WIDTH, MXU_TILE)], axis=-1)
    y = x * lax.rsqrt(ms + EPS) * gain
    outs = []
    for p in range(ATTN_WIDTH // LANES):
        yp = y[:, p * LANES:(p + 1) * LANES]
        outs.append(yp * cos + pltpu.roll(yp, HEAD_DIM, 1) * sin_signed)
    return outs


def _window_sums(halo, u, steps):
    s = jnp.concatenate([halo, u], axis=0)
    for t in range(steps):
        s = s + pltpu.roll(s, 1 << t, 0)
    return s[HALO:, :]


def _choose_blocks(gate_t, i):
    nq = gate_t.shape[1]
    g3 = gate_t.reshape(N_HEADS, N_SLOTS, nq)
    slot = lax.broadcasted_iota(jnp.int32, g3.shape, 1)
    gm = jnp.where(slot < i, g3, NEG)
    rank = jnp.zeros(g3.shape, jnp.int32)
    for j in range(N_SLOTS):
        other = jnp.broadcast_to(gm[:, j:j + 1, :], g3.shape)
        ahead = (other > gm) | ((other == gm) & (slot > j))
        rank = rank + jnp.where(ahead, 1, 0)
    chosen = (rank < MOBA_TOP_K) & (slot < i)
    bias = jnp.where(chosen | (slot == i), 0.0, NEG)
    return bias.reshape(N_HEADS * N_SLOTS, nq)


def _inproj_rows(row0, x, mixg, win_ref, bg, poolw_ref, pools, qg, kg, cos, sin_signed, gmat_ref,
                 wbp_ref, halo_ref):
    tm = x.shape[0]
    h = _rms(x, mixg).astype(BF16)

    u = _dot_cols(h, win_ref, 0, POOL_WIDTH)
    pos = row0 + lax.broadcasted_iota(jnp.int32, (tm, 1), 0)
    ds = []
    for g, w in enumerate(POOL_WINDOWS):
        lanes = slice(g * POOL_GROUP, (g + 1) * POOL_GROUP)
        wsum = _window_sums(halo_ref[:, lanes], u[:, lanes], g + 1)
        cnt = jnp.minimum(pos + 1, w).astype(F32)
        ds.append((wsum / cnt - u[:, lanes]).astype(BF16))
    per_tile = MXU_TILE // POOL_GROUP
    ys = [jnp.dot(jnp.concatenate(ds[t * per_tile:(t + 1) * per_tile], axis=-1), poolw_ref[t],
                  preferred_element_type=F32) for t in range(len(ds) // per_tile)]
    y_pool = (jnp.concatenate(ys, axis=-1) * pools).astype(BF16)
    halo_ref[...] = u[tm - HALO:, :]

    gl = _dot_cols(h, win_ref, POOL_WIDTH + 3 * ATTN_WIDTH, win_ref.shape[1])
    gates = _sigmoid(gl + bg)
    p_out = gates[:, :D_MODEL] * _dot_cols(y_pool, wbp_ref, 0, D_MODEL)
    g1_out = gates[:, D_MODEL:]

    o1 = POOL_WIDTH
    q = _dot_cols(h, win_ref, o1, o1 + ATTN_WIDTH)
    k = _dot_cols(h, win_ref, o1 + ATTN_WIDTH, o1 + 2 * ATTN_WIDTH)
    v = _dot_cols(h, win_ref, o1 + 2 * ATTN_WIDTH, o1 + 3 * ATTN_WIDTH)
    q_tiles = _head_rms_rope(q, gmat_ref[...], qg, cos, sin_signed)
    k_tiles = _head_rms_rope(k, gmat_ref[...], kg, cos, sin_signed)
    return p_out, g1_out, q_tiles, k_tiles, v


def _inproj_block(i, q_tiles, k_tiles, v, hmask_ref, kbt_ref):
    tm = v.shape[0]
    qn = jnp.concatenate(q_tiles, axis=-1)
    kn = jnp.concatenate(k_tiles, axis=-1)

    nt_dims = (((1,), (1,)), ((), ()))
    q_hi, q_lo = _split_bf16(qn)
    kb_hi, kb_lo = _split_bf16(kbt_ref[...])
    gate_t = (lax.dot_general(kb_hi, q_hi, nt_dims, preferred_element_type=F32)
              + lax.dot_general(kb_hi, q_lo, nt_dims, preferred_element_type=F32)
              + lax.dot_general(kb_lo, q_hi, nt_dims, preferred_element_type=F32))
    bias = _choose_blocks(gate_t, i).T

    kbar = jnp.sum(kn, axis=0, keepdims=True) * (1.0 / MOBA_BLOCK)
    row = lax.broadcasted_iota(jnp.int32, kbt_ref.shape, 0)
    mine = (row % N_SLOTS == i) & (hmask_ref[...] > 0.0)
    kbt_ref[...] = jnp.where(mine, kbar, kbt_ref[...])

    lane = _lane_iota((tm, LANES))
    ones_rows = jnp.where(lax.broadcasted_iota(jnp.int32, (VT_ROWS - HEAD_DIM, tm), 0) == 0,
                          1.0, 0.0)
    qa, ka, vat = [], [], []
    for hd in range(N_HEADS):
        p, odd, grp = hd // 2, hd % 2, _HEAD_GROUP[hd]
        own = (lane // _HALF) % 2 == odd
        vp = v[:, p * LANES:(p + 1) * LANES]
        qa.append(jnp.where(own, q_tiles[p] * (HEAD_DIM ** -0.5),
                            jnp.where(lane // N_SLOTS == grp, bias, 0.0)).astype(BF16))
        ka.append(jnp.where(own, k_tiles[p],
                            jnp.where(lane == grp * N_SLOTS + i, 1.0, 0.0)).astype(BF16))
        vt = vp.T[odd * HEAD_DIM:(odd + 1) * HEAD_DIM, :]
        vat.append(jnp.concatenate([vt, ones_rows], axis=0).astype(BF16))
    return qa, ka, vat


def _inproj_kernel(x_ref, mixg_ref, win_ref, bg_ref, poolw_ref, pools_ref, qg_ref, kg_ref,
                   cos_ref, sin_ref, gmat_ref, wbp_ref, hmask_ref,
                   qa_ref, ka_ref, vat_ref, p_ref, g1_ref,
                   halo_ref, kbt_ref):
    step = pl.program_id(1)
    step_rows = x_ref.shape[1]

    @pl.when(step == 0)
    def _():
        halo_ref[...] = jnp.zeros(halo_ref.shape, F32)
        kbt_ref[...] = jnp.zeros(kbt_ref.shape, F32)

    for r0 in range(0, step_rows, INPROJ_MM_ROWS):
        rows = slice(r0, r0 + INPROJ_MM_ROWS)
        p_out, g1_out, q_tiles, k_tiles, v = _inproj_rows(
            step * step_rows + r0, x_ref[0, rows, :], mixg_ref[...], win_ref, bg_ref[...],
            poolw_ref, pools_ref[...], qg_ref[...], kg_ref[...], cos_ref[rows, :],
            sin_ref[rows, :], gmat_ref, wbp_ref, halo_ref)
        p_ref[0, rows, :] = p_out
        g1_ref[0, rows, :] = g1_out
        for t0 in range(0, INPROJ_MM_ROWS, MOBA_BLOCK):
            sub = slice(t0, t0 + MOBA_BLOCK)
            blk = slice(r0 + t0, r0 + t0 + MOBA_BLOCK)
            qa, ka, vat = _inproj_block(
                (step * step_rows + r0 + t0) // MOBA_BLOCK, [q[sub] for q in q_tiles],
                [k[sub] for k in k_tiles], v[sub], hmask_ref, kbt_ref)
            for hd in range(N_HEADS):
                qa_ref[0, hd, blk, :] = qa[hd]
                ka_ref[0, hd, blk, :] = ka[hd]
                vat_ref[0, hd, :, blk] = vat[hd]


def _inproj(l, x, mixg, win, bg, poolw, pools, qg, kg, cos, sin_signed, gmat, wbp, hmask):
    b, s, _ = x.shape
    tm = INPROJ_TILES * MOBA_BLOCK
    const = lambda *shape: pl.BlockSpec(shape, lambda bi, i: (0,) * len(shape))
    head_spec = pl.BlockSpec((1, N_HEADS, tm, LANES), lambda bi, i: (bi, 0, i, 0))
    headt_spec = pl.BlockSpec((1, N_HEADS, VT_ROWS, tm), lambda bi, i: (bi, 0, 0, i))
    headt_shape = jax.ShapeDtypeStruct((b, N_HEADS, VT_ROWS, s), BF16)
    row_spec = pl.BlockSpec((1, tm, D_MODEL), lambda bi, i: (bi, i, 0))
    head_shape = jax.ShapeDtypeStruct((b, N_HEADS, s, LANES), BF16)
    row_shape = jax.ShapeDtypeStruct((b, s, D_MODEL), F32)
    return pl.pallas_call(
        _inproj_kernel,
        grid=(b, s // tm),
        in_specs=[
            row_spec,
            _layer_spec(l, (1, D_MODEL)),
            _layer_spec(l, win.shape[1:], pipeline_mode=pl.Buffered(1)),
            _layer_spec(l, (1, 2 * D_MODEL)),
            _layer_spec(l, poolw.shape[1:]),
            _layer_spec(l, (1, POOL_WIDTH)),
            _layer_spec(l, (1, ATTN_WIDTH)),
            _layer_spec(l, (1, ATTN_WIDTH)),
            pl.BlockSpec((tm, LANES), lambda bi, i: (i, 0)),
            pl.BlockSpec((tm, LANES), lambda bi, i: (i, 0)),
            const(MXU_TILE, MXU_TILE),
            _resident(wbp),
            const(N_HEADS * N_SLOTS, ATTN_WIDTH),
        ],
        out_specs=[head_spec, head_spec, headt_spec, row_spec, row_spec],
        out_shape=[head_shape, head_shape, headt_shape, row_shape, row_shape],
        scratch_shapes=[
            pltpu.VMEM((HALO, POOL_WIDTH), F32),
            pltpu.VMEM((N_HEADS * N_SLOTS, ATTN_WIDTH), F32),
        ],
        compiler_params=pltpu.CompilerParams(
            dimension_semantics=("arbitrary", "arbitrary"), vmem_limit_bytes=VMEM_LIMIT),
        name="inproj",
    )(x, mixg, win, bg, poolw, pools, qg, kg, cos, sin_signed, gmat, wbp, hmask)


def _attn_tile_group(i, nk, q_ref, k_ref, vt_ref, o_ref, s_refs):
    tq = MOBA_BLOCK
    q_rows = pl.ds(pl.multiple_of(i * tq, tq), tq)
    nt_dims = (((1,), (1,)), ((), ()))
    rel = (lax.broadcasted_iota(jnp.int32, (tq, tq), 0)
           - lax.broadcasted_iota(jnp.int32, (tq, tq), 1))

    def scores(hh):
        q = q_ref[0, hh, q_rows, :]
        for j in range(nk):
            s = lax.dot_general(k_ref[0, hh, j * tq:(j + 1) * tq, :], q, nt_dims,
                                preferred_element_type=F32)
            if j >= nk - ATTN_GROUP:
                s = jnp.where(rel <= (i - j) * tq, s, NEG)
            s_refs[hh % len(s_refs)][j] = s

    def softmax_pv(hh):
        s_ref = s_refs[hh % len(s_refs)]
        mrun = s_ref[0]
        for j in range(1, nk):
            mrun = jnp.maximum(mrun, s_ref[j])
        m = jnp.broadcast_to(jnp.max(mrun, axis=0, keepdims=True), (tq, tq))
        pt = jnp.concatenate([jnp.exp(s_ref[j] - m).astype(BF16) for j in range(nk)], axis=0)
        acc = jnp.dot(vt_ref[0, hh, :, 0:nk * tq], pt, preferred_element_type=F32)
        return acc[:HEAD_DIM, :] / acc[HEAD_DIM:HEAD_DIM + 1, :]

    outs = []
    scores(0)
    for hh in range(ATTN_HEADS):
        if hh + 1 < ATTN_HEADS:
            scores(hh + 1)
        outs.append(softmax_pv(hh))
        if hh % 2:
            pair = jnp.concatenate([outs[hh - 1], outs[hh]], axis=0)
            o_ref[0, q_rows, (hh // 2) * LANES:(hh // 2 + 1) * LANES] = pair.T.astype(o_ref.dtype)


def _attn_kernel(q_ref, k_ref, vt_ref, o_ref, *s_refs):
    for c in range(N_SLOTS // ATTN_GROUP):
        def body(i, carry, nk=ATTN_GROUP * (c + 1)):
            _attn_tile_group(i, nk, q_ref, k_ref, vt_ref, o_ref, s_refs)
            return carry
        lax.fori_loop(ATTN_GROUP * c, ATTN_GROUP * (c + 1), body, 0)


def _attn(qa, ka, vat):
    b, nh, s, _ = qa.shape
    tq = MOBA_BLOCK
    qk_spec = pl.BlockSpec((1, ATTN_HEADS, s, LANES), lambda bi, p: (bi, p, 0, 0))
    vt_spec = pl.BlockSpec((1, ATTN_HEADS, VT_ROWS, s), lambda bi, p: (bi, p, 0, 0))
    return pl.pallas_call(
        _attn_kernel,
        grid=(b, nh // ATTN_HEADS),
        in_specs=[qk_spec, qk_spec, vt_spec],
        out_specs=pl.BlockSpec((1, s, HEAD_DIM * ATTN_HEADS), lambda bi, p: (bi, 0, p)),
        out_shape=jax.ShapeDtypeStruct((b, s, ATTN_WIDTH), BF16),
        scratch_shapes=[pltpu.VMEM((s // tq, tq, tq), F32)] * ATTN_SCORE_BUFS,
        compiler_params=pltpu.CompilerParams(
            dimension_semantics=("arbitrary", "arbitrary"),
            vmem_limit_bytes=VMEM_LIMIT),
        name="attn",
    )(qa, ka, vat)


def _rope_tables(s):
    inv_freq = 1.0 / ROPE_THETA ** (np.arange(_HALF, dtype=np.float64) * (2.0 / HEAD_DIM))
    ang = np.arange(s, dtype=np.float64)[:, None] * inv_freq[None, :]
    cos, sin = np.cos(ang), np.sin(ang)
    cos = np.tile(cos, (1, LANES // _HALF))
    sin_signed = np.concatenate([-sin, -sin, sin, sin], axis=-1)
    return jnp.asarray(cos, F32), jnp.asarray(sin_signed, F32)


def _permute_heads(w):
    lead = w.shape[:-1]
    w = w.reshape(lead + (N_HEADS // 2, 2, 2, _HALF))
    return jnp.swapaxes(w, -3, -2).reshape(lead + (ATTN_WIDTH,))


def kernel(x, ffn1_norm, ffn1_w_gate_up, ffn1_w_down, mix_norm, w_in, b_gate, pool_w, pool_scale,
           q_norm, k_norm, w_branch_pool, w_branch_attn, w_out, ffn2_norm, ffn2_w_gate_up,
           ffn2_w_down):
    b, s, d = x.shape
    assert s == N_SLOTS * MOBA_BLOCK and d == D_MODEL
    depth = ffn1_norm.shape[0]
    cos, sin_signed = _rope_tables(s)
    head, dim = _qk_layout()
    gmat = jnp.asarray((head[:MXU_TILE, None] == head[None, :MXU_TILE]) / HEAD_DIM, BF16)
    hmask = jnp.asarray(np.repeat(np.asarray(_GROUP_HEAD), N_SLOTS)[:, None] == head[None, :], F32)
    o1, o2, o3 = POOL_WIDTH, POOL_WIDTH + ATTN_WIDTH, POOL_WIDTH + 2 * ATTN_WIDTH

    row = lambda p: p[:, None, :]
    win = jnp.concatenate([w_in[..., :o1], _permute_heads(w_in[..., o1:o2]),
                           _permute_heads(w_in[..., o2:o3]), w_in[..., o3:]], axis=-1).astype(BF16)
    qg = row(q_norm[:, dim])
    kg = row(k_norm[:, dim])
    per_tile = MXU_TILE // POOL_GROUP
    n_tiles = len(POOL_WINDOWS) // per_tile
    poolw = jnp.einsum('ltaij,ab->ltaibj',
                       pool_w.reshape(depth, n_tiles, per_tile, POOL_GROUP, POOL_GROUP),
                       jnp.eye(per_tile, dtype=F32)
                       ).reshape(depth, n_tiles, MXU_TILE, MXU_TILE).astype(BF16)

    wgu, wd = ffn1_w_gate_up[0].astype(BF16), ffn1_w_down[0].astype(BF16)
    x2d = x.reshape(b * s, d)
    for l in range(depth):
        mixer_jobs = [(ffn2_w_gate_up, l), (ffn2_w_down, l), (w_branch_pool, l),
                      (w_branch_attn, l), (w_out, l)]
        x2d, (wgu2, wd2, wbp, wba, wo) = _ffn(l, x2d, row(ffn1_norm), wgu, wd, mixer_jobs)
        qa, ka, vat, p, g1 = _inproj(
            l, x2d.reshape(b, s, d), row(mix_norm), win, row(b_gate), poolw, row(pool_scale),
            qg, kg, cos, sin_signed, gmat, wbp, hmask)
        ya = _attn(qa, ka, vat)
        next_jobs = [(ffn1_w_gate_up, l + 1), (ffn1_w_down, l + 1)] if l + 1 < depth else []
        x2d, nxt = _merge_ffn(l, x2d, p.reshape(b * s, d), g1.reshape(b * s, d),
                              ya.reshape(b * s, ATTN_WIDTH), wba, wo, row(ffn2_norm), wgu2, wd2,
                              next_jobs)
        if nxt:
            wgu, wd = nxt
    return x2d.reshape(b, s, d)
```

```python
import numpy as np

import jax
import jax.numpy as jnp
from jax import lax
from jax.experimental import pallas as pl
from jax.experimental.pallas import tpu as pltpu

F32 = jnp.float32
BF16 = jnp.bfloat16

D_MODEL = 1024
D_FF = 2816
POOL_WINDOWS = (2, 4, 8, 16)
POOL_WIDTH = 512
POOL_GROUP = 128
N_HEADS = 8
HEAD_DIM = 64
ATTN_WIDTH = 512
MOBA_BLOCK = 256
MOBA_TOP_K = 3
ROPE_THETA = 10000.0
EPS = 1e-6
NEG = -1e30

LANES = 128
SUBLANES = 8
MXU_TILE = 256
BF16_TILE_ROWS = 16
VT_ROWS = 80
HALO = 16
N_SLOTS = 16
FFN_TM = 512
FFN_TF = 256
INPROJ_TILES = 4
INPROJ_MM_ROWS = 256
ATTN_GROUP = 2
ATTN_HEADS = 4
ATTN_SCORE_BUFS = 4
VMEM_LIMIT = 60 * 1024 * 1024

_HALF = HEAD_DIM // 2
_GROUP_HEAD = (1, 3, 0, 2, 5, 7, 4, 6)
_HEAD_GROUP = tuple(_GROUP_HEAD.index(h) for h in range(N_HEADS))


def _qk_layout():
    c = np.arange(ATTN_WIDTH)
    lane = c % LANES
    head = 2 * (c // LANES) + (lane // _HALF) % 2
    dim = lane % _HALF + _HALF * (lane // HEAD_DIM)
    return head, dim


def _rms(x, g):
    ms = jnp.mean(x * x, axis=-1, keepdims=True)
    return x * lax.rsqrt(ms + EPS) * g


def _sigmoid(x):
    return 0.5 * jnp.tanh(0.5 * x) + 0.5


def _lane_iota(shape):
    return lax.broadcasted_iota(jnp.int32, shape, len(shape) - 1)


def _layer_spec(l, shape, **kwargs):
    return pl.BlockSpec((None,) + tuple(shape), lambda *_: (l,) + (0,) * len(shape), **kwargs)


def _dot_cols(lhs, w_ref, start, stop):
    return jnp.concatenate(
        [jnp.dot(lhs, w_ref[:, c:c + MXU_TILE], preferred_element_type=F32)
         for c in range(start, stop, MXU_TILE)], axis=-1)


def _split_bf16(x):
    hi = x.astype(BF16)
    return hi, (x - hi.astype(F32)).astype(BF16)


def _swiglu_residual(x, g_ref, wgu_ref, wd_ref, o_ref):
    h = _rms(x, g_ref[...]).astype(BF16)
    acc = jnp.zeros(x.shape, F32)
    for c in range(D_FF // FFN_TF):
        a = jnp.dot(h, wgu_ref[:, c * FFN_TF:(c + 1) * FFN_TF], preferred_element_type=F32)
        b = jnp.dot(h, wgu_ref[:, D_FF + c * FFN_TF:D_FF + (c + 1) * FFN_TF],
                    preferred_element_type=F32)
        act = (a * jax.nn.sigmoid(a) * b).astype(BF16)
        acc = acc + jnp.dot(act, wd_ref[c * FFN_TF:(c + 1) * FFN_TF, :],
                            preferred_element_type=F32)
    o_ref[...] = x + 0.5 * acc


def _run_cast_jobs(src_refs, dst_refs):
    for src, dst in zip(src_refs, dst_refs):
        dst[...] = src[...].astype(dst.dtype)


def _ffn_kernel(x_ref, g_ref, wgu_ref, wd_ref, *rest):
    n_jobs = (len(rest) - 1) // 2
    o_ref = rest[n_jobs]
    _swiglu_residual(x_ref[...], g_ref, wgu_ref, wd_ref, o_ref)
    _run_cast_jobs(rest[:n_jobs], rest[n_jobs + 1:])


def _merge_ffn_kernel(x_ref, p_ref, g1_ref, ya_ref, wba_ref, wo_ref, g_ref, wgu_ref, wd_ref,
                      *rest):
    n_jobs = (len(rest) - 1) // 2
    o_ref = rest[n_jobs]
    merged = p_ref[...] + g1_ref[...] * jnp.dot(ya_ref[...], wba_ref[...],
                                                preferred_element_type=F32)
    x = x_ref[...] + jnp.dot(merged.astype(BF16), wo_ref[...], preferred_element_type=F32)
    _swiglu_residual(x, g_ref, wgu_ref, wd_ref, o_ref)
    _run_cast_jobs(rest[:n_jobs], rest[n_jobs + 1:])


def _cast_job_specs(jobs, n_steps):
    in_specs, out_specs, out_shapes = [], [], []
    for w, l in jobs:
        _, r, c = w.shape
        n_blocks = max(nb for nb in range(1, n_steps + 1)
                       if n_steps % nb == 0 and r % (nb * BF16_TILE_ROWS) == 0)
        rows = r // n_blocks
        in_specs.append(pl.BlockSpec(
            (None, rows, c), lambda i, l=l, n_blocks=n_blocks: (l, (i * n_blocks) // n_steps, 0)))
        out_specs.append(pl.BlockSpec(
            (rows, c), lambda i, n_blocks=n_blocks: ((i * n_blocks) // n_steps, 0)))
        out_shapes.append(jax.ShapeDtypeStruct((r, c), BF16))
    return in_specs, out_specs, out_shapes


def _resident(w):
    return pl.BlockSpec(w.shape, lambda *_: (0, 0), pipeline_mode=pl.Buffered(1))


def _merge_ffn(l, x2d, p2d, g12d, ya2d, wba, wo, g, wgu, wd, cast_jobs=()):
    t = x2d.shape[0]
    n_steps = t // FFN_TM
    row = pl.BlockSpec((FFN_TM, D_MODEL), lambda i: (i, 0))
    job_in, job_out, job_shapes = _cast_job_specs(cast_jobs, n_steps)
    outs = pl.pallas_call(
        _merge_ffn_kernel,
        grid=(n_steps,),
        in_specs=[
            row, row, row,
            pl.BlockSpec((FFN_TM, ATTN_WIDTH), lambda i: (i, 0)),
            _resident(wba), _resident(wo),
            _layer_spec(l, (1, D_MODEL)),
            _resident(wgu), _resident(wd),
        ] + job_in,
        out_specs=[row] + job_out,
        out_shape=[jax.ShapeDtypeStruct((t, D_MODEL), F32)] + job_shapes,
        compiler_params=pltpu.CompilerParams(
            dimension_semantics=("arbitrary",), vmem_limit_bytes=VMEM_LIMIT),
        name="merge_ffn",
    )(x2d, p2d, g12d, ya2d, wba, wo, g, wgu, wd, *[w for w, _ in cast_jobs])
    return outs[0], outs[1:]


def _ffn(l, x2d, g, wgu, wd, cast_jobs=()):
    t = x2d.shape[0]
    n_steps = t // FFN_TM
    row = pl.BlockSpec((FFN_TM, D_MODEL), lambda i: (i, 0))
    job_in, job_out, job_shapes = _cast_job_specs(cast_jobs, n_steps)
    outs = pl.pallas_call(
        _ffn_kernel,
        grid=(n_steps,),
        in_specs=[row, _layer_spec(l, (1, D_MODEL)), _resident(wgu), _resident(wd)] + job_in,
        out_specs=[row] + job_out,
        out_shape=[jax.ShapeDtypeStruct((t, D_MODEL), F32)] + job_shapes,
        compiler_params=pltpu.CompilerParams(
            dimension_semantics=("arbitrary",), vmem_limit_bytes=VMEM_LIMIT),
        name="ffn",
    )(x2d, g, wgu, wd, *[w for w, _ in cast_jobs])
    return outs[0], outs[1:]


def _head_rms_rope(x, gain, cos, sin_signed):
    even = (_lane_iota((x.shape[0], LANES)) // _HALF) % 2 == 0
    outs = []
    for p in range(ATTN_WIDTH // LANES):
        lanes = slice(p * LANES, (p + 1) * LANES)
        xp = x[:, lanes]
        sq = xp * xp
        ss_even = jnp.sum(jnp.where(even, sq, 0.0), axis=-1, keepdims=True)
        ss_odd = jnp.sum(jnp.where(even, 0.0, sq), axis=-1, keepdims=True)
        ms = jnp.where(even, ss_even, ss_odd) * (1.0 / HEAD_DIM)
        yp = xp * lax.rsqrt(ms + EPS) * gain[:, lanes]
        outs.append(yp * cos + pltpu.roll(yp, HEAD_DIM, 1) * sin_signed)
    return outs


def _window_sums(halo, u, steps):
    s = jnp.concatenate([halo, u], axis=0)
    for t in range(steps):
        s = s + pltpu.roll(s, 1 << t, 0)
    return s[HALO:, :]


def _choose_blocks(gate_t, i):
    nq = gate_t.shape[1]
    g3 = gate_t.reshape(N_HEADS, N_SLOTS, nq)
    slot = lax.broadcasted_iota(jnp.int32, g3.shape, 1)
    gm = jnp.where(slot < i, g3, NEG)
    rank = jnp.zeros(g3.shape, jnp.int32)
    for j in range(N_SLOTS):
        other = jnp.broadcast_to(gm[:, j:j + 1, :], g3.shape)
        ahead = (other > gm) | ((other == gm) & (slot > j))
        rank = rank + jnp.where(ahead, 1, 0)
    chosen = (rank < MOBA_TOP_K) & (slot < i)
    bias = jnp.where(chosen | (slot == i), 0.0, NEG)
    return bias.reshape(N_HEADS * N_SLOTS, nq)


def _inproj_rows(row0, x, mixg, win_ref, bg, poolw_ref, pools, qg, kg, cos, sin_signed, wbp_ref,
                 halo_ref):
    tm = x.shape[0]
    h = _rms(x, mixg).astype(BF16)

    u = _dot_cols(h, win_ref, 0, POOL_WIDTH)
    pos = row0 + lax.broadcasted_iota(jnp.int32, (tm, 1), 0)
    ds = []
    for g, w in enumerate(POOL_WINDOWS):
        lanes = slice(g * POOL_GROUP, (g + 1) * POOL_GROUP)
        wsum = _window_sums(halo_ref[:, lanes], u[:, lanes], g + 1)
        cnt = jnp.minimum(pos + 1, w).astype(F32)
        ds.append((wsum / cnt - u[:, lanes]).astype(BF16))
    per_tile = MXU_TILE // POOL_GROUP
    ys = [jnp.dot(jnp.concatenate(ds[t * per_tile:(t + 1) * per_tile], axis=-1), poolw_ref[t],
                  preferred_element_type=F32) for t in range(len(ds) // per_tile)]
    y_pool = (jnp.concatenate(ys, axis=-1) * pools).astype(BF16)
    halo_ref[...] = u[tm - HALO:, :]

    gl = _dot_cols(h, win_ref, POOL_WIDTH + 3 * ATTN_WIDTH, win_ref.shape[1])
    gates = _sigmoid(gl + bg)
    p_out = gates[:, :D_MODEL] * _dot_cols(y_pool, wbp_ref, 0, D_MODEL)
    g1_out = gates[:, D_MODEL:]

    o1 = POOL_WIDTH
    q = _dot_cols(h, win_ref, o1, o1 + ATTN_WIDTH)
    k = _dot_cols(h, win_ref, o1 + ATTN_WIDTH, o1 + 2 * ATTN_WIDTH)
    v = _dot_cols(h, win_ref, o1 + 2 * ATTN_WIDTH, o1 + 3 * ATTN_WIDTH)
    q_tiles = _head_rms_rope(q, qg, cos, sin_signed)
    k_tiles = _head_rms_rope(k, kg, cos, sin_signed)
    return p_out, g1_out, q_tiles, k_tiles, v


def _inproj_block(i, q_tiles, k_tiles, v, hmask_ref, kbt_ref):
    tm = v.shape[0]
    qn = jnp.concatenate(q_tiles, axis=-1)
    kn = jnp.concatenate(k_tiles, axis=-1)

    nt_dims = (((1,), (1,)), ((), ()))
    q_hi, q_lo = _split_bf16(qn)
    kb_hi, kb_lo = _split_bf16(kbt_ref[...])
    gate_t = (lax.dot_general(kb_hi, q_hi, nt_dims, preferred_element_type=F32)
              + lax.dot_general(kb_hi, q_lo, nt_dims, preferred_element_type=F32)
              + lax.dot_general(kb_lo, q_hi, nt_dims, preferred_element_type=F32))
    bias = _choose_blocks(gate_t, i).T

    kbar = jnp.sum(kn, axis=0, keepdims=True) * (1.0 / MOBA_BLOCK)
    row = lax.broadcasted_iota(jnp.int32, kbt_ref.shape, 0)
    mine = (row % N_SLOTS == i) & (hmask_ref[...] > 0.0)
    kbt_ref[...] = jnp.where(mine, kbar, kbt_ref[...])

    lane = _lane_iota((tm, LANES))
    ones_rows = jnp.where(lax.broadcasted_iota(jnp.int32, (VT_ROWS - HEAD_DIM, tm), 0) == 0,
                          1.0, 0.0)
    qa, ka, vat = [], [], []
    for hd in range(N_HEADS):
        p, odd, grp = hd // 2, hd % 2, _HEAD_GROUP[hd]
        own = (lane // _HALF) % 2 == odd
        vp = v[:, p * LANES:(p + 1) * LANES]
        qa.append(jnp.where(own, q_tiles[p] * (HEAD_DIM ** -0.5),
                            jnp.where(lane // N_SLOTS == grp, bias, 0.0)).astype(BF16))
        ka.append(jnp.where(own, k_tiles[p],
                            jnp.where(lane == grp * N_SLOTS + i, 1.0, 0.0)).astype(BF16))
        vt = vp.T[odd * HEAD_DIM:(odd + 1) * HEAD_DIM, :]
        vat.append(jnp.concatenate([vt, ones_rows], axis=0).astype(BF16))
    return qa, ka, vat


def _inproj_kernel(x_ref, mixg_ref, win_ref, bg_ref, poolw_ref, pools_ref, qg_ref, kg_ref,
                   cos_ref, sin_ref, wbp_ref, hmask_ref,
                   qa_ref, ka_ref, vat_ref, p_ref, g1_ref,
                   halo_ref, kbt_ref):
    step = pl.program_id(1)
    step_rows = x_ref.shape[1]

    @pl.when(step == 0)
    def _():
        halo_ref[...] = jnp.zeros(halo_ref.shape, F32)
        kbt_ref[...] = jnp.zeros(kbt_ref.shape, F32)

    for r0 in range(0, step_rows, INPROJ_MM_ROWS):
        rows = slice(r0, r0 + INPROJ_MM_ROWS)
        p_out, g1_out, q_tiles, k_tiles, v = _inproj_rows(
            step * step_rows + r0, x_ref[0, rows, :], mixg_ref[...], win_ref, bg_ref[...],
            poolw_ref, pools_ref[...], qg_ref[...], kg_ref[...], cos_ref[rows, :],
            sin_ref[rows, :], wbp_ref, halo_ref)
        p_ref[0, rows, :] = p_out
        g1_ref[0, rows, :] = g1_out
        for t0 in range(0, INPROJ_MM_ROWS, MOBA_BLOCK):
            sub = slice(t0, t0 + MOBA_BLOCK)
            blk = slice(r0 + t0, r0 + t0 + MOBA_BLOCK)
            qa, ka, vat = _inproj_block(
                (step * step_rows + r0 + t0) // MOBA_BLOCK, [q[sub] for q in q_tiles],
                [k[sub] for k in k_tiles], v[sub], hmask_ref, kbt_ref)
            for hd in range(N_HEADS):
                qa_ref[0, hd, blk, :] = qa[hd]
                ka_ref[0, hd, blk, :] = ka[hd]
                vat_ref[0, hd, :, blk] = vat[hd]


def _inproj(l, x, mixg, win, bg, poolw, pools, qg, kg, cos, sin_signed, wbp, hmask):
    b, s, _ = x.shape
    tm = INPROJ_TILES * MOBA_BLOCK
    const = lambda *shape: pl.BlockSpec(shape, lambda bi, i: (0,) * len(shape))
    head_spec = pl.BlockSpec((1, N_HEADS, tm, LANES), lambda bi, i: (bi, 0, i, 0))
    headt_spec = pl.BlockSpec((1, N_HEADS, VT_ROWS, tm), lambda bi, i: (bi, 0, 0, i))
    headt_shape = jax.ShapeDtypeStruct((b, N_HEADS, VT_ROWS, s), BF16)
    row_spec = pl.BlockSpec((1, tm, D_MODEL), lambda bi, i: (bi, i, 0))
    head_shape = jax.ShapeDtypeStruct((b, N_HEADS, s, LANES), BF16)
    row_shape = jax.ShapeDtypeStruct((b, s, D_MODEL), F32)
    return pl.pallas_call(
        _inproj_kernel,
        grid=(b, s // tm),
        in_specs=[
            row_spec,
            _layer_spec(l, (1, D_MODEL)),
            _layer_spec(l, win.shape[1:], pipeline_mode=pl.Buffered(1)),
            _layer_spec(l, (1, 2 * D_MODEL)),
            _layer_spec(l, poolw.shape[1:]),
            _layer_spec(l, (1, POOL_WIDTH)),
            _layer_spec(l, (1, ATTN_WIDTH)),
            _layer_spec(l, (1, ATTN_WIDTH)),
            pl.BlockSpec((tm, LANES), lambda bi, i: (i, 0)),
            pl.BlockSpec((tm, LANES), lambda bi, i: (i, 0)),
            _resident(wbp),
            const(N_HEADS * N_SLOTS, ATTN_WIDTH),
        ],
        out_specs=[head_spec, head_spec, headt_spec, row_spec, row_spec],
        out_shape=[head_shape, head_shape, headt_shape, row_shape, row_shape],
        scratch_shapes=[
            pltpu.VMEM((HALO, POOL_WIDTH), F32),
            pltpu.VMEM((N_HEADS * N_SLOTS, ATTN_WIDTH), F32),
        ],
        compiler_params=pltpu.CompilerParams(
            dimension_semantics=("arbitrary", "arbitrary"), vmem_limit_bytes=VMEM_LIMIT),
        name="inproj",
    )(x, mixg, win, bg, poolw, pools, qg, kg, cos, sin_signed, wbp, hmask)


def _attn_tile_group(i, nk, q_ref, k_ref, vt_ref, o_ref, s_refs):
    tq = MOBA_BLOCK
    q_rows = pl.ds(pl.multiple_of(i * tq, tq), tq)
    nt_dims = (((1,), (1,)), ((), ()))
    rel = (lax.broadcasted_iota(jnp.int32, (tq, tq), 0)
           - lax.broadcasted_iota(jnp.int32, (tq, tq), 1))

    def score_tile(hh, q, j):
        s = lax.dot_general(k_ref[0, hh, j * tq:(j + 1) * tq, :], q, nt_dims,
                            preferred_element_type=F32)
        if j >= nk - ATTN_GROUP:
            s = jnp.where(rel <= (i - j) * tq, s, NEG)
        s_refs[hh % len(s_refs)][j] = s
        return jnp.max(s.reshape(tq // SUBLANES, SUBLANES, tq), axis=0)

    def fold(mrun, tile_max):
        return tile_max if mrun is None else jnp.maximum(mrun, tile_max)

    outs = []
    q = q_ref[0, 0, q_rows, :]
    mrun_next = None
    for j in range(nk):
        mrun_next = fold(mrun_next, score_tile(0, q, j))
    for hh in range(ATTN_HEADS):
        m = jnp.broadcast_to(jnp.max(mrun_next, axis=0, keepdims=True), (tq, tq))
        s_ref = s_refs[hh % len(s_refs)]
        if hh + 1 < ATTN_HEADS:
            q = q_ref[0, hh + 1, q_rows, :]
        mrun_next, pts = None, []
        for j in range(nk):
            if hh + 1 < ATTN_HEADS:
                mrun_next = fold(mrun_next, score_tile(hh + 1, q, j))
            pts.append(jnp.exp(s_ref[j] - m).astype(BF16))
        acc = jnp.dot(vt_ref[0, hh, :, 0:nk * tq], jnp.concatenate(pts, axis=0),
                      preferred_element_type=F32)
        outs.append(acc[:HEAD_DIM, :] / acc[HEAD_DIM:HEAD_DIM + 1, :])
        if hh % 2:
            pair = jnp.concatenate([outs[hh - 1], outs[hh]], axis=0)
            o_ref[0, q_rows, (hh // 2) * LANES:(hh // 2 + 1) * LANES] = pair.T.astype(o_ref.dtype)


def _attn_kernel(q_ref, k_ref, vt_ref, o_ref, *s_refs):
    for c in range(N_SLOTS // ATTN_GROUP):
        def body(i, carry, nk=ATTN_GROUP * (c + 1)):
            _attn_tile_group(i, nk, q_ref, k_ref, vt_ref, o_ref, s_refs)
            return carry
        lax.fori_loop(ATTN_GROUP * c, ATTN_GROUP * (c + 1), body, 0)


def _attn(qa, ka, vat):
    b, nh, s, _ = qa.shape
    tq = MOBA_BLOCK
    qk_spec = pl.BlockSpec((1, ATTN_HEADS, s, LANES), lambda bi, p: (bi, p, 0, 0))
    vt_spec = pl.BlockSpec((1, ATTN_HEADS, VT_ROWS, s), lambda bi, p: (bi, p, 0, 0))
    return pl.pallas_call(
        _attn_kernel,
        grid=(b, nh // ATTN_HEADS),
        in_specs=[qk_spec, qk_spec, vt_spec],
        out_specs=pl.BlockSpec((1, s, HEAD_DIM * ATTN_HEADS), lambda bi, p: (bi, 0, p)),
        out_shape=jax.ShapeDtypeStruct((b, s, ATTN_WIDTH), BF16),
        scratch_shapes=[pltpu.VMEM((s // tq, tq, tq), F32)] * ATTN_SCORE_BUFS,
        compiler_params=pltpu.CompilerParams(
            dimension_semantics=("arbitrary", "arbitrary"),
            vmem_limit_bytes=VMEM_LIMIT),
        name="attn",
    )(qa, ka, vat)


def _rope_tables(s):
    inv_freq = 1.0 / ROPE_THETA ** (np.arange(_HALF, dtype=np.float64) * (2.0 / HEAD_DIM))
    ang = np.arange(s, dtype=np.float64)[:, None] * inv_freq[None, :]
    cos, sin = np.cos(ang), np.sin(ang)
    cos = np.tile(cos, (1, LANES // _HALF))
    sin_signed = np.concatenate([-sin, -sin, sin, sin], axis=-1)
    return jnp.asarray(cos, F32), jnp.asarray(sin_signed, F32)


def _permute_heads(w):
    lead = w.shape[:-1]
    w = w.reshape(lead + (N_HEADS // 2, 2, 2, _HALF))
    return jnp.swapaxes(w, -3, -2).reshape(lead + (ATTN_WIDTH,))


def kernel(x, ffn1_norm, ffn1_w_gate_up, ffn1_w_down, mix_norm, w_in, b_gate, pool_w, pool_scale,
           q_norm, k_norm, w_branch_pool, w_branch_attn, w_out, ffn2_norm, ffn2_w_gate_up,
           ffn2_w_down):
    b, s, d = x.shape
    assert s == N_SLOTS * MOBA_BLOCK and d == D_MODEL
    depth = ffn1_norm.shape[0]
    cos, sin_signed = _rope_tables(s)
    head, dim = _qk_layout()
    hmask = jnp.asarray(np.repeat(np.asarray(_GROUP_HEAD), N_SLOTS)[:, None] == head[None, :], F32)
    o1, o2, o3 = POOL_WIDTH, POOL_WIDTH + ATTN_WIDTH, POOL_WIDTH + 2 * ATTN_WIDTH

    row = lambda p: p[:, None, :]
    win = jnp.concatenate([w_in[..., :o1], _permute_heads(w_in[..., o1:o2]),
                           _permute_heads(w_in[..., o2:o3]), w_in[..., o3:]], axis=-1).astype(BF16)
    qg = row(q_norm[:, dim])
    kg = row(k_norm[:, dim])
    per_tile = MXU_TILE // POOL_GROUP
    n_tiles = len(POOL_WINDOWS) // per_tile
    poolw = jnp.einsum('ltaij,ab->ltaibj',
                       pool_w.reshape(depth, n_tiles, per_tile, POOL_GROUP, POOL_GROUP),
                       jnp.eye(per_tile, dtype=F32)
                       ).reshape(depth, n_tiles, MXU_TILE, MXU_TILE).astype(BF16)

    wgu, wd = ffn1_w_gate_up[0].astype(BF16), ffn1_w_down[0].astype(BF16)
    x2d = x.reshape(b * s, d)
    for l in range(depth):
        mixer_jobs = [(ffn2_w_gate_up, l), (ffn2_w_down, l), (w_branch_pool, l),
                      (w_branch_attn, l), (w_out, l)]
        x2d, (wgu2, wd2, wbp, wba, wo) = _ffn(l, x2d, row(ffn1_norm), wgu, wd, mixer_jobs)
        qa, ka, vat, p, g1 = _inproj(
            l, x2d.reshape(b, s, d), row(mix_norm), win, row(b_gate), poolw, row(pool_scale),
            qg, kg, cos, sin_signed, wbp, hmask)
        ya = _attn(qa, ka, vat)
        next_jobs = [(ffn1_w_gate_up, l + 1), (ffn1_w_down, l + 1)] if l + 1 < depth else []
        x2d, nxt = _merge_ffn(l, x2d, p.reshape(b * s, d), g1.reshape(b * s, d),
                              ya.reshape(b * s, ATTN_WIDTH), wba, wo, row(ffn2_norm), wgu2, wd2,
                              next_jobs)
        if nxt:
            wgu, wd = nxt
    return x2d.reshape(b, s, d)
```

```python
import numpy as np

import jax
import jax.numpy as jnp
from jax import lax
from jax.experimental import pallas as pl
from jax.experimental.pallas import tpu as pltpu

F32 = jnp.float32
BF16 = jnp.bfloat16

D_MODEL = 1024
D_FF = 2816
POOL_WINDOWS = (2, 4, 8, 16)
POOL_WIDTH = 512
POOL_GROUP = 128
N_HEADS = 8
HEAD_DIM = 64
ATTN_WIDTH = 512
MOBA_BLOCK = 256
MOBA_TOP_K = 3
ROPE_THETA = 10000.0
EPS = 1e-6
NEG = -1e30

LANES = 128
SUBLANES = 8
MXU_TILE = 256
BF16_TILE_ROWS = 16
VT_ROWS = 80
HALO = 16
N_SLOTS = 16
FFN_TM = 512
FFN_TF = 256
INPROJ_TILES = 4
INPROJ_MM_ROWS = 256
ATTN_GROUP = 2
ATTN_HEADS = 4
VMEM_LIMIT = 60 * 1024 * 1024

_HALF = HEAD_DIM // 2
_GROUP_HEAD = (1, 3, 0, 2, 5, 7, 4, 6)
_HEAD_GROUP = tuple(_GROUP_HEAD.index(h) for h in range(N_HEADS))


def _qk_layout():
    c = np.arange(ATTN_WIDTH)
    lane = c % LANES
    head = 2 * (c // LANES) + (lane // _HALF) % 2
    dim = lane % _HALF + _HALF * (lane // HEAD_DIM)
    return head, dim


def _rms(x, g):
    ms = jnp.mean(x * x, axis=-1, keepdims=True)
    return x * lax.rsqrt(ms + EPS) * g


def _sigmoid_of_twice(half_x):
    return 0.5 * jnp.tanh(half_x) + 0.5


def _lane_iota(shape):
    return lax.broadcasted_iota(jnp.int32, shape, len(shape) - 1)


def _layer_spec(l, shape, **kwargs):
    return pl.BlockSpec((None,) + tuple(shape), lambda *_: (l,) + (0,) * len(shape), **kwargs)


def _dot_cols(lhs, w_ref, start, stop):
    return jnp.concatenate(
        [jnp.dot(lhs, w_ref[:, c:c + MXU_TILE], preferred_element_type=F32)
         for c in range(start, stop, MXU_TILE)], axis=-1)


def _split_bf16(x):
    hi = x.astype(BF16)
    return hi, (x - hi.astype(F32)).astype(BF16)


def _swiglu_residual(x, g_ref, wgu_ref, wd_ref, o_ref):
    h = _rms(x, g_ref[...]).astype(BF16)
    acc = jnp.zeros(x.shape, F32)
    for c in range(D_FF // FFN_TF):
        a = jnp.dot(h, wgu_ref[:, c * FFN_TF:(c + 1) * FFN_TF], preferred_element_type=F32)
        b = jnp.dot(h, wgu_ref[:, D_FF + c * FFN_TF:D_FF + (c + 1) * FFN_TF],
                    preferred_element_type=F32)
        act = (a * jax.nn.sigmoid(a) * b).astype(BF16)
        acc = acc + jnp.dot(act, wd_ref[c * FFN_TF:(c + 1) * FFN_TF, :],
                            preferred_element_type=F32)
    o_ref[...] = x + 0.5 * acc


def _run_cast_jobs(src_refs, dst_refs):
    for src, dst in zip(src_refs, dst_refs):
        dst[...] = src[...].astype(dst.dtype)


def _ffn_kernel(x_ref, g_ref, wgu_ref, wd_ref, *rest):
    n_jobs = (len(rest) - 1) // 2
    o_ref = rest[n_jobs]
    _swiglu_residual(x_ref[...], g_ref, wgu_ref, wd_ref, o_ref)
    _run_cast_jobs(rest[:n_jobs], rest[n_jobs + 1:])


def _merge_ffn_kernel(x_ref, p_ref, g1_ref, ya_ref, wba_ref, wo_ref, g_ref, wgu_ref, wd_ref,
                      *rest):
    n_jobs = (len(rest) - 1) // 2
    o_ref = rest[n_jobs]
    merged = p_ref[...] + g1_ref[...] * jnp.dot(ya_ref[...], wba_ref[...],
                                                preferred_element_type=F32)
    x = x_ref[...] + jnp.dot(merged.astype(BF16), wo_ref[...], preferred_element_type=F32)
    _swiglu_residual(x, g_ref, wgu_ref, wd_ref, o_ref)
    _run_cast_jobs(rest[:n_jobs], rest[n_jobs + 1:])


def _cast_job_specs(jobs, n_steps):
    in_specs, out_specs, out_shapes = [], [], []
    for w, l in jobs:
        _, r, c = w.shape
        n_blocks = max(nb for nb in range(1, n_steps + 1)
                       if n_steps % nb == 0 and r % (nb * BF16_TILE_ROWS) == 0)
        rows = r // n_blocks
        in_specs.append(pl.BlockSpec(
            (None, rows, c), lambda i, l=l, n_blocks=n_blocks: (l, (i * n_blocks) // n_steps, 0)))
        out_specs.append(pl.BlockSpec(
            (rows, c), lambda i, n_blocks=n_blocks: ((i * n_blocks) // n_steps, 0)))
        out_shapes.append(jax.ShapeDtypeStruct((r, c), BF16))
    return in_specs, out_specs, out_shapes


def _resident(w):
    return pl.BlockSpec(w.shape, lambda *_: (0, 0), pipeline_mode=pl.Buffered(1))


def _merge_ffn(l, x2d, p2d, g12d, ya2d, wba, wo, g, wgu, wd, cast_jobs=()):
    t = x2d.shape[0]
    n_steps = t // FFN_TM
    row = pl.BlockSpec((FFN_TM, D_MODEL), lambda i: (i, 0))
    job_in, job_out, job_shapes = _cast_job_specs(cast_jobs, n_steps)
    outs = pl.pallas_call(
        _merge_ffn_kernel,
        grid=(n_steps,),
        in_specs=[
            row, row, row,
            pl.BlockSpec((FFN_TM, ATTN_WIDTH), lambda i: (i, 0)),
            _resident(wba), _resident(wo),
            _layer_spec(l, (1, D_MODEL)),
            _resident(wgu), _resident(wd),
        ] + job_in,
        out_specs=[row] + job_out,
        out_shape=[jax.ShapeDtypeStruct((t, D_MODEL), F32)] + job_shapes,
        compiler_params=pltpu.CompilerParams(
            dimension_semantics=("arbitrary",), vmem_limit_bytes=VMEM_LIMIT),
        name="merge_ffn",
    )(x2d, p2d, g12d, ya2d, wba, wo, g, wgu, wd, *[w for w, _ in cast_jobs])
    return outs[0], outs[1:]


def _ffn(l, x2d, g, wgu, wd, cast_jobs=()):
    t = x2d.shape[0]
    n_steps = t // FFN_TM
    row = pl.BlockSpec((FFN_TM, D_MODEL), lambda i: (i, 0))
    job_in, job_out, job_shapes = _cast_job_specs(cast_jobs, n_steps)
    outs = pl.pallas_call(
        _ffn_kernel,
        grid=(n_steps,),
        in_specs=[row, _layer_spec(l, (1, D_MODEL)), _resident(wgu), _resident(wd)] + job_in,
        out_specs=[row] + job_out,
        out_shape=[jax.ShapeDtypeStruct((t, D_MODEL), F32)] + job_shapes,
        compiler_params=pltpu.CompilerParams(
            dimension_semantics=("arbitrary",), vmem_limit_bytes=VMEM_LIMIT),
        name="ffn",
    )(x2d, g, wgu, wd, *[w for w, _ in cast_jobs])
    return outs[0], outs[1:]


def _head_rms_rope(x, gain, cos, sin_signed):
    even = (_lane_iota((x.shape[0], LANES)) // _HALF) % 2 == 0
    outs = []
    for p in range(ATTN_WIDTH // LANES):
        lanes = slice(p * LANES, (p + 1) * LANES)
        xp = x[:, lanes]
        sq = xp * xp
        ss_even = jnp.sum(jnp.where(even, sq, 0.0), axis=-1, keepdims=True)
        ss_odd = jnp.sum(jnp.where(even, 0.0, sq), axis=-1, keepdims=True)
        ms = jnp.where(even, ss_even, ss_odd) * (1.0 / HEAD_DIM)
        yp = xp * lax.rsqrt(ms + EPS) * gain[:, lanes]
        outs.append(yp * cos + pltpu.roll(yp, HEAD_DIM, 1) * sin_signed)
    return outs


def _window_sums(halo, u, steps):
    s = jnp.concatenate([halo, u], axis=0)
    for t in range(steps):
        s = s + pltpu.roll(s, 1 << t, 0)
    return s[HALO:, :]


def _choose_blocks(gate_t, i):
    nq = gate_t.shape[1]
    g3 = gate_t.reshape(N_HEADS, N_SLOTS, nq)
    slot = lax.broadcasted_iota(jnp.int32, g3.shape, 1)
    gm = jnp.where(slot < i, g3, NEG)
    rank = jnp.zeros(g3.shape, jnp.int32)
    for j in range(N_SLOTS):
        other = jnp.broadcast_to(gm[:, j:j + 1, :], g3.shape)
        ahead = (other > gm) | ((other == gm) & (slot > j))
        rank = rank + jnp.where(ahead, 1, 0)
    chosen = (rank < MOBA_TOP_K) & (slot < i)
    bias = jnp.where(chosen | (slot == i), 0.0, NEG)
    return bias.reshape(N_HEADS * N_SLOTS, nq)


def _inproj_rows(row0, x, mixg, win_ref, bg, poolw_ref, pools, qg, kg, cos, sin_signed, wbp_ref,
                 halo_ref):
    tm = x.shape[0]
    h = _rms(x, mixg).astype(BF16)

    u = _dot_cols(h, win_ref, 0, POOL_WIDTH)
    pos = row0 + lax.broadcasted_iota(jnp.int32, (tm, 1), 0)
    ds = []
    for g, w in enumerate(POOL_WINDOWS):
        lanes = slice(g * POOL_GROUP, (g + 1) * POOL_GROUP)
        wsum = _window_sums(halo_ref[:, lanes], u[:, lanes], g + 1)
        cnt = jnp.minimum(pos + 1, w).astype(F32)
        ds.append((wsum / cnt - u[:, lanes]).astype(BF16))
    per_tile = MXU_TILE // POOL_GROUP
    ys = [jnp.dot(jnp.concatenate(ds[t * per_tile:(t + 1) * per_tile], axis=-1), poolw_ref[t],
                  preferred_element_type=F32) for t in range(len(ds) // per_tile)]
    y_pool = (jnp.concatenate(ys, axis=-1) * pools).astype(BF16)
    halo_ref[...] = u[tm - HALO:, :]

    gl = _dot_cols(h, win_ref, POOL_WIDTH + 3 * ATTN_WIDTH, win_ref.shape[1])
    gates = _sigmoid_of_twice(gl + bg)
    p_out = gates[:, :D_MODEL] * _dot_cols(y_pool, wbp_ref, 0, D_MODEL)
    g1_out = gates[:, D_MODEL:]

    o1 = POOL_WIDTH
    q = _dot_cols(h, win_ref, o1, o1 + ATTN_WIDTH)
    k = _dot_cols(h, win_ref, o1 + ATTN_WIDTH, o1 + 2 * ATTN_WIDTH)
    v = _dot_cols(h, win_ref, o1 + 2 * ATTN_WIDTH, o1 + 3 * ATTN_WIDTH)
    q_tiles = _head_rms_rope(q, qg, cos, sin_signed)
    k_tiles = _head_rms_rope(k, kg, cos, sin_signed)
    return p_out, g1_out, q_tiles, k_tiles, v


def _inproj_block(i, q_tiles, k_tiles, v, hmask_ref, kbt_ref):
    tm = v.shape[0]
    qn = jnp.concatenate(q_tiles, axis=-1)
    kn = jnp.concatenate(k_tiles, axis=-1)

    nt_dims = (((1,), (1,)), ((), ()))
    q_hi, q_lo = _split_bf16(qn)
    kb_hi, kb_lo = _split_bf16(kbt_ref[...])
    gate_t = (lax.dot_general(kb_hi, q_hi, nt_dims, preferred_element_type=F32)
              + lax.dot_general(kb_hi, q_lo, nt_dims, preferred_element_type=F32)
              + lax.dot_general(kb_lo, q_hi, nt_dims, preferred_element_type=F32))
    bias = _choose_blocks(gate_t, i).T

    kbar = jnp.sum(kn, axis=0, keepdims=True) * (1.0 / MOBA_BLOCK)
    row = lax.broadcasted_iota(jnp.int32, kbt_ref.shape, 0)
    mine = (row % N_SLOTS == i) & (hmask_ref[...] > 0.0)
    kbt_ref[...] = jnp.where(mine, kbar, kbt_ref[...])

    lane = _lane_iota((tm, LANES))
    ones_rows = jnp.where(lax.broadcasted_iota(jnp.int32, (VT_ROWS - HEAD_DIM, tm), 0) == 0,
                          1.0, 0.0)
    qa, ka, vat = [], [], []
    for hd in range(N_HEADS):
        p, odd, grp = hd // 2, hd % 2, _HEAD_GROUP[hd]
        own = (lane // _HALF) % 2 == odd
        vp = v[:, p * LANES:(p + 1) * LANES]
        qa.append(jnp.where(own, q_tiles[p] * (HEAD_DIM ** -0.5),
                            jnp.where(lane // N_SLOTS == grp, bias, 0.0)).astype(BF16))
        ka.append(jnp.where(own, k_tiles[p],
                            jnp.where(lane == grp * N_SLOTS + i, 1.0, 0.0)).astype(BF16))
        vt = vp.T[odd * HEAD_DIM:(odd + 1) * HEAD_DIM, :]
        vat.append(jnp.concatenate([vt, ones_rows], axis=0).astype(BF16))
    return qa, ka, vat


def _inproj_kernel(x_ref, mixg_ref, win_ref, bg_ref, poolw_ref, pools_ref, qg_ref, kg_ref,
                   cos_ref, sin_ref, wbp_ref, hmask_ref,
                   qa_ref, ka_ref, vat_ref, p_ref, g1_ref,
                   halo_ref, kbt_ref):
    step = pl.program_id(1)
    step_rows = x_ref.shape[1]

    @pl.when(step == 0)
    def _():
        halo_ref[...] = jnp.zeros(halo_ref.shape, F32)
        kbt_ref[...] = jnp.zeros(kbt_ref.shape, F32)

    for r0 in range(0, step_rows, INPROJ_MM_ROWS):
        rows = slice(r0, r0 + INPROJ_MM_ROWS)
        p_out, g1_out, q_tiles, k_tiles, v = _inproj_rows(
            step * step_rows + r0, x_ref[0, rows, :], mixg_ref[...], win_ref, bg_ref[...],
            poolw_ref, pools_ref[...], qg_ref[...], kg_ref[...], cos_ref[rows, :],
            sin_ref[rows, :], wbp_ref, halo_ref)
        p_ref[0, rows, :] = p_out
        g1_ref[0, rows, :] = g1_out
        for t0 in range(0, INPROJ_MM_ROWS, MOBA_BLOCK):
            sub = slice(t0, t0 + MOBA_BLOCK)
            blk = slice(r0 + t0, r0 + t0 + MOBA_BLOCK)
            qa, ka, vat = _inproj_block(
                (step * step_rows + r0 + t0) // MOBA_BLOCK, [q[sub] for q in q_tiles],
                [k[sub] for k in k_tiles], v[sub], hmask_ref, kbt_ref)
            for hd in range(N_HEADS):
                qa_ref[0, hd, blk, :] = qa[hd]
                ka_ref[0, hd, blk, :] = ka[hd]
                vat_ref[0, hd, :, blk] = vat[hd]


def _inproj(l, x, mixg, win, bg, poolw, pools, qg, kg, cos, sin_signed, wbp, hmask):
    b, s, _ = x.shape
    tm = INPROJ_TILES * MOBA_BLOCK
    const = lambda *shape: pl.BlockSpec(shape, lambda bi, i: (0,) * len(shape))
    head_spec = pl.BlockSpec((1, N_HEADS, tm, LANES), lambda bi, i: (bi, 0, i, 0))
    headt_spec = pl.BlockSpec((1, N_HEADS, VT_ROWS, tm), lambda bi, i: (bi, 0, 0, i))
    headt_shape = jax.ShapeDtypeStruct((b, N_HEADS, VT_ROWS, s), BF16)
    row_spec = pl.BlockSpec((1, tm, D_MODEL), lambda bi, i: (bi, i, 0))
    head_shape = jax.ShapeDtypeStruct((b, N_HEADS, s, LANES), BF16)
    row_shape = jax.ShapeDtypeStruct((b, s, D_MODEL), F32)
    return pl.pallas_call(
        _inproj_kernel,
        grid=(b, s // tm),
        in_specs=[
            row_spec,
            _layer_spec(l, (1, D_MODEL)),
            _layer_spec(l, win.shape[1:], pipeline_mode=pl.Buffered(1)),
            _layer_spec(l, (1, 2 * D_MODEL)),
            _layer_spec(l, poolw.shape[1:]),
            _layer_spec(l, (1, POOL_WIDTH)),
            _layer_spec(l, (1, ATTN_WIDTH)),
            _layer_spec(l, (1, ATTN_WIDTH)),
            pl.BlockSpec((tm, LANES), lambda bi, i: (i, 0)),
            pl.BlockSpec((tm, LANES), lambda bi, i: (i, 0)),
            _resident(wbp),
            const(N_HEADS * N_SLOTS, ATTN_WIDTH),
        ],
        out_specs=[head_spec, head_spec, headt_spec, row_spec, row_spec],
        out_shape=[head_shape, head_shape, headt_shape, row_shape, row_shape],
        scratch_shapes=[
            pltpu.VMEM((HALO, POOL_WIDTH), F32),
            pltpu.VMEM((N_HEADS * N_SLOTS, ATTN_WIDTH), F32),
        ],
        compiler_params=pltpu.CompilerParams(
            dimension_semantics=("arbitrary", "arbitrary"), vmem_limit_bytes=VMEM_LIMIT),
        name="inproj",
    )(x, mixg, win, bg, poolw, pools, qg, kg, cos, sin_signed, wbp, hmask)


def _score_tile(i, hh, j, masked, q, k_ref, s_ref):
    tq = MOBA_BLOCK
    s = lax.dot_general(k_ref[0, hh, j * tq:(j + 1) * tq, :], q, (((1,), (1,)), ((), ())),
                        preferred_element_type=F32)
    if masked:
        rel = (lax.broadcasted_iota(jnp.int32, (tq, tq), 0)
               - lax.broadcasted_iota(jnp.int32, (tq, tq), 1))
        s = jnp.where(rel <= (i - j) * tq, s, NEG)
    s_ref[j] = s
    return jnp.max(s.reshape(tq // SUBLANES, SUBLANES, tq), axis=0)


def _q_block(q_ref, hh, i):
    return q_ref[0, hh, pl.ds(pl.multiple_of(i * MOBA_BLOCK, MOBA_BLOCK), MOBA_BLOCK), :]


def _attn_block(i, nk, mrun, q_ref, k_ref, vt_ref, o_ref, s_refs):
    tq = MOBA_BLOCK
    q_rows = pl.ds(pl.multiple_of(i * tq, tq), tq)
    outs = []
    for hh in range(ATTN_HEADS):
        m = jnp.broadcast_to(jnp.max(mrun, axis=0, keepdims=True), (tq, tq))
        s_ref = s_refs[hh]
        if hh + 1 < ATTN_HEADS:
            nxt_h, nxt_i = hh + 1, i
        else:
            nxt_h, nxt_i = 0, jnp.minimum(i + 1, N_SLOTS - 1)
        q = _q_block(q_ref, nxt_h, nxt_i)
        mrun, pts = None, []
        for j in range(nk):
            tile_max = _score_tile(nxt_i, nxt_h, j, j >= nk - ATTN_GROUP, q, k_ref,
                                   s_refs[nxt_h])
            mrun = tile_max if mrun is None else jnp.maximum(mrun, tile_max)
            pts.append(jnp.exp(s_ref[j] - m).astype(BF16))
        acc = jnp.dot(vt_ref[0, hh, :, 0:nk * tq], jnp.concatenate(pts, axis=0),
                      preferred_element_type=F32)
        outs.append(acc[:HEAD_DIM, :] / acc[HEAD_DIM:HEAD_DIM + 1, :])
        if hh % 2:
            pair = jnp.concatenate([outs[hh - 1], outs[hh]], axis=0)
            o_ref[0, q_rows, (hh // 2) * LANES:(hh // 2 + 1) * LANES] = pair.T.astype(o_ref.dtype)
    return mrun


def _attn_kernel(q_ref, k_ref, vt_ref, o_ref, *s_refs):
    mrun = jnp.full((SUBLANES, MOBA_BLOCK), NEG, F32)
    for c in range(N_SLOTS // ATTN_GROUP):
        nk, first = ATTN_GROUP * (c + 1), ATTN_GROUP * c
        q = _q_block(q_ref, 0, first)
        for j in range(nk - ATTN_GROUP, nk):
            mrun = jnp.maximum(mrun, _score_tile(first, 0, j, True, q, k_ref, s_refs[0]))

        def body(i, mrun, nk=nk):
            return _attn_block(i, nk, mrun, q_ref, k_ref, vt_ref, o_ref, s_refs)
        mrun = lax.fori_loop(first, first + ATTN_GROUP, body, mrun)


def _attn(qa, ka, vat):
    b, nh, s, _ = qa.shape
    tq = MOBA_BLOCK
    qk_spec = pl.BlockSpec((1, ATTN_HEADS, s, LANES), lambda bi, p: (bi, p, 0, 0))
    vt_spec = pl.BlockSpec((1, ATTN_HEADS, VT_ROWS, s), lambda bi, p: (bi, p, 0, 0))
    return pl.pallas_call(
        _attn_kernel,
        grid=(b, nh // ATTN_HEADS),
        in_specs=[qk_spec, qk_spec, vt_spec],
        out_specs=pl.BlockSpec((1, s, HEAD_DIM * ATTN_HEADS), lambda bi, p: (bi, 0, p)),
        out_shape=jax.ShapeDtypeStruct((b, s, ATTN_WIDTH), BF16),
        scratch_shapes=[pltpu.VMEM((s // tq, tq, tq), F32)] * ATTN_HEADS,
        compiler_params=pltpu.CompilerParams(
            dimension_semantics=("arbitrary", "arbitrary"),
            vmem_limit_bytes=VMEM_LIMIT),
        name="attn",
    )(qa, ka, vat)


def _rope_tables(s):
    inv_freq = 1.0 / ROPE_THETA ** (np.arange(_HALF, dtype=np.float64) * (2.0 / HEAD_DIM))
    ang = np.arange(s, dtype=np.float64)[:, None] * inv_freq[None, :]
    cos, sin = np.cos(ang), np.sin(ang)
    cos = np.tile(cos, (1, LANES // _HALF))
    sin_signed = np.concatenate([-sin, -sin, sin, sin], axis=-1)
    return jnp.asarray(cos, F32), jnp.asarray(sin_signed, F32)


def _permute_heads(w):
    lead = w.shape[:-1]
    w = w.reshape(lead + (N_HEADS // 2, 2, 2, _HALF))
    return jnp.swapaxes(w, -3, -2).reshape(lead + (ATTN_WIDTH,))


def kernel(x, ffn1_norm, ffn1_w_gate_up, ffn1_w_down, mix_norm, w_in, b_gate, pool_w, pool_scale,
           q_norm, k_norm, w_branch_pool, w_branch_attn, w_out, ffn2_norm, ffn2_w_gate_up,
           ffn2_w_down):
    b, s, d = x.shape
    assert s == N_SLOTS * MOBA_BLOCK and d == D_MODEL
    depth = ffn1_norm.shape[0]
    cos, sin_signed = _rope_tables(s)
    head, dim = _qk_layout()
    hmask = jnp.asarray(np.repeat(np.asarray(_GROUP_HEAD), N_SLOTS)[:, None] == head[None, :], F32)
    o1, o2, o3 = POOL_WIDTH, POOL_WIDTH + ATTN_WIDTH, POOL_WIDTH + 2 * ATTN_WIDTH
    o4 = o3 + ATTN_WIDTH

    row = lambda p: p[:, None, :]
    win = jnp.concatenate([w_in[..., :o1], _permute_heads(w_in[..., o1:o2]),
                           _permute_heads(w_in[..., o2:o3]), w_in[..., o3:o4],
                           0.5 * w_in[..., o4:]], axis=-1).astype(BF16)
    qg = row(q_norm[:, dim])
    kg = row(k_norm[:, dim])
    per_tile = MXU_TILE // POOL_GROUP
    n_tiles = len(POOL_WINDOWS) // per_tile
    poolw = jnp.einsum('ltaij,ab->ltaibj',
                       pool_w.reshape(depth, n_tiles, per_tile, POOL_GROUP, POOL_GROUP),
                       jnp.eye(per_tile, dtype=F32)
                       ).reshape(depth, n_tiles, MXU_TILE, MXU_TILE).astype(BF16)

    wgu, wd = ffn1_w_gate_up[0].astype(BF16), ffn1_w_down[0].astype(BF16)
    x2d = x.reshape(b * s, d)
    for l in range(depth):
        mixer_jobs = [(ffn2_w_gate_up, l), (ffn2_w_down, l), (w_branch_pool, l),
                      (w_branch_attn, l), (w_out, l)]
        x2d, (wgu2, wd2, wbp, wba, wo) = _ffn(l, x2d, row(ffn1_norm), wgu, wd, mixer_jobs)
        qa, ka, vat, p, g1 = _inproj(
            l, x2d.reshape(b, s, d), row(mix_norm), win, row(0.5 * b_gate), poolw, row(pool_scale),
            qg, kg, cos, sin_signed, wbp, hmask)
        ya = _attn(qa, ka, vat)
        next_jobs = [(ffn1_w_gate_up, l + 1), (ffn1_w_down, l + 1)] if l + 1 < depth else []
        x2d, nxt = _merge_ffn(l, x2d, p.reshape(b * s, d), g1.reshape(b * s, d),
                              ya.reshape(b * s, ATTN_WIDTH), wba, wo, row(ffn2_norm), wgu2, wd2,
                              next_jobs)
        if nxt:
            wgu, wd = nxt
    return x2d.reshape(b, s, d)
```

```python
import numpy as np

import jax
import jax.numpy as jnp
from jax import lax
from jax.experimental import pallas as pl
from jax.experimental.pallas import tpu as pltpu

F32 = jnp.float32
BF16 = jnp.bfloat16

D_MODEL = 1024
D_FF = 2816
POOL_WINDOWS = (2, 4, 8, 16)
POOL_WIDTH = 512
POOL_GROUP = 128
N_HEADS = 8
HEAD_DIM = 64
ATTN_WIDTH = 512
MOBA_BLOCK = 256
MOBA_TOP_K = 3
ROPE_THETA = 10000.0
EPS = 1e-6
NEG = -1e30

LANES = 128
SUBLANES = 8
MXU_TILE = 256
BF16_TILE_ROWS = 16
VT_ROWS = 80
HALO = 16
N_SLOTS = 16
FFN_TM = 512
FFN_TF = 256
INPROJ_TILES = 4
INPROJ_MM_ROWS = 256
ATTN_GROUP = 2
ATTN_HEADS = 4
VMEM_LIMIT = 60 * 1024 * 1024

_HALF = HEAD_DIM // 2
Q_SCALE = float(HEAD_DIM ** -0.5 * np.log2(np.e))
_GROUP_HEAD = (1, 3, 0, 2, 5, 7, 4, 6)
_HEAD_GROUP = tuple(_GROUP_HEAD.index(h) for h in range(N_HEADS))


def _qk_layout():
    c = np.arange(ATTN_WIDTH)
    lane = c % LANES
    head = 2 * (c // LANES) + (lane // _HALF) % 2
    dim = lane % _HALF + _HALF * (lane // HEAD_DIM)
    return head, dim


def _rms(x, g):
    ms = jnp.mean(x * x, axis=-1, keepdims=True)
    return x * lax.rsqrt(ms + EPS) * g


def _sigmoid_of_twice(half_x):
    return 0.5 * jnp.tanh(half_x) + 0.5


def _lane_iota(shape):
    return lax.broadcasted_iota(jnp.int32, shape, len(shape) - 1)


def _layer_spec(l, shape, **kwargs):
    return pl.BlockSpec((None,) + tuple(shape), lambda *_: (l,) + (0,) * len(shape), **kwargs)


def _dot_cols(lhs, w_ref, start, stop):
    return jnp.concatenate(
        [jnp.dot(lhs, w_ref[:, c:c + MXU_TILE], preferred_element_type=F32)
         for c in range(start, stop, MXU_TILE)], axis=-1)


def _split_bf16(x):
    hi = x.astype(BF16)
    return hi, (x - hi.astype(F32)).astype(BF16)


def _swiglu_residual(x, g_ref, wgu_ref, wd_ref, o_ref):
    h = _rms(x, g_ref[...]).astype(BF16)
    acc = jnp.zeros(x.shape, F32)
    for c in range(D_FF // FFN_TF):
        a = jnp.dot(h, wgu_ref[:, c * FFN_TF:(c + 1) * FFN_TF], preferred_element_type=F32)
        b = jnp.dot(h, wgu_ref[:, D_FF + c * FFN_TF:D_FF + (c + 1) * FFN_TF],
                    preferred_element_type=F32)
        act = (a * jax.nn.sigmoid(a) * b).astype(BF16)
        acc = acc + jnp.dot(act, wd_ref[c * FFN_TF:(c + 1) * FFN_TF, :],
                            preferred_element_type=F32)
    o_ref[...] = x + 0.5 * acc


def _run_cast_jobs(src_refs, dst_refs):
    for src, dst in zip(src_refs, dst_refs):
        dst[...] = src[...].astype(dst.dtype)


def _ffn_kernel(x_ref, g_ref, wgu_ref, wd_ref, *rest):
    n_jobs = (len(rest) - 1) // 2
    o_ref = rest[n_jobs]
    _swiglu_residual(x_ref[...], g_ref, wgu_ref, wd_ref, o_ref)
    _run_cast_jobs(rest[:n_jobs], rest[n_jobs + 1:])


def _merge_ffn_kernel(x_ref, p_ref, g1_ref, ya_ref, wba_ref, wo_ref, g_ref, wgu_ref, wd_ref,
                      *rest):
    n_jobs = (len(rest) - 1) // 2
    o_ref = rest[n_jobs]
    merged = p_ref[...] + g1_ref[...] * jnp.dot(ya_ref[...], wba_ref[...],
                                                preferred_element_type=F32)
    x = x_ref[...] + jnp.dot(merged.astype(BF16), wo_ref[...], preferred_element_type=F32)
    _swiglu_residual(x, g_ref, wgu_ref, wd_ref, o_ref)
    _run_cast_jobs(rest[:n_jobs], rest[n_jobs + 1:])


def _cast_job_specs(jobs, n_steps):
    in_specs, out_specs, out_shapes = [], [], []
    for w, l in jobs:
        _, r, c = w.shape
        n_blocks = max(nb for nb in range(1, n_steps + 1)
                       if n_steps % nb == 0 and r % (nb * BF16_TILE_ROWS) == 0)
        rows = r // n_blocks
        in_specs.append(pl.BlockSpec(
            (None, rows, c), lambda i, l=l, n_blocks=n_blocks: (l, (i * n_blocks) // n_steps, 0)))
        out_specs.append(pl.BlockSpec(
            (rows, c), lambda i, n_blocks=n_blocks: ((i * n_blocks) // n_steps, 0)))
        out_shapes.append(jax.ShapeDtypeStruct((r, c), BF16))
    return in_specs, out_specs, out_shapes


def _resident(w):
    return pl.BlockSpec(w.shape, lambda *_: (0, 0), pipeline_mode=pl.Buffered(1))


def _merge_ffn(l, x2d, p2d, g12d, ya2d, wba, wo, g, wgu, wd, cast_jobs=()):
    t = x2d.shape[0]
    n_steps = t // FFN_TM
    row = pl.BlockSpec((FFN_TM, D_MODEL), lambda i: (i, 0))
    job_in, job_out, job_shapes = _cast_job_specs(cast_jobs, n_steps)
    outs = pl.pallas_call(
        _merge_ffn_kernel,
        grid=(n_steps,),
        in_specs=[
            row, row, row,
            pl.BlockSpec((FFN_TM, ATTN_WIDTH), lambda i: (i, 0)),
            _resident(wba), _resident(wo),
            _layer_spec(l, (1, D_MODEL)),
            _resident(wgu), _resident(wd),
        ] + job_in,
        out_specs=[row] + job_out,
        out_shape=[jax.ShapeDtypeStruct((t, D_MODEL), F32)] + job_shapes,
        compiler_params=pltpu.CompilerParams(
            dimension_semantics=("arbitrary",), vmem_limit_bytes=VMEM_LIMIT),
        name="merge_ffn",
    )(x2d, p2d, g12d, ya2d, wba, wo, g, wgu, wd, *[w for w, _ in cast_jobs])
    return outs[0], outs[1:]


def _ffn(l, x2d, g, wgu, wd, cast_jobs=()):
    t = x2d.shape[0]
    n_steps = t // FFN_TM
    row = pl.BlockSpec((FFN_TM, D_MODEL), lambda i: (i, 0))
    job_in, job_out, job_shapes = _cast_job_specs(cast_jobs, n_steps)
    outs = pl.pallas_call(
        _ffn_kernel,
        grid=(n_steps,),
        in_specs=[row, _layer_spec(l, (1, D_MODEL)), _resident(wgu), _resident(wd)] + job_in,
        out_specs=[row] + job_out,
        out_shape=[jax.ShapeDtypeStruct((t, D_MODEL), F32)] + job_shapes,
        compiler_params=pltpu.CompilerParams(
            dimension_semantics=("arbitrary",), vmem_limit_bytes=VMEM_LIMIT),
        name="ffn",
    )(x2d, g, wgu, wd, *[w for w, _ in cast_jobs])
    return outs[0], outs[1:]


def _head_rms_rope(x, gain, cos, sin_signed):
    even = (_lane_iota((x.shape[0], LANES)) // _HALF) % 2 == 0
    outs = []
    for p in range(ATTN_WIDTH // LANES):
        lanes = slice(p * LANES, (p + 1) * LANES)
        xp = x[:, lanes]
        sq = xp * xp
        ss_even = jnp.sum(jnp.where(even, sq, 0.0), axis=-1, keepdims=True)
        ss_odd = jnp.sum(jnp.where(even, 0.0, sq), axis=-1, keepdims=True)
        ms = jnp.where(even, ss_even, ss_odd) * (1.0 / HEAD_DIM)
        yp = xp * lax.rsqrt(ms + EPS) * gain[:, lanes]
        outs.append(yp * cos + pltpu.roll(yp, HEAD_DIM, 1) * sin_signed)
    return outs


def _window_sums(halo, u, steps):
    s = jnp.concatenate([halo, u], axis=0)
    for t in range(steps):
        s = s + pltpu.roll(s, 1 << t, 0)
    return s[HALO:, :]


def _choose_blocks(gate_t, i):
    nq = gate_t.shape[1]
    g3 = gate_t.reshape(N_HEADS, N_SLOTS, nq)
    slot = lax.broadcasted_iota(jnp.int32, g3.shape, 1)
    gm = jnp.where(slot < i, g3, NEG)
    rank = jnp.zeros(g3.shape, jnp.int32)
    for j in range(N_SLOTS):
        other = jnp.broadcast_to(gm[:, j:j + 1, :], g3.shape)
        ahead = (other > gm) | ((other == gm) & (slot > j))
        rank = rank + jnp.where(ahead, 1, 0)
    chosen = (rank < MOBA_TOP_K) & (slot < i)
    bias = jnp.where(chosen | (slot == i), 0.0, NEG)
    return bias.reshape(N_HEADS * N_SLOTS, nq)


def _inproj_rows(row0, x, mixg, win_ref, bg, poolw_ref, pools, qg, kg, cos, sin_signed, wbp_ref,
                 halo_ref):
    tm = x.shape[0]
    h = _rms(x, mixg).astype(BF16)

    u = _dot_cols(h, win_ref, 0, POOL_WIDTH)
    pos = row0 + lax.broadcasted_iota(jnp.int32, (tm, 1), 0)
    ds = []
    for g, w in enumerate(POOL_WINDOWS):
        lanes = slice(g * POOL_GROUP, (g + 1) * POOL_GROUP)
        wsum = _window_sums(halo_ref[:, lanes], u[:, lanes], g + 1)
        cnt = jnp.minimum(pos + 1, w).astype(F32)
        ds.append((wsum / cnt - u[:, lanes]).astype(BF16))
    per_tile = MXU_TILE // POOL_GROUP
    ys = [jnp.dot(jnp.concatenate(ds[t * per_tile:(t + 1) * per_tile], axis=-1), poolw_ref[t],
                  preferred_element_type=F32) for t in range(len(ds) // per_tile)]
    y_pool = (jnp.concatenate(ys, axis=-1) * pools).astype(BF16)
    halo_ref[...] = u[tm - HALO:, :]

    gl = _dot_cols(h, win_ref, POOL_WIDTH + 3 * ATTN_WIDTH, win_ref.shape[1])
    gates = _sigmoid_of_twice(gl + bg)
    p_out = gates[:, :D_MODEL] * _dot_cols(y_pool, wbp_ref, 0, D_MODEL)
    g1_out = gates[:, D_MODEL:]

    o1 = POOL_WIDTH
    q = _dot_cols(h, win_ref, o1, o1 + ATTN_WIDTH)
    k = _dot_cols(h, win_ref, o1 + ATTN_WIDTH, o1 + 2 * ATTN_WIDTH)
    v = _dot_cols(h, win_ref, o1 + 2 * ATTN_WIDTH, o1 + 3 * ATTN_WIDTH)
    q_tiles = _head_rms_rope(q, qg, cos, sin_signed)
    k_tiles = _head_rms_rope(k, kg, cos, sin_signed)
    return p_out, g1_out, q_tiles, k_tiles, v


def _inproj_block(i, q_tiles, k_tiles, v, hmask_ref, kbt_ref):
    tm = v.shape[0]
    qn = jnp.concatenate(q_tiles, axis=-1)
    kn = jnp.concatenate(k_tiles, axis=-1)

    nt_dims = (((1,), (1,)), ((), ()))
    q_hi, q_lo = _split_bf16(qn)
    kb_hi, kb_lo = _split_bf16(kbt_ref[...])
    gate_t = (lax.dot_general(kb_hi, q_hi, nt_dims, preferred_element_type=F32)
              + lax.dot_general(kb_hi, q_lo, nt_dims, preferred_element_type=F32)
              + lax.dot_general(kb_lo, q_hi, nt_dims, preferred_element_type=F32))
    bias = _choose_blocks(gate_t, i).T

    kbar = jnp.sum(kn, axis=0, keepdims=True) * (1.0 / MOBA_BLOCK)
    row = lax.broadcasted_iota(jnp.int32, kbt_ref.shape, 0)
    mine = (row % N_SLOTS == i) & (hmask_ref[...] > 0.0)
    kbt_ref[...] = jnp.where(mine, kbar, kbt_ref[...])

    lane = _lane_iota((tm, LANES))
    ones_rows = jnp.where(lax.broadcasted_iota(jnp.int32, (VT_ROWS - HEAD_DIM, tm), 0) == 0,
                          1.0, 0.0)
    qa, ka, vat = [], [], []
    for hd in range(N_HEADS):
        p, odd, grp = hd // 2, hd % 2, _HEAD_GROUP[hd]
        own = (lane // _HALF) % 2 == odd
        vp = v[:, p * LANES:(p + 1) * LANES]
        qa.append(jnp.where(own, q_tiles[p] * Q_SCALE,
                            jnp.where(lane // N_SLOTS == grp, bias, 0.0)).astype(BF16))
        ka.append(jnp.where(own, k_tiles[p],
                            jnp.where(lane == grp * N_SLOTS + i, 1.0, 0.0)).astype(BF16))
        vt = vp.T[odd * HEAD_DIM:(odd + 1) * HEAD_DIM, :]
        vat.append(jnp.concatenate([vt, ones_rows], axis=0).astype(BF16))
    return qa, ka, vat


def _inproj_kernel(x_ref, mixg_ref, win_ref, bg_ref, poolw_ref, pools_ref, qg_ref, kg_ref,
                   cos_ref, sin_ref, wbp_ref, hmask_ref,
                   qa_ref, ka_ref, vat_ref, p_ref, g1_ref,
                   halo_ref, kbt_ref):
    step = pl.program_id(1)
    step_rows = x_ref.shape[1]

    @pl.when(step == 0)
    def _():
        halo_ref[...] = jnp.zeros(halo_ref.shape, F32)
        kbt_ref[...] = jnp.zeros(kbt_ref.shape, F32)

    for r0 in range(0, step_rows, INPROJ_MM_ROWS):
        rows = slice(r0, r0 + INPROJ_MM_ROWS)
        p_out, g1_out, q_tiles, k_tiles, v = _inproj_rows(
            step * step_rows + r0, x_ref[0, rows, :], mixg_ref[...], win_ref, bg_ref[...],
            poolw_ref, pools_ref[...], qg_ref[...], kg_ref[...], cos_ref[rows, :],
            sin_ref[rows, :], wbp_ref, halo_ref)
        p_ref[0, rows, :] = p_out
        g1_ref[0, rows, :] = g1_out
        for t0 in range(0, INPROJ_MM_ROWS, MOBA_BLOCK):
            sub = slice(t0, t0 + MOBA_BLOCK)
            blk = slice(r0 + t0, r0 + t0 + MOBA_BLOCK)
            qa, ka, vat = _inproj_block(
                (step * step_rows + r0 + t0) // MOBA_BLOCK, [q[sub] for q in q_tiles],
                [k[sub] for k in k_tiles], v[sub], hmask_ref, kbt_ref)
            for hd in range(N_HEADS):
                qa_ref[0, hd, blk, :] = qa[hd]
                ka_ref[0, hd, blk, :] = ka[hd]
                vat_ref[0, hd, :, blk] = vat[hd]


def _inproj(l, x, mixg, win, bg, poolw, pools, qg, kg, cos, sin_signed, wbp, hmask):
    b, s, _ = x.shape
    tm = INPROJ_TILES * MOBA_BLOCK
    const = lambda *shape: pl.BlockSpec(shape, lambda bi, i: (0,) * len(shape))
    head_spec = pl.BlockSpec((1, N_HEADS, tm, LANES), lambda bi, i: (bi, 0, i, 0))
    headt_spec = pl.BlockSpec((1, N_HEADS, VT_ROWS, tm), lambda bi, i: (bi, 0, 0, i))
    headt_shape = jax.ShapeDtypeStruct((b, N_HEADS, VT_ROWS, s), BF16)
    row_spec = pl.BlockSpec((1, tm, D_MODEL), lambda bi, i: (bi, i, 0))
    head_shape = jax.ShapeDtypeStruct((b, N_HEADS, s, LANES), BF16)
    row_shape = jax.ShapeDtypeStruct((b, s, D_MODEL), F32)
    return pl.pallas_call(
        _inproj_kernel,
        grid=(b, s // tm),
        in_specs=[
            row_spec,
            _layer_spec(l, (1, D_MODEL)),
            _layer_spec(l, win.shape[1:], pipeline_mode=pl.Buffered(1)),
            _layer_spec(l, (1, 2 * D_MODEL)),
            _layer_spec(l, poolw.shape[1:]),
            _layer_spec(l, (1, POOL_WIDTH)),
            _layer_spec(l, (1, ATTN_WIDTH)),
            _layer_spec(l, (1, ATTN_WIDTH)),
            pl.BlockSpec((tm, LANES), lambda bi, i: (i, 0)),
            pl.BlockSpec((tm, LANES), lambda bi, i: (i, 0)),
            _resident(wbp),
            const(N_HEADS * N_SLOTS, ATTN_WIDTH),
        ],
        out_specs=[head_spec, head_spec, headt_spec, row_spec, row_spec],
        out_shape=[head_shape, head_shape, headt_shape, row_shape, row_shape],
        scratch_shapes=[
            pltpu.VMEM((HALO, POOL_WIDTH), F32),
            pltpu.VMEM((N_HEADS * N_SLOTS, ATTN_WIDTH), F32),
        ],
        compiler_params=pltpu.CompilerParams(
            dimension_semantics=("arbitrary", "arbitrary"), vmem_limit_bytes=VMEM_LIMIT),
        name="inproj",
    )(x, mixg, win, bg, poolw, pools, qg, kg, cos, sin_signed, wbp, hmask)


def _score_tile(i, hh, j, masked, q, k_ref, s_ref):
    tq = MOBA_BLOCK
    s = lax.dot_general(k_ref[0, hh, j * tq:(j + 1) * tq, :], q, (((1,), (1,)), ((), ())),
                        preferred_element_type=F32)
    if masked:
        rel = (lax.broadcasted_iota(jnp.int32, (tq, tq), 0)
               - lax.broadcasted_iota(jnp.int32, (tq, tq), 1))
        s = jnp.where(rel <= (i - j) * tq, s, NEG)
    s_ref[j] = s
    return jnp.max(s.reshape(tq // SUBLANES, SUBLANES, tq), axis=0)


def _q_block(q_ref, hh, i):
    return q_ref[0, hh, pl.ds(pl.multiple_of(i * MOBA_BLOCK, MOBA_BLOCK), MOBA_BLOCK), :]


def _attn_block(i, nk, mrun, q_ref, k_ref, vt_ref, o_ref, s_refs):
    tq = MOBA_BLOCK
    q_rows = pl.ds(pl.multiple_of(i * tq, tq), tq)
    outs = []
    for hh in range(ATTN_HEADS):
        m = jnp.broadcast_to(jnp.max(mrun, axis=0, keepdims=True), (tq, tq))
        s_ref = s_refs[hh]
        if hh + 1 < ATTN_HEADS:
            nxt_h, nxt_i = hh + 1, i
        else:
            nxt_h, nxt_i = 0, jnp.minimum(i + 1, N_SLOTS - 1)
        q = _q_block(q_ref, nxt_h, nxt_i)
        mrun, pts = None, []
        for j in range(nk):
            tile_max = _score_tile(nxt_i, nxt_h, j, j >= nk - ATTN_GROUP, q, k_ref,
                                   s_refs[nxt_h])
            mrun = tile_max if mrun is None else jnp.maximum(mrun, tile_max)
            pts.append(jnp.exp2(s_ref[j] - m).astype(BF16))
        acc = jnp.dot(vt_ref[0, hh, :, 0:nk * tq], jnp.concatenate(pts, axis=0),
                      preferred_element_type=F32)
        outs.append(acc[:HEAD_DIM, :] / acc[HEAD_DIM:HEAD_DIM + 1, :])
        if hh % 2:
            pair = jnp.concatenate([outs[hh - 1], outs[hh]], axis=0)
            o_ref[0, q_rows, (hh // 2) * LANES:(hh // 2 + 1) * LANES] = pair.T.astype(o_ref.dtype)
    return mrun


def _attn_kernel(q_ref, k_ref, vt_ref, o_ref, *s_refs):
    mrun = jnp.full((SUBLANES, MOBA_BLOCK), NEG, F32)
    for c in range(N_SLOTS // ATTN_GROUP):
        nk, first = ATTN_GROUP * (c + 1), ATTN_GROUP * c
        q = _q_block(q_ref, 0, first)
        for j in range(nk - ATTN_GROUP, nk):
            mrun = jnp.maximum(mrun, _score_tile(first, 0, j, True, q, k_ref, s_refs[0]))

        def body(i, mrun, nk=nk):
            return _attn_block(i, nk, mrun, q_ref, k_ref, vt_ref, o_ref, s_refs)
        mrun = lax.fori_loop(first, first + ATTN_GROUP, body, mrun)


def _attn(qa, ka, vat):
    b, nh, s, _ = qa.shape
    tq = MOBA_BLOCK
    qk_spec = pl.BlockSpec((1, ATTN_HEADS, s, LANES), lambda bi, p: (bi, p, 0, 0))
    vt_spec = pl.BlockSpec((1, ATTN_HEADS, VT_ROWS, s), lambda bi, p: (bi, p, 0, 0))
    return pl.pallas_call(
        _attn_kernel,
        grid=(b, nh // ATTN_HEADS),
        in_specs=[qk_spec, qk_spec, vt_spec],
        out_specs=pl.BlockSpec((1, s, HEAD_DIM * ATTN_HEADS), lambda bi, p: (bi, 0, p)),
        out_shape=jax.ShapeDtypeStruct((b, s, ATTN_WIDTH), BF16),
        scratch_shapes=[pltpu.VMEM((s // tq, tq, tq), F32)] * ATTN_HEADS,
        compiler_params=pltpu.CompilerParams(
            dimension_semantics=("arbitrary", "arbitrary"),
            vmem_limit_bytes=VMEM_LIMIT),
        name="attn",
    )(qa, ka, vat)


def _rope_tables(s):
    inv_freq = 1.0 / ROPE_THETA ** (np.arange(_HALF, dtype=np.float64) * (2.0 / HEAD_DIM))
    ang = np.arange(s, dtype=np.float64)[:, None] * inv_freq[None, :]
    cos, sin = np.cos(ang), np.sin(ang)
    cos = np.tile(cos, (1, LANES // _HALF))
    sin_signed = np.concatenate([-sin, -sin, sin, sin], axis=-1)
    return jnp.asarray(cos, F32), jnp.asarray(sin_signed, F32)


def _permute_heads(w):
    lead = w.shape[:-1]
    w = w.reshape(lead + (N_HEADS // 2, 2, 2, _HALF))
    return jnp.swapaxes(w, -3, -2).reshape(lead + (ATTN_WIDTH,))


def kernel(x, ffn1_norm, ffn1_w_gate_up, ffn1_w_down, mix_norm, w_in, b_gate, pool_w, pool_scale,
           q_norm, k_norm, w_branch_pool, w_branch_attn, w_out, ffn2_norm, ffn2_w_gate_up,
           ffn2_w_down):
    b, s, d = x.shape
    assert s == N_SLOTS * MOBA_BLOCK and d == D_MODEL
    depth = ffn1_norm.shape[0]
    cos, sin_signed = _rope_tables(s)
    head, dim = _qk_layout()
    hmask = jnp.asarray(np.repeat(np.asarray(_GROUP_HEAD), N_SLOTS)[:, None] == head[None, :], F32)
    o1, o2, o3 = POOL_WIDTH, POOL_WIDTH + ATTN_WIDTH, POOL_WIDTH + 2 * ATTN_WIDTH
    o4 = o3 + ATTN_WIDTH

    row = lambda p: p[:, None, :]
    win = jnp.concatenate([w_in[..., :o1], _permute_heads(w_in[..., o1:o2]),
                           _permute_heads(w_in[..., o2:o3]), w_in[..., o3:o4],
                           0.5 * w_in[..., o4:]], axis=-1).astype(BF16)
    qg = row(q_norm[:, dim])
    kg = row(k_norm[:, dim])
    per_tile = MXU_TILE // POOL_GROUP
    n_tiles = len(POOL_WINDOWS) // per_tile
    poolw = jnp.einsum('ltaij,ab->ltaibj',
                       pool_w.reshape(depth, n_tiles, per_tile, POOL_GROUP, POOL_GROUP),
                       jnp.eye(per_tile, dtype=F32)
                       ).reshape(depth, n_tiles, MXU_TILE, MXU_TILE).astype(BF16)

    wgu, wd = ffn1_w_gate_up[0].astype(BF16), ffn1_w_down[0].astype(BF16)
    x2d = x.reshape(b * s, d)
    for l in range(depth):
        mixer_jobs = [(ffn2_w_gate_up, l), (ffn2_w_down, l), (w_branch_pool, l),
                      (w_branch_attn, l), (w_out, l)]
        x2d, (wgu2, wd2, wbp, wba, wo) = _ffn(l, x2d, row(ffn1_norm), wgu, wd, mixer_jobs)
        qa, ka, vat, p, g1 = _inproj(
            l, x2d.reshape(b, s, d), row(mix_norm), win, row(0.5 * b_gate), poolw, row(pool_scale),
            qg, kg, cos, sin_signed, wbp, hmask)
        ya = _attn(qa, ka, vat)
        next_jobs = [(ffn1_w_gate_up, l + 1), (ffn1_w_down, l + 1)] if l + 1 < depth else []
        x2d, nxt = _merge_ffn(l, x2d, p.reshape(b * s, d), g1.reshape(b * s, d),
                              ya.reshape(b * s, ATTN_WIDTH), wba, wo, row(ffn2_norm), wgu2, wd2,
                              next_jobs)
        if nxt:
            wgu, wd = nxt
    return x2d.reshape(b, s, d)
```

```python
import numpy as np

import jax
import jax.numpy as jnp
from jax import lax
from jax.experimental import pallas as pl
from jax.experimental.pallas import tpu as pltpu

F32 = jnp.float32
BF16 = jnp.bfloat16

D_MODEL = 1024
D_FF = 2816
POOL_WINDOWS = (2, 4, 8, 16)
POOL_WIDTH = 512
POOL_GROUP = 128
N_HEADS = 8
HEAD_DIM = 64
ATTN_WIDTH = 512
MOBA_BLOCK = 256
MOBA_TOP_K = 3
ROPE_THETA = 10000.0
EPS = 1e-6
NEG = -1e30

LANES = 128
SUBLANES = 8
MXU_TILE = 256
BF16_TILE_ROWS = 16
VT_ROWS = 80
HALO = 16
N_SLOTS = 16
FFN_TM = 512
FFN_TF = 256
INPROJ_TILES = 4
ATTN_GROUP = 2
ATTN_HEADS = 4
VMEM_LIMIT = 60 * 1024 * 1024

_HALF = HEAD_DIM // 2
Q_SCALE = float(HEAD_DIM ** -0.5 * np.log2(np.e))
_GROUP_HEAD = (1, 3, 0, 2, 5, 7, 4, 6)
_HEAD_GROUP = tuple(_GROUP_HEAD.index(h) for h in range(N_HEADS))


def _qk_layout():
    c = np.arange(ATTN_WIDTH)
    lane = c % LANES
    head = 2 * (c // LANES) + (lane // _HALF) % 2
    dim = lane % _HALF + _HALF * (lane // HEAD_DIM)
    return head, dim


def _rms(x, g):
    ms = jnp.mean(x * x, axis=-1, keepdims=True)
    return x * lax.rsqrt(ms + EPS) * g


def _sigmoid_of_twice(half_x):
    return 0.5 * jnp.tanh(half_x) + 0.5


def _lane_iota(shape):
    return lax.broadcasted_iota(jnp.int32, shape, len(shape) - 1)


def _layer_spec(l, shape, **kwargs):
    return pl.BlockSpec((None,) + tuple(shape), lambda *_: (l,) + (0,) * len(shape), **kwargs)


def _split_bf16(x):
    hi = x.astype(BF16)
    return hi, (x - hi.astype(F32)).astype(BF16)


def _swiglu_residual(x, g_ref, wgu_ref, wd_ref, o_ref):
    h = _rms(x, g_ref[...]).astype(BF16)
    acc = jnp.zeros(x.shape, F32)
    for c in range(D_FF // FFN_TF):
        a = jnp.dot(h, wgu_ref[:, c * FFN_TF:(c + 1) * FFN_TF], preferred_element_type=F32)
        b = jnp.dot(h, wgu_ref[:, D_FF + c * FFN_TF:D_FF + (c + 1) * FFN_TF],
                    preferred_element_type=F32)
        act = (a * jax.nn.sigmoid(a) * b).astype(BF16)
        acc = acc + jnp.dot(act, wd_ref[c * FFN_TF:(c + 1) * FFN_TF, :],
                            preferred_element_type=F32)
    o_ref[...] = x + 0.5 * acc


def _run_cast_jobs(src_refs, dst_refs):
    for src, dst in zip(src_refs, dst_refs):
        dst[...] = src[...].astype(dst.dtype)


def _ffn_kernel(x_ref, g_ref, wgu_ref, wd_ref, *rest):
    n_jobs = (len(rest) - 1) // 2
    o_ref = rest[n_jobs]
    _swiglu_residual(x_ref[...], g_ref, wgu_ref, wd_ref, o_ref)
    _run_cast_jobs(rest[:n_jobs], rest[n_jobs + 1:])


def _merge_ffn_kernel(x_ref, p_ref, g1_ref, ya_ref, wba_ref, wo_ref, g_ref, wgu_ref, wd_ref,
                      *rest):
    n_jobs = (len(rest) - 1) // 2
    o_ref = rest[n_jobs]
    merged = p_ref[...] + g1_ref[...] * jnp.dot(ya_ref[...], wba_ref[...],
                                                preferred_element_type=F32)
    x = x_ref[...] + jnp.dot(merged.astype(BF16), wo_ref[...], preferred_element_type=F32)
    _swiglu_residual(x, g_ref, wgu_ref, wd_ref, o_ref)
    _run_cast_jobs(rest[:n_jobs], rest[n_jobs + 1:])


def _cast_job_specs(jobs, n_steps):
    in_specs, out_specs, out_shapes = [], [], []
    for w, l in jobs:
        _, r, c = w.shape
        n_blocks = max(nb for nb in range(1, n_steps + 1)
                       if n_steps % nb == 0 and r % (nb * BF16_TILE_ROWS) == 0)
        rows = r // n_blocks
        in_specs.append(pl.BlockSpec(
            (None, rows, c), lambda i, l=l, n_blocks=n_blocks: (l, (i * n_blocks) // n_steps, 0)))
        out_specs.append(pl.BlockSpec(
            (rows, c), lambda i, n_blocks=n_blocks: ((i * n_blocks) // n_steps, 0)))
        out_shapes.append(jax.ShapeDtypeStruct((r, c), BF16))
    return in_specs, out_specs, out_shapes


def _resident(w):
    return pl.BlockSpec(w.shape, lambda *_: (0, 0), pipeline_mode=pl.Buffered(1))


def _merge_ffn(l, x2d, p2d, g12d, ya2d, wba, wo, g, wgu, wd, cast_jobs=()):
    t = x2d.shape[0]
    n_steps = t // FFN_TM
    row = pl.BlockSpec((FFN_TM, D_MODEL), lambda i: (i, 0))
    job_in, job_out, job_shapes = _cast_job_specs(cast_jobs, n_steps)
    outs = pl.pallas_call(
        _merge_ffn_kernel,
        grid=(n_steps,),
        in_specs=[
            row, row, row,
            pl.BlockSpec((FFN_TM, ATTN_WIDTH), lambda i: (i, 0)),
            _resident(wba), _resident(wo),
            _layer_spec(l, (1, D_MODEL)),
            _resident(wgu), _resident(wd),
        ] + job_in,
        out_specs=[row] + job_out,
        out_shape=[jax.ShapeDtypeStruct((t, D_MODEL), F32)] + job_shapes,
        compiler_params=pltpu.CompilerParams(
            dimension_semantics=("arbitrary",), vmem_limit_bytes=VMEM_LIMIT),
        name="merge_ffn",
    )(x2d, p2d, g12d, ya2d, wba, wo, g, wgu, wd, *[w for w, _ in cast_jobs])
    return outs[0], outs[1:]


def _ffn(l, x2d, g, wgu, wd, cast_jobs=()):
    t = x2d.shape[0]
    n_steps = t // FFN_TM
    row = pl.BlockSpec((FFN_TM, D_MODEL), lambda i: (i, 0))
    job_in, job_out, job_shapes = _cast_job_specs(cast_jobs, n_steps)
    outs = pl.pallas_call(
        _ffn_kernel,
        grid=(n_steps,),
        in_specs=[row, _layer_spec(l, (1, D_MODEL)), _resident(wgu), _resident(wd)] + job_in,
        out_specs=[row] + job_out,
        out_shape=[jax.ShapeDtypeStruct((t, D_MODEL), F32)] + job_shapes,
        compiler_params=pltpu.CompilerParams(
            dimension_semantics=("arbitrary",), vmem_limit_bytes=VMEM_LIMIT),
        name="ffn",
    )(x2d, g, wgu, wd, *[w for w, _ in cast_jobs])
    return outs[0], outs[1:]


def _head_rms_rope(x, gain, cos, sin_signed):
    even = (_lane_iota((x.shape[0], LANES)) // _HALF) % 2 == 0
    outs = []
    for p in range(ATTN_WIDTH // LANES):
        lanes = slice(p * LANES, (p + 1) * LANES)
        xp = x[:, lanes]
        sq = xp * xp
        ss_even = jnp.sum(jnp.where(even, sq, 0.0), axis=-1, keepdims=True)
        ss_odd = jnp.sum(jnp.where(even, 0.0, sq), axis=-1, keepdims=True)
        ms = jnp.where(even, ss_even, ss_odd) * (1.0 / HEAD_DIM)
        yp = xp * lax.rsqrt(ms + EPS) * gain[:, lanes]
        outs.append(yp * cos + pltpu.roll(yp, HEAD_DIM, 1) * sin_signed)
    return outs


def _window_sums(halo, u, steps):
    s = jnp.concatenate([halo, u], axis=0)
    for t in range(steps):
        s = s + pltpu.roll(s, 1 << t, 0)
    return s[HALO:, :]


def _choose_blocks(gate_t, i):
    nq = gate_t.shape[1]
    g3 = gate_t.reshape(N_HEADS, N_SLOTS, nq)
    slot = lax.broadcasted_iota(jnp.int32, g3.shape, 1)
    gm = jnp.where(slot < i, g3, NEG)
    rank = jnp.zeros(g3.shape, jnp.int32)
    for j in range(N_SLOTS):
        other = jnp.broadcast_to(gm[:, j:j + 1, :], g3.shape)
        ahead = (other > gm) | ((other == gm) & (slot > j))
        rank = rank + jnp.where(ahead, 1, 0)
    chosen = (rank < MOBA_TOP_K) & (slot < i)
    bias = jnp.where(chosen | (slot == i), 0.0, NEG)
    return bias.reshape(N_HEADS * N_SLOTS, nq)


def _inproj_rows(row0, x, mixg, win_ref, bg, poolw_ref, pools, qg, kg, cos, sin_signed, wbp_ref,
                 halo_ref):
    tm = x.shape[0]
    h = _rms(x, mixg).astype(BF16)

    u = jnp.dot(h, win_ref[:, 0:POOL_WIDTH], preferred_element_type=F32)
    pos = row0 + lax.broadcasted_iota(jnp.int32, (tm, 1), 0)
    ds = []
    for g, w in enumerate(POOL_WINDOWS):
        lanes = slice(g * POOL_GROUP, (g + 1) * POOL_GROUP)
        wsum = _window_sums(halo_ref[:, lanes], u[:, lanes], g + 1)
        cnt = jnp.minimum(pos + 1, w).astype(F32)
        ds.append((wsum / cnt - u[:, lanes]).astype(BF16))
    per_tile = MXU_TILE // POOL_GROUP
    ys = [jnp.dot(jnp.concatenate(ds[t * per_tile:(t + 1) * per_tile], axis=-1), poolw_ref[t],
                  preferred_element_type=F32) for t in range(len(ds) // per_tile)]
    y_pool = (jnp.concatenate(ys, axis=-1) * pools).astype(BF16)
    halo_ref[...] = u[tm - HALO:, :]

    gl = jnp.dot(h, win_ref[:, POOL_WIDTH + 3 * ATTN_WIDTH:], preferred_element_type=F32)
    gates = _sigmoid_of_twice(gl + bg)
    p_out = gates[:, :D_MODEL] * jnp.dot(y_pool, wbp_ref[...], preferred_element_type=F32)
    g1_out = gates[:, D_MODEL:]

    o1 = POOL_WIDTH
    q = jnp.dot(h, win_ref[:, o1:o1 + ATTN_WIDTH], preferred_element_type=F32)
    k = jnp.dot(h, win_ref[:, o1 + ATTN_WIDTH:o1 + 2 * ATTN_WIDTH], preferred_element_type=F32)
    v = jnp.dot(h, win_ref[:, o1 + 2 * ATTN_WIDTH:o1 + 3 * ATTN_WIDTH],
                preferred_element_type=F32)
    q_tiles = _head_rms_rope(q, qg, cos, sin_signed)
    k_tiles = _head_rms_rope(k, kg, cos, sin_signed)
    return p_out, g1_out, q_tiles, k_tiles, v


def _inproj_block(i, q_tiles, k_tiles, v, hmask_ref, kbt_ref):
    tm = v.shape[0]
    qn = jnp.concatenate(q_tiles, axis=-1)
    kn = jnp.concatenate(k_tiles, axis=-1)

    nt_dims = (((1,), (1,)), ((), ()))
    q_hi, q_lo = _split_bf16(qn)
    kb_hi, kb_lo = _split_bf16(kbt_ref[...])
    gate_t = (lax.dot_general(kb_hi, q_hi, nt_dims, preferred_element_type=F32)
              + lax.dot_general(kb_hi, q_lo, nt_dims, preferred_element_type=F32)
              + lax.dot_general(kb_lo, q_hi, nt_dims, preferred_element_type=F32))
    bias = _choose_blocks(gate_t, i).T

    kbar = jnp.sum(kn, axis=0, keepdims=True) * (1.0 / MOBA_BLOCK)
    row = lax.broadcasted_iota(jnp.int32, kbt_ref.shape, 0)
    mine = (row % N_SLOTS == i) & (hmask_ref[...] > 0.0)
    kbt_ref[...] = jnp.where(mine, kbar, kbt_ref[...])

    lane = _lane_iota((tm, LANES))
    ones_rows = jnp.where(lax.broadcasted_iota(jnp.int32, (VT_ROWS - HEAD_DIM, tm), 0) == 0,
                          1.0, 0.0)
    qa, ka, vat = [], [], []
    for hd in range(N_HEADS):
        p, odd, grp = hd // 2, hd % 2, _HEAD_GROUP[hd]
        own = (lane // _HALF) % 2 == odd
        vp = v[:, p * LANES:(p + 1) * LANES]
        qa.append(jnp.where(own, q_tiles[p] * Q_SCALE,
                            jnp.where(lane // N_SLOTS == grp, bias, 0.0)).astype(BF16))
        ka.append(jnp.where(own, k_tiles[p],
                            jnp.where(lane == grp * N_SLOTS + i, 1.0, 0.0)).astype(BF16))
        vt = vp.T[odd * HEAD_DIM:(odd + 1) * HEAD_DIM, :]
        vat.append(jnp.concatenate([vt, ones_rows], axis=0).astype(BF16))
    return qa, ka, vat


def _inproj_kernel(x_ref, mixg_ref, win_ref, bg_ref, poolw_ref, pools_ref, qg_ref, kg_ref,
                   cos_ref, sin_ref, wbp_ref, hmask_ref,
                   qa_ref, ka_ref, vat_ref, p_ref, g1_ref,
                   halo_ref, kbt_ref):
    step = pl.program_id(1)
    step_rows = x_ref.shape[1]

    @pl.when(step == 0)
    def _():
        halo_ref[...] = jnp.zeros(halo_ref.shape, F32)
        kbt_ref[...] = jnp.zeros(kbt_ref.shape, F32)

    for r0 in range(0, step_rows, MOBA_BLOCK):
        rows = slice(r0, r0 + MOBA_BLOCK)
        p_out, g1_out, q_tiles, k_tiles, v = _inproj_rows(
            step * step_rows + r0, x_ref[0, rows, :], mixg_ref[...], win_ref, bg_ref[...],
            poolw_ref, pools_ref[...], qg_ref[...], kg_ref[...], cos_ref[rows, :],
            sin_ref[rows, :], wbp_ref, halo_ref)
        p_ref[0, rows, :] = p_out
        g1_ref[0, rows, :] = g1_out
        qa, ka, vat = _inproj_block((step * step_rows + r0) // MOBA_BLOCK, q_tiles, k_tiles, v,
                                    hmask_ref, kbt_ref)
        for hd in range(N_HEADS):
            qa_ref[0, hd, rows, :] = qa[hd]
            ka_ref[0, hd, rows, :] = ka[hd]
            vat_ref[0, hd, :, rows] = vat[hd]


def _inproj(l, x, mixg, win, bg, poolw, pools, qg, kg, cos, sin_signed, wbp, hmask):
    b, s, _ = x.shape
    tm = INPROJ_TILES * MOBA_BLOCK
    const = lambda *shape: pl.BlockSpec(shape, lambda bi, i: (0,) * len(shape))
    head_spec = pl.BlockSpec((1, N_HEADS, tm, LANES), lambda bi, i: (bi, 0, i, 0))
    headt_spec = pl.BlockSpec((1, N_HEADS, VT_ROWS, tm), lambda bi, i: (bi, 0, 0, i))
    headt_shape = jax.ShapeDtypeStruct((b, N_HEADS, VT_ROWS, s), BF16)
    row_spec = pl.BlockSpec((1, tm, D_MODEL), lambda bi, i: (bi, i, 0))
    head_shape = jax.ShapeDtypeStruct((b, N_HEADS, s, LANES), BF16)
    row_shape = jax.ShapeDtypeStruct((b, s, D_MODEL), F32)
    return pl.pallas_call(
        _inproj_kernel,
        grid=(b, s // tm),
        in_specs=[
            row_spec,
            _layer_spec(l, (1, D_MODEL)),
            _layer_spec(l, win.shape[1:], pipeline_mode=pl.Buffered(1)),
            _layer_spec(l, (1, 2 * D_MODEL)),
            _layer_spec(l, poolw.shape[1:]),
            _layer_spec(l, (1, POOL_WIDTH)),
            _layer_spec(l, (1, ATTN_WIDTH)),
            _layer_spec(l, (1, ATTN_WIDTH)),
            pl.BlockSpec((tm, LANES), lambda bi, i: (i, 0)),
            pl.BlockSpec((tm, LANES), lambda bi, i: (i, 0)),
            _resident(wbp),
            const(N_HEADS * N_SLOTS, ATTN_WIDTH),
        ],
        out_specs=[head_spec, head_spec, headt_spec, row_spec, row_spec],
        out_shape=[head_shape, head_shape, headt_shape, row_shape, row_shape],
        scratch_shapes=[
            pltpu.VMEM((HALO, POOL_WIDTH), F32),
            pltpu.VMEM((N_HEADS * N_SLOTS, ATTN_WIDTH), F32),
        ],
        compiler_params=pltpu.CompilerParams(
            dimension_semantics=("arbitrary", "arbitrary"), vmem_limit_bytes=VMEM_LIMIT),
        name="inproj",
    )(x, mixg, win, bg, poolw, pools, qg, kg, cos, sin_signed, wbp, hmask)


def _score_tile(i, hh, j, masked, q, k_ref, s_ref):
    tq = MOBA_BLOCK
    s = lax.dot_general(k_ref[0, hh, j * tq:(j + 1) * tq, :], q, (((1,), (1,)), ((), ())),
                        preferred_element_type=F32)
    if masked:
        rel = (lax.broadcasted_iota(jnp.int32, (tq, tq), 0)
               - lax.broadcasted_iota(jnp.int32, (tq, tq), 1))
        s = jnp.where(rel <= (i - j) * tq, s, NEG)
    s_ref[j] = s
    return jnp.max(s.reshape(tq // SUBLANES, SUBLANES, tq), axis=0)


def _q_block(q_ref, hh, i):
    return q_ref[0, hh, pl.ds(pl.multiple_of(i * MOBA_BLOCK, MOBA_BLOCK), MOBA_BLOCK), :]


def _attn_block(i, nk, mrun, q_ref, k_ref, vt_ref, o_ref, s_refs):
    tq = MOBA_BLOCK
    q_rows = pl.ds(pl.multiple_of(i * tq, tq), tq)
    outs = []
    for hh in range(ATTN_HEADS):
        m = jnp.broadcast_to(jnp.max(mrun, axis=0, keepdims=True), (tq, tq))
        s_ref = s_refs[hh]
        if hh + 1 < ATTN_HEADS:
            nxt_h, nxt_i = hh + 1, i
        else:
            nxt_h, nxt_i = 0, jnp.minimum(i + 1, N_SLOTS - 1)
        q = _q_block(q_ref, nxt_h, nxt_i)
        mrun, pts = None, []
        for j in range(nk):
            tile_max = _score_tile(nxt_i, nxt_h, j, j >= nk - ATTN_GROUP, q, k_ref,
                                   s_refs[nxt_h])
            mrun = tile_max if mrun is None else jnp.maximum(mrun, tile_max)
            pts.append(jnp.exp2(s_ref[j] - m).astype(BF16))
        acc = jnp.dot(vt_ref[0, hh, :, 0:nk * tq], jnp.concatenate(pts, axis=0),
                      preferred_element_type=F32)
        outs.append(acc[:HEAD_DIM, :] / acc[HEAD_DIM:HEAD_DIM + 1, :])
        if hh % 2:
            pair = jnp.concatenate([outs[hh - 1], outs[hh]], axis=0)
            o_ref[0, q_rows, (hh // 2) * LANES:(hh // 2 + 1) * LANES] = pair.T.astype(o_ref.dtype)
    return mrun


def _attn_kernel(q_ref, k_ref, vt_ref, o_ref, *s_refs):
    mrun = jnp.full((SUBLANES, MOBA_BLOCK), NEG, F32)
    for c in range(N_SLOTS // ATTN_GROUP):
        nk, first = ATTN_GROUP * (c + 1), ATTN_GROUP * c
        q = _q_block(q_ref, 0, first)
        for j in range(nk - ATTN_GROUP, nk):
            mrun = jnp.maximum(mrun, _score_tile(first, 0, j, True, q, k_ref, s_refs[0]))

        def body(i, mrun, nk=nk):
            return _attn_block(i, nk, mrun, q_ref, k_ref, vt_ref, o_ref, s_refs)
        mrun = lax.fori_loop(first, first + ATTN_GROUP, body, mrun)


def _attn(qa, ka, vat):
    b, nh, s, _ = qa.shape
    tq = MOBA_BLOCK
    qk_spec = pl.BlockSpec((1, ATTN_HEADS, s, LANES), lambda bi, p: (bi, p, 0, 0))
    vt_spec = pl.BlockSpec((1, ATTN_HEADS, VT_ROWS, s), lambda bi, p: (bi, p, 0, 0))
    return pl.pallas_call(
        _attn_kernel,
        grid=(b, nh // ATTN_HEADS),
        in_specs=[qk_spec, qk_spec, vt_spec],
        out_specs=pl.BlockSpec((1, s, HEAD_DIM * ATTN_HEADS), lambda bi, p: (bi, 0, p)),
        out_shape=jax.ShapeDtypeStruct((b, s, ATTN_WIDTH), BF16),
        scratch_shapes=[pltpu.VMEM((s // tq, tq, tq), F32)] * ATTN_HEADS,
        compiler_params=pltpu.CompilerParams(
            dimension_semantics=("arbitrary", "arbitrary"),
            vmem_limit_bytes=VMEM_LIMIT),
        name="attn",
    )(qa, ka, vat)


def _rope_tables(s):
    inv_freq = 1.0 / ROPE_THETA ** (np.arange(_HALF, dtype=np.float64) * (2.0 / HEAD_DIM))
    ang = np.arange(s, dtype=np.float64)[:, None] * inv_freq[None, :]
    cos, sin = np.cos(ang), np.sin(ang)
    cos = np.tile(cos, (1, LANES // _HALF))
    sin_signed = np.concatenate([-sin, -sin, sin, sin], axis=-1)
    return jnp.asarray(cos, F32), jnp.asarray(sin_signed, F32)


def _permute_heads(w):
    lead = w.shape[:-1]
    w = w.reshape(lead + (N_HEADS // 2, 2, 2, _HALF))
    return jnp.swapaxes(w, -3, -2).reshape(lead + (ATTN_WIDTH,))


def kernel(x, ffn1_norm, ffn1_w_gate_up, ffn1_w_down, mix_norm, w_in, b_gate, pool_w, pool_scale,
           q_norm, k_norm, w_branch_pool, w_branch_attn, w_out, ffn2_norm, ffn2_w_gate_up,
           ffn2_w_down):
    b, s, d = x.shape
    assert s == N_SLOTS * MOBA_BLOCK and d == D_MODEL
    depth = ffn1_norm.shape[0]
    cos, sin_signed = _rope_tables(s)
    head, dim = _qk_layout()
    hmask = jnp.asarray(np.repeat(np.asarray(_GROUP_HEAD), N_SLOTS)[:, None] == head[None, :], F32)
    o1, o2, o3 = POOL_WIDTH, POOL_WIDTH + ATTN_WIDTH, POOL_WIDTH + 2 * ATTN_WIDTH
    o4 = o3 + ATTN_WIDTH

    row = lambda p: p[:, None, :]
    win = jnp.concatenate([w_in[..., :o1], _permute_heads(w_in[..., o1:o2]),
                           _permute_heads(w_in[..., o2:o3]), w_in[..., o3:o4],
                           0.5 * w_in[..., o4:]], axis=-1).astype(BF16)
    qg = row(q_norm[:, dim])
    kg = row(k_norm[:, dim])
    per_tile = MXU_TILE // POOL_GROUP
    n_tiles = len(POOL_WINDOWS) // per_tile
    poolw = jnp.einsum('ltaij,ab->ltaibj',
                       pool_w.reshape(depth, n_tiles, per_tile, POOL_GROUP, POOL_GROUP),
                       jnp.eye(per_tile, dtype=F32)
                       ).reshape(depth, n_tiles, MXU_TILE, MXU_TILE).astype(BF16)

    wgu, wd = ffn1_w_gate_up[0].astype(BF16), ffn1_w_down[0].astype(BF16)
    x2d = x.reshape(b * s, d)
    for l in range(depth):
        mixer_jobs = [(ffn2_w_gate_up, l), (ffn2_w_down, l), (w_branch_pool, l),
                      (w_branch_attn, l), (w_out, l)]
        x2d, (wgu2, wd2, wbp, wba, wo) = _ffn(l, x2d, row(ffn1_norm), wgu, wd, mixer_jobs)
        qa, ka, vat, p, g1 = _inproj(
            l, x2d.reshape(b, s, d), row(mix_norm), win, row(0.5 * b_gate), poolw, row(pool_scale),
            qg, kg, cos, sin_signed, wbp, hmask)
        ya = _attn(qa, ka, vat)
        next_jobs = [(ffn1_w_gate_up, l + 1), (ffn1_w_down, l + 1)] if l + 1 < depth else []
        x2d, nxt = _merge_ffn(l, x2d, p.reshape(b * s, d), g1.reshape(b * s, d),
                              ya.reshape(b * s, ATTN_WIDTH), wba, wo, row(ffn2_norm), wgu2, wd2,
                              next_jobs)
        if nxt:
            wgu, wd = nxt
    return x2d.reshape(b, s, d)
```

```python
import numpy as np

import jax
import jax.numpy as jnp
from jax import lax
from jax.experimental import pallas as pl
from jax.experimental.pallas import tpu as pltpu

F32 = jnp.float32
BF16 = jnp.bfloat16

D_MODEL = 1024
D_FF = 2816
POOL_WINDOWS = (2, 4, 8, 16)
POOL_WIDTH = 512
POOL_GROUP = 128
N_HEADS = 8
HEAD_DIM = 64
ATTN_WIDTH = 512
MOBA_BLOCK = 256
MOBA_TOP_K = 3
ROPE_THETA = 10000.0
EPS = 1e-6
NEG = -1e30

LANES = 128
SUBLANES = 8
MXU_TILE = 256
BF16_TILE_ROWS = 16
VT_ROWS = 80
HALO = 16
N_SLOTS = 16
FFN_TM = 512
FFN_TF = 256
INPROJ_TILES = 4
ATTN_GROUP = 2
ATTN_HEADS = 4
VMEM_LIMIT = 60 * 1024 * 1024

_HALF = HEAD_DIM // 2
Q_SCALE = float(HEAD_DIM ** -0.5 * np.log2(np.e))
_GROUP_HEAD = (1, 3, 0, 2, 5, 7, 4, 6)
_HEAD_GROUP = tuple(_GROUP_HEAD.index(h) for h in range(N_HEADS))


def _qk_layout():
    c = np.arange(ATTN_WIDTH)
    lane = c % LANES
    head = 2 * (c // LANES) + (lane // _HALF) % 2
    dim = lane % _HALF + _HALF * (lane // HEAD_DIM)
    return head, dim


def _rms(x, g):
    ms = jnp.mean(x * x, axis=-1, keepdims=True)
    return x * lax.rsqrt(ms + EPS) * g


def _sigmoid_of_twice(half_x):
    return 0.5 * jnp.tanh(half_x) + 0.5


def _lane_iota(shape):
    return lax.broadcasted_iota(jnp.int32, shape, len(shape) - 1)


def _layer_spec(l, shape, **kwargs):
    return pl.BlockSpec((None,) + tuple(shape), lambda *_: (l,) + (0,) * len(shape), **kwargs)


def _split_bf16(x):
    hi = x.astype(BF16)
    return hi, (x - hi.astype(F32)).astype(BF16)


def _swiglu_residual(x, g_ref, wgu_ref, wd_ref, o_ref):
    h = _rms(x, g_ref[...]).astype(BF16)
    acc = jnp.zeros(x.shape, F32)
    for c in range(D_FF // FFN_TF):
        a = jnp.dot(h, wgu_ref[:, c * FFN_TF:(c + 1) * FFN_TF], preferred_element_type=F32)
        b = jnp.dot(h, wgu_ref[:, D_FF + c * FFN_TF:D_FF + (c + 1) * FFN_TF],
                    preferred_element_type=F32)
        act = (a * jax.nn.sigmoid(a) * b).astype(BF16)
        acc = acc + jnp.dot(act, wd_ref[c * FFN_TF:(c + 1) * FFN_TF, :],
                            preferred_element_type=F32)
    o_ref[...] = x + 0.5 * acc


def _run_cast_jobs(src_refs, dst_refs):
    for src, dst in zip(src_refs, dst_refs):
        dst[...] = src[...].astype(dst.dtype)


def _ffn_kernel(x_ref, g_ref, wgu_ref, wd_ref, *rest):
    n_jobs = (len(rest) - 1) // 2
    o_ref = rest[n_jobs]
    _swiglu_residual(x_ref[...], g_ref, wgu_ref, wd_ref, o_ref)
    _run_cast_jobs(rest[:n_jobs], rest[n_jobs + 1:])


def _merge_ffn_kernel(x_ref, p_ref, g1_ref, ya_ref, wba_ref, wo_ref, g_ref, wgu_ref, wd_ref,
                      *rest):
    n_jobs = (len(rest) - 1) // 2
    o_ref = rest[n_jobs]
    merged = p_ref[...] + g1_ref[...] * jnp.dot(ya_ref[...], wba_ref[...],
                                                preferred_element_type=F32)
    x = x_ref[...] + jnp.dot(merged.astype(BF16), wo_ref[...], preferred_element_type=F32)
    _swiglu_residual(x, g_ref, wgu_ref, wd_ref, o_ref)
    _run_cast_jobs(rest[:n_jobs], rest[n_jobs + 1:])


def _cast_job_specs(jobs, n_steps):
    in_specs, out_specs, out_shapes = [], [], []
    for w, l in jobs:
        _, r, c = w.shape
        n_blocks = max(nb for nb in range(1, n_steps + 1)
                       if n_steps % nb == 0 and r % (nb * BF16_TILE_ROWS) == 0)
        rows = r // n_blocks
        in_specs.append(pl.BlockSpec(
            (None, rows, c), lambda i, l=l, n_blocks=n_blocks: (l, (i * n_blocks) // n_steps, 0)))
        out_specs.append(pl.BlockSpec(
            (rows, c), lambda i, n_blocks=n_blocks: ((i * n_blocks) // n_steps, 0)))
        out_shapes.append(jax.ShapeDtypeStruct((r, c), BF16))
    return in_specs, out_specs, out_shapes


def _resident(w):
    return pl.BlockSpec(w.shape, lambda *_: (0, 0), pipeline_mode=pl.Buffered(1))


def _merge_ffn(l, x2d, p2d, g12d, ya2d, wba, wo, g, wgu, wd, cast_jobs=()):
    t = x2d.shape[0]
    n_steps = t // FFN_TM
    row = pl.BlockSpec((FFN_TM, D_MODEL), lambda i: (i, 0))
    job_in, job_out, job_shapes = _cast_job_specs(cast_jobs, n_steps)
    outs = pl.pallas_call(
        _merge_ffn_kernel,
        grid=(n_steps,),
        in_specs=[
            row, row, row,
            pl.BlockSpec((FFN_TM, ATTN_WIDTH), lambda i: (i, 0)),
            _resident(wba), _resident(wo),
            _layer_spec(l, (1, D_MODEL)),
            _resident(wgu), _resident(wd),
        ] + job_in,
        out_specs=[row] + job_out,
        out_shape=[jax.ShapeDtypeStruct((t, D_MODEL), F32)] + job_shapes,
        compiler_params=pltpu.CompilerParams(
            dimension_semantics=("arbitrary",), vmem_limit_bytes=VMEM_LIMIT),
        name="merge_ffn",
    )(x2d, p2d, g12d, ya2d, wba, wo, g, wgu, wd, *[w for w, _ in cast_jobs])
    return outs[0], outs[1:]


def _ffn(l, x2d, g, wgu, wd, cast_jobs=()):
    t = x2d.shape[0]
    n_steps = t // FFN_TM
    row = pl.BlockSpec((FFN_TM, D_MODEL), lambda i: (i, 0))
    job_in, job_out, job_shapes = _cast_job_specs(cast_jobs, n_steps)
    outs = pl.pallas_call(
        _ffn_kernel,
        grid=(n_steps,),
        in_specs=[row, _layer_spec(l, (1, D_MODEL)), _resident(wgu), _resident(wd)] + job_in,
        out_specs=[row] + job_out,
        out_shape=[jax.ShapeDtypeStruct((t, D_MODEL), F32)] + job_shapes,
        compiler_params=pltpu.CompilerParams(
            dimension_semantics=("arbitrary",), vmem_limit_bytes=VMEM_LIMIT),
        name="ffn",
    )(x2d, g, wgu, wd, *[w for w, _ in cast_jobs])
    return outs[0], outs[1:]


def _head_rms_rope(x, gain, cos, sin_signed):
    even = (_lane_iota((x.shape[0], LANES)) // _HALF) % 2 == 0
    outs = []
    for p in range(ATTN_WIDTH // LANES):
        lanes = slice(p * LANES, (p + 1) * LANES)
        xp = x[:, lanes]
        sq = xp * xp
        ss_even = jnp.sum(jnp.where(even, sq, 0.0), axis=-1, keepdims=True)
        ss_odd = jnp.sum(jnp.where(even, 0.0, sq), axis=-1, keepdims=True)
        ms = jnp.where(even, ss_even, ss_odd) * (1.0 / HEAD_DIM)
        yp = xp * lax.rsqrt(ms + EPS) * gain[:, lanes]
        outs.append(yp * cos + pltpu.roll(yp, HEAD_DIM, 1) * sin_signed)
    return outs


def _window_sums(halo, u, steps):
    s = jnp.concatenate([halo, u], axis=0)
    for t in range(steps):
        s = s + pltpu.roll(s, 1 << t, 0)
    return s[HALO:, :]


def _choose_blocks(gate_t, i):
    nq = gate_t.shape[1]
    g3 = gate_t.reshape(N_HEADS, N_SLOTS, nq)
    slot = lax.broadcasted_iota(jnp.int32, g3.shape, 1)
    gm = jnp.where(slot < i, g3, NEG)
    rank = jnp.zeros(g3.shape, jnp.int32)
    for j in range(N_SLOTS):
        other = jnp.broadcast_to(gm[:, j:j + 1, :], g3.shape)
        ahead = (other > gm) | ((other == gm) & (slot > j))
        rank = rank + jnp.where(ahead, 1, 0)
    chosen = (rank < MOBA_TOP_K) & (slot < i)
    bias = jnp.where(chosen | (slot == i), 0.0, NEG)
    return bias.reshape(N_HEADS * N_SLOTS, nq)


def _inproj_rows(row0, x, mixg, win_ref, bg, poolw_ref, pools, qg, kg, cos, sin_signed, wbp_ref,
                 halo_ref):
    tm = x.shape[0]
    h = _rms(x, mixg).astype(BF16)

    u = jnp.dot(h, win_ref[:, 0:POOL_WIDTH], preferred_element_type=F32)
    pos = row0 + lax.broadcasted_iota(jnp.int32, (tm, 1), 0)
    ds = []
    for g, w in enumerate(POOL_WINDOWS):
        lanes = slice(g * POOL_GROUP, (g + 1) * POOL_GROUP)
        wsum = _window_sums(halo_ref[:, lanes], u[:, lanes], g + 1)
        cnt = jnp.minimum(pos + 1, w).astype(F32)
        ds.append((wsum / cnt - u[:, lanes]).astype(BF16))
    per_tile = MXU_TILE // POOL_GROUP
    ys = [jnp.dot(jnp.concatenate(ds[t * per_tile:(t + 1) * per_tile], axis=-1), poolw_ref[t],
                  preferred_element_type=F32) for t in range(len(ds) // per_tile)]
    y_pool = (jnp.concatenate(ys, axis=-1) * pools).astype(BF16)
    halo_ref[...] = u[tm - HALO:, :]

    gl = jnp.dot(h, win_ref[:, POOL_WIDTH + 3 * ATTN_WIDTH:], preferred_element_type=F32)
    gates = _sigmoid_of_twice(gl + bg)
    p_out = gates[:, :D_MODEL] * jnp.dot(y_pool, wbp_ref[...], preferred_element_type=F32)
    g1_out = gates[:, D_MODEL:]

    o1 = POOL_WIDTH
    q = jnp.dot(h, win_ref[:, o1:o1 + ATTN_WIDTH], preferred_element_type=F32)
    k = jnp.dot(h, win_ref[:, o1 + ATTN_WIDTH:o1 + 2 * ATTN_WIDTH], preferred_element_type=F32)
    v = jnp.dot(h, win_ref[:, o1 + 2 * ATTN_WIDTH:o1 + 3 * ATTN_WIDTH],
                preferred_element_type=F32)
    q_tiles = _head_rms_rope(q, qg, cos, sin_signed)
    k_tiles = _head_rms_rope(k, kg, cos, sin_signed)
    return p_out, g1_out, q_tiles, k_tiles, v


def _inproj_block(i, q_tiles, k_tiles, v, hmask_ref, kbt_ref):
    tm = v.shape[0]
    qn = jnp.concatenate(q_tiles, axis=-1)
    kn = jnp.concatenate(k_tiles, axis=-1)

    nt_dims = (((1,), (1,)), ((), ()))
    q_hi, q_lo = _split_bf16(qn)
    kb_hi, kb_lo = _split_bf16(kbt_ref[...])
    gate_t = (lax.dot_general(kb_hi, q_hi, nt_dims, preferred_element_type=F32)
              + lax.dot_general(kb_hi, q_lo, nt_dims, preferred_element_type=F32)
              + lax.dot_general(kb_lo, q_hi, nt_dims, preferred_element_type=F32))
    bias = _choose_blocks(gate_t, i).T

    kbar = jnp.sum(kn, axis=0, keepdims=True) * (1.0 / MOBA_BLOCK)
    row = lax.broadcasted_iota(jnp.int32, kbt_ref.shape, 0)
    mine = (row % N_SLOTS == i) & (hmask_ref[...] > 0.0)
    kbt_ref[...] = jnp.where(mine, kbar, kbt_ref[...])

    lane = _lane_iota((tm, LANES))
    ones_rows = jnp.where(lax.broadcasted_iota(jnp.int32, (VT_ROWS - HEAD_DIM, tm), 0) == 0,
                          1.0, 0.0)
    qa, ka, vat = [], [], []
    for hd in range(N_HEADS):
        p, odd, grp = hd // 2, hd % 2, _HEAD_GROUP[hd]
        own = (lane // _HALF) % 2 == odd
        vp = v[:, p * LANES:(p + 1) * LANES]
        qa.append(jnp.where(own, q_tiles[p] * Q_SCALE,
                            jnp.where(lane // N_SLOTS == grp, bias, 0.0)).astype(BF16))
        ka.append(jnp.where(own, k_tiles[p],
                            jnp.where(lane == grp * N_SLOTS + i, 1.0, 0.0)).astype(BF16))
        vt = vp.T[odd * HEAD_DIM:(odd + 1) * HEAD_DIM, :]
        vat.append(jnp.concatenate([vt, ones_rows], axis=0).astype(BF16))
    return qa, ka, vat


def _inproj_kernel(x_ref, mixg_ref, win_ref, bg_ref, poolw_ref, pools_ref, qg_ref, kg_ref,
                   cos_ref, sin_ref, wbp_ref, hmask_ref,
                   qa_ref, ka_ref, vat_ref, p_ref, g1_ref,
                   halo_ref, kbt_ref):
    step = pl.program_id(1)
    step_rows = x_ref.shape[1]

    @pl.when(step == 0)
    def _():
        halo_ref[...] = jnp.zeros(halo_ref.shape, F32)
        kbt_ref[...] = jnp.zeros(kbt_ref.shape, F32)

    for r0 in range(0, step_rows, MOBA_BLOCK):
        rows = slice(r0, r0 + MOBA_BLOCK)
        p_out, g1_out, q_tiles, k_tiles, v = _inproj_rows(
            step * step_rows + r0, x_ref[0, rows, :], mixg_ref[...], win_ref, bg_ref[...],
            poolw_ref, pools_ref[...], qg_ref[...], kg_ref[...], cos_ref[rows, :],
            sin_ref[rows, :], wbp_ref, halo_ref)
        p_ref[0, rows, :] = p_out
        g1_ref[0, rows, :] = g1_out
        qa, ka, vat = _inproj_block((step * step_rows + r0) // MOBA_BLOCK, q_tiles, k_tiles, v,
                                    hmask_ref, kbt_ref)
        for hd in range(N_HEADS):
            qa_ref[0, hd, rows, :] = qa[hd]
            ka_ref[0, hd, rows, :] = ka[hd]
            vat_ref[0, hd, :, rows] = vat[hd]


def _inproj(l, x, mixg, win, bg, poolw, pools, qg, kg, cos, sin_signed, wbp, hmask):
    b, s, _ = x.shape
    tm = INPROJ_TILES * MOBA_BLOCK
    const = lambda *shape: pl.BlockSpec(shape, lambda bi, i: (0,) * len(shape))
    head_spec = pl.BlockSpec((1, N_HEADS, tm, LANES), lambda bi, i: (bi, 0, i, 0))
    headt_spec = pl.BlockSpec((1, N_HEADS, VT_ROWS, tm), lambda bi, i: (bi, 0, 0, i))
    headt_shape = jax.ShapeDtypeStruct((b, N_HEADS, VT_ROWS, s), BF16)
    row_spec = pl.BlockSpec((1, tm, D_MODEL), lambda bi, i: (bi, i, 0))
    head_shape = jax.ShapeDtypeStruct((b, N_HEADS, s, LANES), BF16)
    row_shape = jax.ShapeDtypeStruct((b, s, D_MODEL), F32)
    return pl.pallas_call(
        _inproj_kernel,
        grid=(b, s // tm),
        in_specs=[
            row_spec,
            _layer_spec(l, (1, D_MODEL)),
            _layer_spec(l, win.shape[1:], pipeline_mode=pl.Buffered(1)),
            _layer_spec(l, (1, 2 * D_MODEL)),
            _layer_spec(l, poolw.shape[1:]),
            _layer_spec(l, (1, POOL_WIDTH)),
            _layer_spec(l, (1, ATTN_WIDTH)),
            _layer_spec(l, (1, ATTN_WIDTH)),
            pl.BlockSpec((tm, LANES), lambda bi, i: (i, 0)),
            pl.BlockSpec((tm, LANES), lambda bi, i: (i, 0)),
            _resident(wbp),
            const(N_HEADS * N_SLOTS, ATTN_WIDTH),
        ],
        out_specs=[head_spec, head_spec, headt_spec, row_spec, row_spec],
        out_shape=[head_shape, head_shape, headt_shape, row_shape, row_shape],
        scratch_shapes=[
            pltpu.VMEM((HALO, POOL_WIDTH), F32),
            pltpu.VMEM((N_HEADS * N_SLOTS, ATTN_WIDTH), F32),
        ],
        compiler_params=pltpu.CompilerParams(
            dimension_semantics=("arbitrary", "arbitrary"), vmem_limit_bytes=VMEM_LIMIT),
        name="inproj",
    )(x, mixg, win, bg, poolw, pools, qg, kg, cos, sin_signed, wbp, hmask)


def _score_tile(i, hh, j, masked, q, k_ref, s_ref):
    tq = MOBA_BLOCK
    s = lax.dot_general(k_ref[0, hh, j * tq:(j + 1) * tq, :], q, (((1,), (1,)), ((), ())),
                        preferred_element_type=F32)
    if masked:
        rel = (lax.broadcasted_iota(jnp.int32, (tq, tq), 0)
               - lax.broadcasted_iota(jnp.int32, (tq, tq), 1))
        s = jnp.where(rel <= (i - j) * tq, s, NEG)
    s_ref[j] = s
    return jnp.max(s.reshape(tq // SUBLANES, SUBLANES, tq), axis=0)


def _q_block(q_ref, hh, i):
    return q_ref[0, hh, pl.ds(pl.multiple_of(i * MOBA_BLOCK, MOBA_BLOCK), MOBA_BLOCK), :]


def _attn_block(i, nk, mrun, q_ref, k_ref, vt_ref, o_ref, s_refs):
    tq = MOBA_BLOCK
    q_rows = pl.ds(pl.multiple_of(i * tq, tq), tq)
    outs = []
    for hh in range(ATTN_HEADS):
        m = jnp.broadcast_to(jnp.max(mrun, axis=0, keepdims=True), (tq, tq))
        s_ref = s_refs[hh]
        if hh + 1 < ATTN_HEADS:
            nxt_h, nxt_i, nxt_nk = hh + 1, i, nk
        else:
            nxt_h, nxt_i = 0, jnp.minimum(i + 1, N_SLOTS - 1)
            nxt_nk = min(nk + ATTN_GROUP, N_SLOTS)
        q = _q_block(q_ref, nxt_h, nxt_i)
        mrun, pts = None, []
        for j in range(nxt_nk):
            tile_max = _score_tile(nxt_i, nxt_h, j, j >= nk - ATTN_GROUP, q, k_ref,
                                   s_refs[nxt_h])
            mrun = tile_max if mrun is None else jnp.maximum(mrun, tile_max)
            if j < nk:
                pts.append(jnp.exp2(s_ref[j] - m).astype(BF16))
        acc = jnp.dot(vt_ref[0, hh, :, 0:nk * tq], jnp.concatenate(pts, axis=0),
                      preferred_element_type=F32)
        outs.append(acc[:HEAD_DIM, :] / acc[HEAD_DIM:HEAD_DIM + 1, :])
        if hh % 2:
            pair = jnp.concatenate([outs[hh - 1], outs[hh]], axis=0)
            o_ref[0, q_rows, (hh // 2) * LANES:(hh // 2 + 1) * LANES] = pair.T.astype(o_ref.dtype)
    return mrun


def _attn_kernel(q_ref, k_ref, vt_ref, o_ref, *s_refs):
    mrun = None
    q = _q_block(q_ref, 0, 0)
    for j in range(ATTN_GROUP):
        tile_max = _score_tile(0, 0, j, True, q, k_ref, s_refs[0])
        mrun = tile_max if mrun is None else jnp.maximum(mrun, tile_max)
    for c in range(N_SLOTS // ATTN_GROUP):
        nk, first = ATTN_GROUP * (c + 1), ATTN_GROUP * c

        def body(i, mrun, nk=nk):
            return _attn_block(i, nk, mrun, q_ref, k_ref, vt_ref, o_ref, s_refs)
        mrun = lax.fori_loop(first, first + ATTN_GROUP, body, mrun)


def _attn(qa, ka, vat):
    b, nh, s, _ = qa.shape
    tq = MOBA_BLOCK
    qk_spec = pl.BlockSpec((1, ATTN_HEADS, s, LANES), lambda bi, p: (bi, p, 0, 0))
    vt_spec = pl.BlockSpec((1, ATTN_HEADS, VT_ROWS, s), lambda bi, p: (bi, p, 0, 0))
    return pl.pallas_call(
        _attn_kernel,
        grid=(b, nh // ATTN_HEADS),
        in_specs=[qk_spec, qk_spec, vt_spec],
        out_specs=pl.BlockSpec((1, s, HEAD_DIM * ATTN_HEADS), lambda bi, p: (bi, 0, p)),
        out_shape=jax.ShapeDtypeStruct((b, s, ATTN_WIDTH), BF16),
        scratch_shapes=[pltpu.VMEM((s // tq, tq, tq), F32)] * ATTN_HEADS,
        compiler_params=pltpu.CompilerParams(
            dimension_semantics=("arbitrary", "arbitrary"),
            vmem_limit_bytes=VMEM_LIMIT),
        name="attn",
    )(qa, ka, vat)


def _rope_tables(s):
    inv_freq = 1.0 / ROPE_THETA ** (np.arange(_HALF, dtype=np.float64) * (2.0 / HEAD_DIM))
    ang = np.arange(s, dtype=np.float64)[:, None] * inv_freq[None, :]
    cos, sin = np.cos(ang), np.sin(ang)
    cos = np.tile(cos, (1, LANES // _HALF))
    sin_signed = np.concatenate([-sin, -sin, sin, sin], axis=-1)
    return jnp.asarray(cos, F32), jnp.asarray(sin_signed, F32)


def _permute_heads(w):
    lead = w.shape[:-1]
    w = w.reshape(lead + (N_HEADS // 2, 2, 2, _HALF))
    return jnp.swapaxes(w, -3, -2).reshape(lead + (ATTN_WIDTH,))


def kernel(x, ffn1_norm, ffn1_w_gate_up, ffn1_w_down, mix_norm, w_in, b_gate, pool_w, pool_scale,
           q_norm, k_norm, w_branch_pool, w_branch_attn, w_out, ffn2_norm, ffn2_w_gate_up,
           ffn2_w_down):
    b, s, d = x.shape
    assert s == N_SLOTS * MOBA_BLOCK and d == D_MODEL
    depth = ffn1_norm.shape[0]
    cos, sin_signed = _rope_tables(s)
    head, dim = _qk_layout()
    hmask = jnp.asarray(np.repeat(np.asarray(_GROUP_HEAD), N_SLOTS)[:, None] == head[None, :], F32)
    o1, o2, o3 = POOL_WIDTH, POOL_WIDTH + ATTN_WIDTH, POOL_WIDTH + 2 * ATTN_WIDTH
    o4 = o3 + ATTN_WIDTH

    row = lambda p: p[:, None, :]
    win = jnp.concatenate([w_in[..., :o1], _permute_heads(w_in[..., o1:o2]),
                           _permute_heads(w_in[..., o2:o3]), w_in[..., o3:o4],
                           0.5 * w_in[..., o4:]], axis=-1).astype(BF16)
    qg = row(q_norm[:, dim])
    kg = row(k_norm[:, dim])
    per_tile = MXU_TILE // POOL_GROUP
    n_tiles = len(POOL_WINDOWS) // per_tile
    poolw = jnp.einsum('ltaij,ab->ltaibj',
                       pool_w.reshape(depth, n_tiles, per_tile, POOL_GROUP, POOL_GROUP),
                       jnp.eye(per_tile, dtype=F32)
                       ).reshape(depth, n_tiles, MXU_TILE, MXU_TILE).astype(BF16)

    wgu, wd = ffn1_w_gate_up[0].astype(BF16), ffn1_w_down[0].astype(BF16)
    x2d = x.reshape(b * s, d)
    for l in range(depth):
        mixer_jobs = [(ffn2_w_gate_up, l), (ffn2_w_down, l), (w_branch_pool, l),
                      (w_branch_attn, l), (w_out, l)]
        x2d, (wgu2, wd2, wbp, wba, wo) = _ffn(l, x2d, row(ffn1_norm), wgu, wd, mixer_jobs)
        qa, ka, vat, p, g1 = _inproj(
            l, x2d.reshape(b, s, d), row(mix_norm), win, row(0.5 * b_gate), poolw, row(pool_scale),
            qg, kg, cos, sin_signed, wbp, hmask)
        ya = _attn(qa, ka, vat)
        next_jobs = [(ffn1_w_gate_up, l + 1), (ffn1_w_down, l + 1)] if l + 1 < depth else []
        x2d, nxt = _merge_ffn(l, x2d, p.reshape(b * s, d), g1.reshape(b * s, d),
                              ya.reshape(b * s, ATTN_WIDTH), wba, wo, row(ffn2_norm), wgu2, wd2,
                              next_jobs)
        if nxt:
            wgu, wd = nxt
    return x2d.reshape(b, s, d)
```

```python
import functools

import numpy as np

import jax
import jax.numpy as jnp
from jax import lax
from jax.experimental import pallas as pl
from jax.experimental.pallas import tpu as pltpu

F32 = jnp.float32
BF16 = jnp.bfloat16

D_MODEL = 1024
D_FF = 2816
POOL_WINDOWS = (2, 4, 8, 16)
POOL_WIDTH = 512
POOL_GROUP = 128
N_HEADS = 8
HEAD_DIM = 64
ATTN_WIDTH = 512
MOBA_BLOCK = 256
MOBA_TOP_K = 3
ROPE_THETA = 10000.0
EPS = 1e-6
NEG = -1e30

LANES = 128
SUBLANES = 8
MXU_TILE = 256
BF16_TILE_ROWS = 16
VT_ROWS = 80
HALO = 16
N_SLOTS = 16
FFN_TM = 512
FFN_TF = 256
INPROJ_TILES = 4
ATTN_GROUP = 2
ATTN_HEADS = 4
VMEM_LIMIT = 60 * 1024 * 1024

_HALF = HEAD_DIM // 2
Q_SCALE = float(HEAD_DIM ** -0.5 * np.log2(np.e))
_GROUP_HEAD = (1, 3, 0, 2, 5, 7, 4, 6)
_HEAD_GROUP = tuple(_GROUP_HEAD.index(h) for h in range(N_HEADS))


def _qk_layout():
    c = np.arange(ATTN_WIDTH)
    lane = c % LANES
    head = 2 * (c // LANES) + (lane // _HALF) % 2
    dim = lane % _HALF + _HALF * (lane // HEAD_DIM)
    return head, dim


def _rms(x, g):
    ms = jnp.mean(x * x, axis=-1, keepdims=True)
    return x * lax.rsqrt(ms + EPS) * g


def _sigmoid_of_twice(half_x):
    return 0.5 * jnp.tanh(half_x) + 0.5


def _lane_iota(shape):
    return lax.broadcasted_iota(jnp.int32, shape, len(shape) - 1)


def _layer_spec(l, shape, **kwargs):
    return pl.BlockSpec((None,) + tuple(shape), lambda *_: (l,) + (0,) * len(shape), **kwargs)


def _split_bf16(x):
    hi = x.astype(BF16)
    return hi, (x - hi.astype(F32)).astype(BF16)


def _swiglu_residual(x, g_ref, wgu_ref, wd_ref, o_ref):
    h = _rms(x, g_ref[...]).astype(BF16)
    acc = jnp.zeros(x.shape, F32)
    for c in range(D_FF // FFN_TF):
        a = jnp.dot(h, wgu_ref[:, c * FFN_TF:(c + 1) * FFN_TF], preferred_element_type=F32)
        b = jnp.dot(h, wgu_ref[:, D_FF + c * FFN_TF:D_FF + (c + 1) * FFN_TF],
                    preferred_element_type=F32)
        act = (a * jax.nn.sigmoid(a) * b).astype(BF16)
        acc = acc + jnp.dot(act, wd_ref[c * FFN_TF:(c + 1) * FFN_TF, :],
                            preferred_element_type=F32)
    o_ref[...] = x + 0.5 * acc


def _run_cast_jobs(src_refs, dst_refs, prepares):
    for src, dst, prepare in zip(src_refs, dst_refs, prepares):
        w = src[...]
        dst[...] = (w if prepare is None else prepare(w)).astype(dst.dtype)


def _prepare_w_in(w):
    lane = _lane_iota((w.shape[0], LANES))
    second = (lane >= _HALF) & (lane < HEAD_DIM)
    third = (lane >= HEAD_DIM) & (lane < HEAD_DIM + _HALF)
    tiles = []
    for c in range(0, w.shape[1], LANES):
        t = w[:, c:c + LANES]
        if POOL_WIDTH <= c < POOL_WIDTH + 2 * ATTN_WIDTH:
            t = jnp.where(second, pltpu.roll(t, LANES - _HALF, 1),
                          jnp.where(third, pltpu.roll(t, _HALF, 1), t))
        elif c >= POOL_WIDTH + 3 * ATTN_WIDTH:
            t = 0.5 * t
        tiles.append(t)
    return jnp.concatenate(tiles, axis=-1)


def _ffn_kernel(x_ref, g_ref, wgu_ref, wd_ref, *rest, prepares):
    n_jobs = len(prepares)
    o_ref = rest[n_jobs]
    _swiglu_residual(x_ref[...], g_ref, wgu_ref, wd_ref, o_ref)
    _run_cast_jobs(rest[:n_jobs], rest[n_jobs + 1:], prepares)


def _merge_ffn_kernel(x_ref, p_ref, g1_ref, ya_ref, wba_ref, wo_ref, g_ref, wgu_ref, wd_ref,
                      *rest, prepares):
    n_jobs = len(prepares)
    o_ref = rest[n_jobs]
    merged = p_ref[...] + g1_ref[...] * jnp.dot(ya_ref[...], wba_ref[...],
                                                preferred_element_type=F32)
    x = x_ref[...] + jnp.dot(merged.astype(BF16), wo_ref[...], preferred_element_type=F32)
    _swiglu_residual(x, g_ref, wgu_ref, wd_ref, o_ref)
    _run_cast_jobs(rest[:n_jobs], rest[n_jobs + 1:], prepares)


def _cast_job_specs(jobs, n_steps):
    in_specs, out_specs, out_shapes = [], [], []
    for w, l, _ in jobs:
        _, r, c = w.shape
        n_blocks = max(nb for nb in range(1, n_steps + 1)
                       if n_steps % nb == 0 and r % (nb * BF16_TILE_ROWS) == 0)
        rows = r // n_blocks
        in_specs.append(pl.BlockSpec(
            (None, rows, c), lambda i, l=l, n_blocks=n_blocks: (l, (i * n_blocks) // n_steps, 0)))
        out_specs.append(pl.BlockSpec(
            (rows, c), lambda i, n_blocks=n_blocks: ((i * n_blocks) // n_steps, 0)))
        out_shapes.append(jax.ShapeDtypeStruct((r, c), BF16))
    return in_specs, out_specs, out_shapes


def _resident(w):
    return pl.BlockSpec(w.shape, lambda *_: (0, 0), pipeline_mode=pl.Buffered(1))


def _merge_ffn(l, x2d, p2d, g12d, ya2d, wba, wo, g, wgu, wd, cast_jobs=()):
    t = x2d.shape[0]
    n_steps = t // FFN_TM
    row = pl.BlockSpec((FFN_TM, D_MODEL), lambda i: (i, 0))
    job_in, job_out, job_shapes = _cast_job_specs(cast_jobs, n_steps)
    outs = pl.pallas_call(
        functools.partial(_merge_ffn_kernel, prepares=tuple(p for _, _, p in cast_jobs)),
        grid=(n_steps,),
        in_specs=[
            row, row, row,
            pl.BlockSpec((FFN_TM, ATTN_WIDTH), lambda i: (i, 0)),
            _resident(wba), _resident(wo),
            _layer_spec(l, (1, D_MODEL)),
            _resident(wgu), _resident(wd),
        ] + job_in,
        out_specs=[row] + job_out,
        out_shape=[jax.ShapeDtypeStruct((t, D_MODEL), F32)] + job_shapes,
        compiler_params=pltpu.CompilerParams(
            dimension_semantics=("arbitrary",), vmem_limit_bytes=VMEM_LIMIT),
        name="merge_ffn",
    )(x2d, p2d, g12d, ya2d, wba, wo, g, wgu, wd, *[w for w, _, _ in cast_jobs])
    return outs[0], outs[1:]


def _ffn(l, x2d, g, wgu, wd, cast_jobs=()):
    t = x2d.shape[0]
    n_steps = t // FFN_TM
    row = pl.BlockSpec((FFN_TM, D_MODEL), lambda i: (i, 0))
    job_in, job_out, job_shapes = _cast_job_specs(cast_jobs, n_steps)
    outs = pl.pallas_call(
        functools.partial(_ffn_kernel, prepares=tuple(p for _, _, p in cast_jobs)),
        grid=(n_steps,),
        in_specs=[row, _layer_spec(l, (1, D_MODEL)), _resident(wgu), _resident(wd)] + job_in,
        out_specs=[row] + job_out,
        out_shape=[jax.ShapeDtypeStruct((t, D_MODEL), F32)] + job_shapes,
        compiler_params=pltpu.CompilerParams(
            dimension_semantics=("arbitrary",), vmem_limit_bytes=VMEM_LIMIT),
        name="ffn",
    )(x2d, g, wgu, wd, *[w for w, _, _ in cast_jobs])
    return outs[0], outs[1:]


def _head_rms_rope(x, gain, cos, sin_signed):
    even = (_lane_iota((x.shape[0], LANES)) // _HALF) % 2 == 0
    outs = []
    for p in range(ATTN_WIDTH // LANES):
        lanes = slice(p * LANES, (p + 1) * LANES)
        xp = x[:, lanes]
        sq = xp * xp
        ss_even = jnp.sum(jnp.where(even, sq, 0.0), axis=-1, keepdims=True)
        ss_odd = jnp.sum(jnp.where(even, 0.0, sq), axis=-1, keepdims=True)
        ms = jnp.where(even, ss_even, ss_odd) * (1.0 / HEAD_DIM)
        yp = xp * lax.rsqrt(ms + EPS) * gain[:, lanes]
        outs.append(yp * cos + pltpu.roll(yp, HEAD_DIM, 1) * sin_signed)
    return outs


def _window_sums(halo, u, steps):
    s = jnp.concatenate([halo, u], axis=0)
    for t in range(steps):
        s = s + pltpu.roll(s, 1 << t, 0)
    return s[HALO:, :]


def _choose_blocks(gate_t, i):
    nq = gate_t.shape[1]
    g3 = gate_t.reshape(N_HEADS, N_SLOTS, nq)
    slot = lax.broadcasted_iota(jnp.int32, g3.shape, 1)
    gm = jnp.where(slot < i, g3, NEG)
    rank = jnp.zeros(g3.shape, jnp.int32)
    for j in range(N_SLOTS):
        other = jnp.broadcast_to(gm[:, j:j + 1, :], g3.shape)
        ahead = (other > gm) | ((other == gm) & (slot > j))
        rank = rank + jnp.where(ahead, 1, 0)
    chosen = (rank < MOBA_TOP_K) & (slot < i)
    bias = jnp.where(chosen | (slot == i), 0.0, NEG)
    return bias.reshape(N_HEADS * N_SLOTS, nq)


def _inproj_rows(row0, x, mixg, win_ref, bg, poolw_ref, pools, qg, kg, cos, sin_signed, wbp_ref,
                 halo_ref):
    tm = x.shape[0]
    h = _rms(x, mixg).astype(BF16)

    u = jnp.dot(h, win_ref[:, 0:POOL_WIDTH], preferred_element_type=F32)
    pos = row0 + lax.broadcasted_iota(jnp.int32, (tm, 1), 0)
    ds = []
    for g, w in enumerate(POOL_WINDOWS):
        lanes = slice(g * POOL_GROUP, (g + 1) * POOL_GROUP)
        wsum = _window_sums(halo_ref[:, lanes], u[:, lanes], g + 1)
        cnt = jnp.minimum(pos + 1, w).astype(F32)
        ds.append((wsum / cnt - u[:, lanes]).astype(BF16))
    per_tile = MXU_TILE // POOL_GROUP
    ys = [jnp.dot(jnp.concatenate(ds[t * per_tile:(t + 1) * per_tile], axis=-1), poolw_ref[t],
                  preferred_element_type=F32) for t in range(len(ds) // per_tile)]
    y_pool = (jnp.concatenate(ys, axis=-1) * pools).astype(BF16)
    halo_ref[...] = u[tm - HALO:, :]

    gl = jnp.dot(h, win_ref[:, POOL_WIDTH + 3 * ATTN_WIDTH:], preferred_element_type=F32)
    gates = _sigmoid_of_twice(gl + bg)
    p_out = gates[:, :D_MODEL] * jnp.dot(y_pool, wbp_ref[...], preferred_element_type=F32)
    g1_out = gates[:, D_MODEL:]

    o1 = POOL_WIDTH
    q = jnp.dot(h, win_ref[:, o1:o1 + ATTN_WIDTH], preferred_element_type=F32)
    k = jnp.dot(h, win_ref[:, o1 + ATTN_WIDTH:o1 + 2 * ATTN_WIDTH], preferred_element_type=F32)
    v = jnp.dot(h, win_ref[:, o1 + 2 * ATTN_WIDTH:o1 + 3 * ATTN_WIDTH],
                preferred_element_type=F32)
    q_tiles = _head_rms_rope(q, qg, cos, sin_signed)
    k_tiles = _head_rms_rope(k, kg, cos, sin_signed)
    return p_out, g1_out, q_tiles, k_tiles, v


def _inproj_block(i, q_tiles, k_tiles, v, hmask_ref, kbt_ref):
    tm = v.shape[0]
    qn = jnp.concatenate(q_tiles, axis=-1)
    kn = jnp.concatenate(k_tiles, axis=-1)

    nt_dims = (((1,), (1,)), ((), ()))
    q_hi, q_lo = _split_bf16(qn)
    kb_hi, kb_lo = _split_bf16(kbt_ref[...])
    gate_t = (lax.dot_general(kb_hi, q_hi, nt_dims, preferred_element_type=F32)
              + lax.dot_general(kb_hi, q_lo, nt_dims, preferred_element_type=F32)
              + lax.dot_general(kb_lo, q_hi, nt_dims, preferred_element_type=F32))
    bias = _choose_blocks(gate_t, i).T

    kbar = jnp.sum(kn, axis=0, keepdims=True) * (1.0 / MOBA_BLOCK)
    row = lax.broadcasted_iota(jnp.int32, kbt_ref.shape, 0)
    mine = (row % N_SLOTS == i) & (hmask_ref[...] > 0.0)
    kbt_ref[...] = jnp.where(mine, kbar, kbt_ref[...])

    lane = _lane_iota((tm, LANES))
    ones_rows = jnp.where(lax.broadcasted_iota(jnp.int32, (VT_ROWS - HEAD_DIM, tm), 0) == 0,
                          1.0, 0.0)
    qa, ka, vat = [], [], []
    for hd in range(N_HEADS):
        p, odd, grp = hd // 2, hd % 2, _HEAD_GROUP[hd]
        own = (lane // _HALF) % 2 == odd
        vp = v[:, p * LANES:(p + 1) * LANES]
        qa.append(jnp.where(own, q_tiles[p] * Q_SCALE,
                            jnp.where(lane // N_SLOTS == grp, bias, 0.0)).astype(BF16))
        ka.append(jnp.where(own, k_tiles[p],
                            jnp.where(lane == grp * N_SLOTS + i, 1.0, 0.0)).astype(BF16))
        vt = vp.T[odd * HEAD_DIM:(odd + 1) * HEAD_DIM, :]
        vat.append(jnp.concatenate([vt, ones_rows], axis=0).astype(BF16))
    return qa, ka, vat


def _inproj_kernel(x_ref, mixg_ref, win_ref, bg_ref, poolw_ref, pools_ref, qg_ref, kg_ref,
                   cos_ref, sin_ref, wbp_ref, hmask_ref,
                   qa_ref, ka_ref, vat_ref, p_ref, g1_ref,
                   halo_ref, kbt_ref):
    step = pl.program_id(1)
    step_rows = x_ref.shape[1]

    @pl.when(step == 0)
    def _():
        halo_ref[...] = jnp.zeros(halo_ref.shape, F32)
        kbt_ref[...] = jnp.zeros(kbt_ref.shape, F32)

    for r0 in range(0, step_rows, MOBA_BLOCK):
        rows = slice(r0, r0 + MOBA_BLOCK)
        p_out, g1_out, q_tiles, k_tiles, v = _inproj_rows(
            step * step_rows + r0, x_ref[0, rows, :], mixg_ref[...], win_ref, bg_ref[...],
            poolw_ref, pools_ref[...], qg_ref[...], kg_ref[...], cos_ref[rows, :],
            sin_ref[rows, :], wbp_ref, halo_ref)
        p_ref[0, rows, :] = p_out
        g1_ref[0, rows, :] = g1_out
        qa, ka, vat = _inproj_block((step * step_rows + r0) // MOBA_BLOCK, q_tiles, k_tiles, v,
                                    hmask_ref, kbt_ref)
        for hd in range(N_HEADS):
            qa_ref[0, hd, rows, :] = qa[hd]
            ka_ref[0, hd, rows, :] = ka[hd]
            vat_ref[0, hd, :, rows] = vat[hd]


def _inproj(l, x, mixg, win, bg, poolw, pools, qg, kg, cos, sin_signed, wbp, hmask):
    b, s, _ = x.shape
    tm = INPROJ_TILES * MOBA_BLOCK
    const = lambda *shape: pl.BlockSpec(shape, lambda bi, i: (0,) * len(shape))
    head_spec = pl.BlockSpec((1, N_HEADS, tm, LANES), lambda bi, i: (bi, 0, i, 0))
    headt_spec = pl.BlockSpec((1, N_HEADS, VT_ROWS, tm), lambda bi, i: (bi, 0, 0, i))
    headt_shape = jax.ShapeDtypeStruct((b, N_HEADS, VT_ROWS, s), BF16)
    row_spec = pl.BlockSpec((1, tm, D_MODEL), lambda bi, i: (bi, i, 0))
    head_shape = jax.ShapeDtypeStruct((b, N_HEADS, s, LANES), BF16)
    row_shape = jax.ShapeDtypeStruct((b, s, D_MODEL), F32)
    return pl.pallas_call(
        _inproj_kernel,
        grid=(b, s // tm),
        in_specs=[
            row_spec,
            _layer_spec(l, (1, D_MODEL)),
            _resident(win),
            _layer_spec(l, (1, 2 * D_MODEL)),
            _layer_spec(l, poolw.shape[1:]),
            _layer_spec(l, (1, POOL_WIDTH)),
            _layer_spec(l, (1, ATTN_WIDTH)),
            _layer_spec(l, (1, ATTN_WIDTH)),
            pl.BlockSpec((tm, LANES), lambda bi, i: (i, 0)),
            pl.BlockSpec((tm, LANES), lambda bi, i: (i, 0)),
            _resident(wbp),
            const(N_HEADS * N_SLOTS, ATTN_WIDTH),
        ],
        out_specs=[head_spec, head_spec, headt_spec, row_spec, row_spec],
        out_shape=[head_shape, head_shape, headt_shape, row_shape, row_shape],
        scratch_shapes=[
            pltpu.VMEM((HALO, POOL_WIDTH), F32),
            pltpu.VMEM((N_HEADS * N_SLOTS, ATTN_WIDTH), F32),
        ],
        compiler_params=pltpu.CompilerParams(
            dimension_semantics=("arbitrary", "arbitrary"), vmem_limit_bytes=VMEM_LIMIT),
        name="inproj",
    )(x, mixg, win, bg, poolw, pools, qg, kg, cos, sin_signed, wbp, hmask)


def _score_tile(i, hh, j, masked, q, k_ref, s_ref):
    tq = MOBA_BLOCK
    s = lax.dot_general(k_ref[0, hh, j * tq:(j + 1) * tq, :], q, (((1,), (1,)), ((), ())),
                        preferred_element_type=F32)
    if masked:
        rel = (lax.broadcasted_iota(jnp.int32, (tq, tq), 0)
               - lax.broadcasted_iota(jnp.int32, (tq, tq), 1))
        s = jnp.where(rel <= (i - j) * tq, s, NEG)
    s_ref[j] = s
    return jnp.max(s.reshape(tq // SUBLANES, SUBLANES, tq), axis=0)


def _q_block(q_ref, hh, i):
    return q_ref[0, hh, pl.ds(pl.multiple_of(i * MOBA_BLOCK, MOBA_BLOCK), MOBA_BLOCK), :]


def _attn_block(i, nk, mrun, q_ref, k_ref, vt_ref, o_ref, s_refs):
    tq = MOBA_BLOCK
    q_rows = pl.ds(pl.multiple_of(i * tq, tq), tq)
    outs = []
    for hh in range(ATTN_HEADS):
        m = jnp.broadcast_to(jnp.max(mrun, axis=0, keepdims=True), (tq, tq))
        s_ref = s_refs[hh]
        if hh + 1 < ATTN_HEADS:
            nxt_h, nxt_i, nxt_nk = hh + 1, i, nk
        else:
            nxt_h, nxt_i = 0, jnp.minimum(i + 1, N_SLOTS - 1)
            nxt_nk = min(nk + ATTN_GROUP, N_SLOTS)
        q = _q_block(q_ref, nxt_h, nxt_i)
        mrun, pts = None, []
        for j in range(nxt_nk):
            tile_max = _score_tile(nxt_i, nxt_h, j, j >= nk - ATTN_GROUP, q, k_ref,
                                   s_refs[nxt_h])
            mrun = tile_max if mrun is None else jnp.maximum(mrun, tile_max)
            if j < nk:
                pts.append(jnp.exp2(s_ref[j] - m).astype(BF16))
        acc = jnp.dot(vt_ref[0, hh, :, 0:nk * tq], jnp.concatenate(pts, axis=0),
                      preferred_element_type=F32)
        outs.append(acc[:HEAD_DIM, :] / acc[HEAD_DIM:HEAD_DIM + 1, :])
        if hh % 2:
            pair = jnp.concatenate([outs[hh - 1], outs[hh]], axis=0)
            o_ref[0, q_rows, (hh // 2) * LANES:(hh // 2 + 1) * LANES] = pair.T.astype(o_ref.dtype)
    return mrun


def _attn_kernel(q_ref, k_ref, vt_ref, o_ref, *s_refs):
    mrun = None
    q = _q_block(q_ref, 0, 0)
    for j in range(ATTN_GROUP):
        tile_max = _score_tile(0, 0, j, True, q, k_ref, s_refs[0])
        mrun = tile_max if mrun is None else jnp.maximum(mrun, tile_max)
    for c in range(N_SLOTS // ATTN_GROUP):
        nk, first = ATTN_GROUP * (c + 1), ATTN_GROUP * c

        def body(i, mrun, nk=nk):
            return _attn_block(i, nk, mrun, q_ref, k_ref, vt_ref, o_ref, s_refs)
        mrun = lax.fori_loop(first, first + ATTN_GROUP, body, mrun)


def _attn(qa, ka, vat):
    b, nh, s, _ = qa.shape
    tq = MOBA_BLOCK
    qk_spec = pl.BlockSpec((1, ATTN_HEADS, s, LANES), lambda bi, p: (bi, p, 0, 0))
    vt_spec = pl.BlockSpec((1, ATTN_HEADS, VT_ROWS, s), lambda bi, p: (bi, p, 0, 0))
    return pl.pallas_call(
        _attn_kernel,
        grid=(b, nh // ATTN_HEADS),
        in_specs=[qk_spec, qk_spec, vt_spec],
        out_specs=pl.BlockSpec((1, s, HEAD_DIM * ATTN_HEADS), lambda bi, p: (bi, 0, p)),
        out_shape=jax.ShapeDtypeStruct((b, s, ATTN_WIDTH), BF16),
        scratch_shapes=[pltpu.VMEM((s // tq, tq, tq), F32)] * ATTN_HEADS,
        compiler_params=pltpu.CompilerParams(
            dimension_semantics=("arbitrary", "arbitrary"),
            vmem_limit_bytes=VMEM_LIMIT),
        name="attn",
    )(qa, ka, vat)


def _rope_tables(s):
    inv_freq = 1.0 / ROPE_THETA ** (np.arange(_HALF, dtype=np.float64) * (2.0 / HEAD_DIM))
    ang = np.arange(s, dtype=np.float64)[:, None] * inv_freq[None, :]
    cos, sin = np.cos(ang), np.sin(ang)
    cos = np.tile(cos, (1, LANES // _HALF))
    sin_signed = np.concatenate([-sin, -sin, sin, sin], axis=-1)
    return jnp.asarray(cos, F32), jnp.asarray(sin_signed, F32)


def kernel(x, ffn1_norm, ffn1_w_gate_up, ffn1_w_down, mix_norm, w_in, b_gate, pool_w, pool_scale,
           q_norm, k_norm, w_branch_pool, w_branch_attn, w_out, ffn2_norm, ffn2_w_gate_up,
           ffn2_w_down):
    b, s, d = x.shape
    assert s == N_SLOTS * MOBA_BLOCK and d == D_MODEL
    depth = ffn1_norm.shape[0]
    cos, sin_signed = _rope_tables(s)
    head, dim = _qk_layout()
    hmask = jnp.asarray(np.repeat(np.asarray(_GROUP_HEAD), N_SLOTS)[:, None] == head[None, :], F32)

    row = lambda p: p[:, None, :]
    qg = row(q_norm[:, dim])
    kg = row(k_norm[:, dim])
    per_tile = MXU_TILE // POOL_GROUP
    n_tiles = len(POOL_WINDOWS) // per_tile
    poolw = jnp.einsum('ltaij,ab->ltaibj',
                       pool_w.reshape(depth, n_tiles, per_tile, POOL_GROUP, POOL_GROUP),
                       jnp.eye(per_tile, dtype=F32)
                       ).reshape(depth, n_tiles, MXU_TILE, MXU_TILE).astype(BF16)

    wgu, wd = ffn1_w_gate_up[0].astype(BF16), ffn1_w_down[0].astype(BF16)
    x2d = x.reshape(b * s, d)
    for l in range(depth):
        mixer_jobs = [(ffn2_w_gate_up, l, None), (ffn2_w_down, l, None), (w_in, l, _prepare_w_in),
                      (w_branch_pool, l, None), (w_branch_attn, l, None), (w_out, l, None)]
        x2d, (wgu2, wd2, win, wbp, wba, wo) = _ffn(l, x2d, row(ffn1_norm), wgu, wd, mixer_jobs)
        qa, ka, vat, p, g1 = _inproj(
            l, x2d.reshape(b, s, d), row(mix_norm), win, row(0.5 * b_gate), poolw, row(pool_scale),
            qg, kg, cos, sin_signed, wbp, hmask)
        ya = _attn(qa, ka, vat)
        next_jobs = ([(ffn1_w_gate_up, l + 1, None), (ffn1_w_down, l + 1, None)]
                     if l + 1 < depth else [])
        x2d, nxt = _merge_ffn(l, x2d, p.reshape(b * s, d), g1.reshape(b * s, d),
                              ya.reshape(b * s, ATTN_WIDTH), wba, wo, row(ffn2_norm), wgu2, wd2,
                              next_jobs)
        if nxt:
            wgu, wd = nxt
    return x2d.reshape(b, s, d)
```
